```python
import math
import jax, jax.numpy as jnp
from jax import lax
import numpy as np

D_MODEL = 1024
BATCH = 1
SEQ = 16384
DEPTH = 1
DEC_BATCH = 32
DEC_SEQ = 4
PAST_LEN = 16384
PAGE_SIZE = 128

SSM_HEAD_DIM = 64
SSM_WIDTH = D_MODEL
SSM_HEADS = SSM_WIDTH // SSM_HEAD_DIM
SSM_GROUPS = 2
SSM_HG = SSM_HEADS // SSM_GROUPS
SSM_STATE = 128
SSM_CHUNK = 128
CONV_W = 4
CONV_DIM = SSM_WIDTH + 2 * SSM_GROUPS * SSM_STATE
ATT_HEAD_DIM = 64
ATT_WIDTH = D_MODEL // 2
ATT_HEADS = ATT_WIDTH // ATT_HEAD_DIM
MOBA_BLOCK = 256
MOBA_TOPK = 3
Q_BLOCK = 128
RPE_BUCKETS = 32
RPE_MAX_DIST = 128
X_HEADS = 4
X_WIDTH = D_MODEL // 2
X_HEAD_DIM = X_WIDTH // X_HEADS
N_MEM = 256
MIX_WIDTH = SSM_WIDTH + ATT_WIDTH + X_WIDTH
IN_SPLITS = (SSM_WIDTH, CONV_DIM, SSM_HEADS, ATT_WIDTH, ATT_WIDTH, ATT_WIDTH, ATT_WIDTH, X_WIDTH, X_WIDTH)
IN_WIDTH = sum(IN_SPLITS)
EPS = 1e-6

kernel_name = 'hymba_ssd_moba_xmem_step'


def rmsnorm(x, g):
    xf = x.astype(jnp.float32)
    return (xf * lax.rsqrt(jnp.mean(xf * xf, axis=-1, keepdims=True) + EPS)).astype(x.dtype) * g


def split_in_proj(u):
    cuts = np.cumsum(IN_SPLITS)[:-1].tolist()
    return jnp.split(u, cuts, axis=-1)


def causal_dwconv(u, buf, w, b):
    ext = jnp.concatenate([buf.astype(u.dtype), u], axis=1)
    out = lax.conv_general_dilated(ext, w[:, None, :].astype(u.dtype), (1,), 'VALID',
                                   dimension_numbers=('NWC', 'WIO', 'NWC'),
                                   feature_group_count=u.shape[-1])
    return out + b, ext[:, -(CONV_W - 1):]


def ssd_inputs(xbc, dt_raw, conv_buf, conv_w, conv_b, dt_bias, a_log):
    Bsz, T = xbc.shape[:2]
    u, new_buf = causal_dwconv(xbc, conv_buf, conv_w, conv_b)
    u = jax.nn.silu(u).astype(jnp.float32)
    xs, Bm, Cm = jnp.split(u, [SSM_WIDTH, SSM_WIDTH + SSM_GROUPS * SSM_STATE], axis=-1)
    xs = xs.reshape(Bsz, T, SSM_GROUPS, SSM_HG, SSM_HEAD_DIM)
    Bm = Bm.reshape(Bsz, T, SSM_GROUPS, SSM_STATE)
    Cm = Cm.reshape(Bsz, T, SSM_GROUPS, SSM_STATE)
    dt = jax.nn.softplus(dt_raw.astype(jnp.float32) + dt_bias.astype(jnp.float32)).reshape(Bsz, T, SSM_GROUPS, SSM_HG)
    A = -jnp.exp(a_log.astype(jnp.float32)).reshape(SSM_GROUPS, SSM_HG)
    return xs, dt, A, Bm, Cm, new_buf


def ssd_chunked(xs, dt, A, Bm, Cm, h0):
    Bsz, S = xs.shape[:2]
    nc, L = S // SSM_CHUNK, SSM_CHUNK
    xs = xs.reshape(Bsz, nc, L, SSM_GROUPS, SSM_HG, SSM_HEAD_DIM)
    dt = dt.reshape(Bsz, nc, L, SSM_GROUPS, SSM_HG)
    Bm = Bm.reshape(Bsz, nc, L, SSM_GROUPS, SSM_STATE)
    Cm = Cm.reshape(Bsz, nc, L, SSM_GROUPS, SSM_STATE)
    acum = jnp.cumsum(dt * A, axis=2)
    causal = jnp.tril(jnp.ones((L, L), bool))[:, :, None, None]
    seg = acum[:, :, :, None] - acum[:, :, None, :]
    decay = jnp.exp(jnp.where(causal, seg, -jnp.inf))
    cb = jnp.einsum('bclgn,bcsgn->bclsg', Cm, Bm)
    mix = cb[..., None] * decay * dt[:, :, None]
    y_diag = jnp.einsum('bclsgh,bcsghp->bclghp', mix, xs)
    to_end = jnp.exp(acum[:, :, -1:] - acum) * dt
    chunk_states = jnp.einsum('bclgn,bclgh,bclghp->bcghpn', Bm, to_end, xs)
    chunk_decay = jnp.exp(acum[:, :, -1])

    def step(h, inp):
        s, d = inp
        return h * d[..., None, None] + s, h

    h_last, h_prev = lax.scan(step, h0, (jnp.moveaxis(chunk_states, 1, 0), jnp.moveaxis(chunk_decay, 1, 0)))
    h_prev = jnp.moveaxis(h_prev, 0, 1)
    y_off = jnp.einsum('bclgn,bcghpn->bclghp', Cm, h_prev) * jnp.exp(acum)[..., None]
    return (y_diag + y_off).reshape(Bsz, S, SSM_GROUPS, SSM_HG, SSM_HEAD_DIM), h_last


def ssd_recurrent(xs, dt, A, Bm, Cm, h0):
    def step(h, inp):
        xt, dtt, bt, ct = inp
        h = h * jnp.exp(dtt * A)[..., None, None] + jnp.einsum('bgh,bghp,bgn->bghpn', dtt, xt, bt)
        return h, jnp.einsum('bghpn,bgn->bghp', h, ct)

    seq_first = lambda a: jnp.moveaxis(a, 1, 0)
    h_last, ys = lax.scan(step, h0, (seq_first(xs), seq_first(dt), seq_first(Bm), seq_first(Cm)))
    return jnp.moveaxis(ys, 0, 1), h_last


def ssd_output(y, xs, z, d_skip, norm_w):
    Bsz, T = z.shape[:2]
    y = y + d_skip.astype(jnp.float32).reshape(SSM_GROUPS, SSM_HG)[..., None] * xs
    g = (y.reshape(Bsz, T, SSM_WIDTH) * jax.nn.silu(z.astype(jnp.float32))).reshape(Bsz, T, SSM_GROUPS, SSM_WIDTH // SSM_GROUPS)
    g = g * lax.rsqrt(jnp.mean(g * g, axis=-1, keepdims=True) + EPS)
    return (g.reshape(Bsz, T, SSM_WIDTH) * norm_w).astype(z.dtype)


def rel_bucket(dist):
    n = jnp.maximum(dist, 0)
    exact = RPE_BUCKETS // 2
    nf = jnp.maximum(n, 1).astype(jnp.float32)
    large = exact + (jnp.log(nf / exact) / math.log(RPE_MAX_DIST / exact) * (RPE_BUCKETS - exact)).astype(jnp.int32)
    return jnp.where(n < exact, n, jnp.minimum(large, RPE_BUCKETS - 1))


def select_blocks(q, kmean, n_full, n_sel):
    Bsz, Q, H = q.shape[:3]
    ok = (jnp.arange(n_sel)[None, :] < n_full[:, None])[None, :, None, :]
    if n_sel == 0:
        return jnp.zeros((Bsz, Q, H, 0), jnp.int32), ok
    s = jnp.einsum('bqhd,bhnd->bqhn', q.astype(jnp.float32), kmean)
    past = jnp.arange(kmean.shape[2])[None, :] < n_full[:, None]
    s = jnp.where(past[None, :, None, :], s, -jnp.inf)
    return lax.top_k(s, n_sel)[1], ok


def moba_attend(q, qpos, k_sel, v_sel, kpos_sel, ok_sel, k_own, v_own, kpos_own, rel_bias):
    scale = ATT_HEAD_DIM ** -0.5
    qf = q.astype(jnp.float32)
    hidx = jnp.arange(ATT_HEADS)[None, None, :, None]
    s_sel = jnp.einsum('bqhd,bqhnd->bqhn', qf, k_sel) * scale
    s_sel = s_sel + rel_bias[rel_bucket(qpos[None, :, None, None] - kpos_sel), hidx]
    s_sel = jnp.where(ok_sel, s_sel, -jnp.inf)
    d_own = qpos[:, None] - kpos_own[None, :]
    s_own = jnp.einsum('bqhd,bhwd->bqhw', qf, k_own) * scale
    s_own = s_own + rel_bias[rel_bucket(d_own)].transpose(0, 2, 1)[None]
    s_own = jnp.where((d_own >= 0)[None, :, None, :], s_own, -jnp.inf)
    p = jax.nn.softmax(jnp.concatenate([s_sel, s_own], axis=-1), axis=-1)
    ns = k_sel.shape[3]
    out = jnp.einsum('bqhn,bqhnd->bqhd', p[..., :ns], v_sel) + jnp.einsum('bqhw,bhwd->bqhd', p[..., ns:], v_own)
    return out.astype(q.dtype)


def moba_prompt(q, k, v, rel_bias):
    Bsz, S = q.shape[:2]
    W = MOBA_BLOCK
    nb = -(-S // W)
    pad = ((0, 0), (0, nb * W - S), (0, 0), (0, 0))
    kb = jnp.pad(k, pad).reshape(Bsz, nb, W, ATT_HEADS, ATT_HEAD_DIM).transpose(0, 3, 1, 2, 4)
    vb = jnp.pad(v, pad).reshape(Bsz, nb, W, ATT_HEADS, ATT_HEAD_DIM).transpose(0, 3, 1, 2, 4)
    kmean = kb.astype(jnp.float32).mean(axis=3)
    n_sel = min(MOBA_TOPK, nb - 1)
    bi = jnp.arange(Bsz)[:, None, None, None]
    hi = jnp.arange(ATT_HEADS)[None, None, :, None]

    def query_block(i):
        q0 = i * Q_BLOCK
        qi = lax.dynamic_slice_in_dim(q, q0, Q_BLOCK, axis=1)
        qpos = q0 + jnp.arange(Q_BLOCK, dtype=jnp.int32)
        own = q0 // W
        idx, ok = select_blocks(qi, kmean, qpos // W, n_sel)
        k_sel = kb[bi, hi, idx].reshape(Bsz, Q_BLOCK, ATT_HEADS, n_sel * W, ATT_HEAD_DIM)
        v_sel = vb[bi, hi, idx].reshape(Bsz, Q_BLOCK, ATT_HEADS, n_sel * W, ATT_HEAD_DIM)
        kpos_sel = (idx[..., None] * W + jnp.arange(W)).reshape(Bsz, Q_BLOCK, ATT_HEADS, n_sel * W)
        k_own = lax.dynamic_index_in_dim(kb, own, axis=2, keepdims=False)
        v_own = lax.dynamic_index_in_dim(vb, own, axis=2, keepdims=False)
        kpos_own = own * W + jnp.arange(W, dtype=jnp.int32)
        return moba_attend(qi, qpos, k_sel, v_sel, kpos_sel, jnp.repeat(ok, W, axis=-1),
                           k_own, v_own, kpos_own, rel_bias)

    out = lax.map(query_block, jnp.arange(S // Q_BLOCK, dtype=jnp.int32))
    return jnp.moveaxis(out, 0, 1).reshape(Bsz, S, ATT_HEADS, ATT_HEAD_DIM)


def moba_sample(q, k_new, v_new, pool_k, pool_v, page_table, rel_bias):
    Bsz, T = q.shape[:2]
    W, ppb = MOBA_BLOCK, MOBA_BLOCK // PAGE_SIZE
    nb_full = PAST_LEN // W
    own_start = nb_full * W
    qpos = PAST_LEN + jnp.arange(T, dtype=jnp.int32)
    kc = pool_k[page_table[:, :nb_full * ppb]]
    kmean = kc.astype(jnp.float32).reshape(Bsz, nb_full, W, ATT_HEADS, ATT_HEAD_DIM).mean(axis=2).transpose(0, 2, 1, 3)
    n_sel = min(MOBA_TOPK, nb_full)
    idx, ok = select_blocks(q, kmean, jnp.full((T,), nb_full, jnp.int32), n_sel)
    bi = jnp.arange(Bsz)[:, None, None, None, None]
    hi = jnp.arange(ATT_HEADS)[None, None, :, None, None]
    phys = page_table[bi, idx[..., None] * ppb + jnp.arange(ppb)]
    k_sel = pool_k[phys, :, hi].reshape(Bsz, T, ATT_HEADS, n_sel * W, ATT_HEAD_DIM)
    v_sel = pool_v[phys, :, hi].reshape(Bsz, T, ATT_HEADS, n_sel * W, ATT_HEAD_DIM)
    kpos_sel = (idx[..., None] * W + jnp.arange(W)).reshape(Bsz, T, ATT_HEADS, n_sel * W)
    own_pages = page_table[:, own_start // PAGE_SIZE: PAST_LEN // PAGE_SIZE]
    n_own_past = PAST_LEN - own_start
    k_own = jnp.concatenate([pool_k[own_pages].reshape(Bsz, n_own_past, ATT_HEADS, ATT_HEAD_DIM), k_new], axis=1).transpose(0, 2, 1, 3)
    v_own = jnp.concatenate([pool_v[own_pages].reshape(Bsz, n_own_past, ATT_HEADS, ATT_HEAD_DIM), v_new], axis=1).transpose(0, 2, 1, 3)
    kpos_own = own_start + jnp.arange(n_own_past + T, dtype=jnp.int32)
    return moba_attend(q, qpos, k_sel, v_sel, kpos_sel, jnp.repeat(ok, W, axis=-1),
                       k_own, v_own, kpos_own, rel_bias)


def memory_kv(mem, g, w):
    Bsz, M = mem.shape[:2]
    mk, mv = jnp.split(rmsnorm(mem, g) @ w, 2, axis=-1)
    return mk.reshape(Bsz, M, X_HEADS, X_HEAD_DIM), mv.reshape(Bsz, M, X_HEADS, X_HEAD_DIM)


def cross_attend(q, mk, mv):
    s = jnp.einsum('bthd,bmhd->bthm', q.astype(jnp.float32), mk) * X_HEAD_DIM ** -0.5
    p = jax.nn.softmax(s, axis=-1)
    return jnp.einsum('bthm,bmhd->bthd', p, mv).astype(q.dtype)


def merge_heads(y_ssd, y_att, g_att, y_x, g_x):
    lead = y_ssd.shape[:2]
    return jnp.concatenate([y_ssd,
                            y_att.reshape(lead + (ATT_WIDTH,)) * jax.nn.silu(g_att),
                            y_x.reshape(lead + (X_WIDTH,)) * jax.nn.silu(g_x)], axis=-1)


def setup_inputs(seed: int = 0) -> dict:
    key = jax.random.key(seed)
    ks = jax.random.split(key, 24)
    f32 = jnp.float32
    n_pages = PAST_LEN // PAGE_SIZE
    n_phys = (DEC_BATCH * n_pages * 5) // 4
    nrm = lambda k, shape, s=1.0: jax.random.normal(k, shape, f32) * s
    x_prompt = nrm(ks[0], (BATCH, SEQ, D_MODEL))
    x_sample = nrm(ks[1], (DEC_BATCH, DEC_SEQ, D_MODEL))
    mem_prompt = nrm(ks[2], (BATCH, N_MEM, D_MODEL))
    cache_k = nrm(ks[3], (DEPTH, n_phys, PAGE_SIZE, ATT_HEADS, ATT_HEAD_DIM))
    cache_v = nrm(ks[4], (DEPTH, n_phys, PAGE_SIZE, ATT_HEADS, ATT_HEAD_DIM))
    page_table = jax.random.permutation(ks[5], n_phys)[:DEC_BATCH * n_pages].reshape(DEC_BATCH, n_pages).astype(jnp.int32)
    cache_mem_k = nrm(ks[6], (DEPTH, DEC_BATCH, N_MEM, X_HEADS, X_HEAD_DIM))
    cache_mem_v = nrm(ks[7], (DEPTH, DEC_BATCH, N_MEM, X_HEADS, X_HEAD_DIM))
    state_ssm = nrm(ks[8], (DEPTH, DEC_BATCH, SSM_HEADS, SSM_HEAD_DIM, SSM_STATE), 0.3)
    state_conv = nrm(ks[9], (DEPTH, DEC_BATCH, CONV_W - 1, CONV_DIM))
    norm_w = 1.0 + nrm(ks[10], (DEPTH, D_MODEL), 0.05)
    w_in = nrm(ks[11], (DEPTH, D_MODEL, IN_WIDTH), D_MODEL ** -0.5)
    conv_w = nrm(ks[12], (DEPTH, CONV_W, CONV_DIM), CONV_W ** -0.5)
    conv_b = nrm(ks[13], (DEPTH, CONV_DIM), 0.01)
    dt0 = jnp.exp(jax.random.uniform(ks[14], (DEPTH, SSM_HEADS), f32, math.log(1e-3), math.log(1e-1)))
    dt_bias = dt0 + jnp.log(-jnp.expm1(-dt0))
    a_log = jnp.log(jax.random.uniform(ks[15], (DEPTH, SSM_HEADS), f32, 1.0, 16.0))
    d_skip = 1.0 + nrm(ks[16], (DEPTH, SSM_HEADS), 0.1)
    ssm_norm_w = 1.0 + nrm(ks[17], (DEPTH, SSM_WIDTH), 0.05)
    mem_norm_w = 1.0 + nrm(ks[18], (DEPTH, D_MODEL), 0.05)
    w_mem_kv = nrm(ks[19], (DEPTH, D_MODEL, 2 * X_WIDTH), D_MODEL ** -0.5)
    w_out = nrm(ks[20], (DEPTH, MIX_WIDTH, D_MODEL), MIX_WIDTH ** -0.5)
    rel_bias = nrm(ks[21], (RPE_BUCKETS, ATT_HEADS), 0.5)
    final_norm_w = 1.0 + nrm(ks[22], (D_MODEL,), 0.05)
    return {'x_prompt': x_prompt, 'x_sample': x_sample, 'mem_prompt': mem_prompt,
            'cache_k': cache_k, 'cache_v': cache_v, 'page_table': page_table,
            'cache_mem_k': cache_mem_k, 'cache_mem_v': cache_mem_v,
            'state_ssm': state_ssm, 'state_conv': state_conv,
            'norm_w': norm_w, 'w_in': w_in, 'conv_w': conv_w, 'conv_b': conv_b,
            'dt_bias': dt_bias, 'a_log': a_log, 'd_skip': d_skip, 'ssm_norm_w': ssm_norm_w,
            'mem_norm_w': mem_norm_w, 'w_mem_kv': w_mem_kv, 'w_out': w_out,
            'rel_bias': rel_bias, 'final_norm_w': final_norm_w}


def reference(x_prompt, x_sample, mem_prompt, cache_k, cache_v, page_table, cache_mem_k, cache_mem_v,
              state_ssm, state_conv, norm_w, w_in, conv_w, conv_b, dt_bias, a_log, d_skip, ssm_norm_w,
              mem_norm_w, w_mem_kv, w_out, rel_bias, final_norm_w):
    Bp, S = x_prompt.shape[:2]
    Bs, T = x_sample.shape[:2]
    kp_l, vp_l, mkp_l, mvp_l, sp_l, cp_l = [], [], [], [], [], []
    ks_l, vs_l, ss_l, cs_l = [], [], [], []
    hp, hs = x_prompt, x_sample
    for l in range(DEPTH):
        z, xbc, dt_raw, qa, ka, va, ga, qx, gx = split_in_proj(rmsnorm(hp, norm_w[l]) @ w_in[l])
        xs, dt, A, Bm, Cm, conv_p = ssd_inputs(xbc, dt_raw, jnp.zeros((Bp, CONV_W - 1, CONV_DIM), xbc.dtype),
                                               conv_w[l], conv_b[l], dt_bias[l], a_log[l])
        y, h_p = ssd_chunked(xs, dt, A, Bm, Cm,
                             jnp.zeros((Bp, SSM_GROUPS, SSM_HG, SSM_HEAD_DIM, SSM_STATE), jnp.float32))
        y_ssd = ssd_output(y, xs, z, d_skip[l], ssm_norm_w[l])
        qa, ka, va = (t.reshape(Bp, S, ATT_HEADS, ATT_HEAD_DIM) for t in (qa, ka, va))
        y_att = moba_prompt(qa, ka, va, rel_bias)
        mk, mv = memory_kv(mem_prompt, mem_norm_w[l], w_mem_kv[l])
        y_x = cross_attend(qx.reshape(Bp, S, X_HEADS, X_HEAD_DIM), mk, mv)
        hp = hp + merge_heads(y_ssd, y_att, ga, y_x, gx) @ w_out[l]
        kp_l.append(ka)
        vp_l.append(va)
        mkp_l.append(mk)
        mvp_l.append(mv)
        sp_l.append(h_p.reshape(Bp, SSM_HEADS, SSM_HEAD_DIM, SSM_STATE))
        cp_l.append(conv_p)
        z, xbc, dt_raw, qa, ka, va, ga, qx, gx = split_in_proj(rmsnorm(hs, norm_w[l]) @ w_in[l])
        xs, dt, A, Bm, Cm, conv_s = ssd_inputs(xbc, dt_raw, state_conv[l], conv_w[l], conv_b[l], dt_bias[l], a_log[l])
        h0 = state_ssm[l].astype(jnp.float32).reshape(Bs, SSM_GROUPS, SSM_HG, SSM_HEAD_DIM, SSM_STATE)
        y, h_s = ssd_recurrent(xs, dt, A, Bm, Cm, h0)
        y_ssd = ssd_output(y, xs, z, d_skip[l], ssm_norm_w[l])
        qa, ka, va = (t.reshape(Bs, T, ATT_HEADS, ATT_HEAD_DIM) for t in (qa, ka, va))
        y_att = moba_sample(qa, ka, va, cache_k[l], cache_v[l], page_table, rel_bias)
        y_x = cross_attend(qx.reshape(Bs, T, X_HEADS, X_HEAD_DIM), cache_mem_k[l], cache_mem_v[l])
        hs = hs + merge_heads(y_ssd, y_att, ga, y_x, gx) @ w_out[l]
        ks_l.append(ka)
        vs_l.append(va)
        ss_l.append(h_s.reshape(Bs, SSM_HEADS, SSM_HEAD_DIM, SSM_STATE))
        cs_l.append(conv_s)
    y_prompt = rmsnorm(hp, final_norm_w)
    y_sample = rmsnorm(hs, final_norm_w)
    k_prompt, v_prompt = jnp.stack(kp_l), jnp.stack(vp_l)
    mem_k_prompt, mem_v_prompt = jnp.stack(mkp_l), jnp.stack(mvp_l)
    ssm_prompt, conv_prompt = jnp.stack(sp_l), jnp.stack(cp_l)
    k_sample, v_sample = jnp.stack(ks_l), jnp.stack(vs_l)
    ssm_sample, conv_sample = jnp.stack(ss_l), jnp.stack(cs_l)
    return (y_prompt, y_sample, k_prompt, v_prompt, mem_k_prompt, mem_v_prompt, ssm_prompt, conv_prompt,
            k_sample, v_sample, ssm_sample, conv_sample)
```

```python
import functools
import math

import numpy as np
import jax
import jax.numpy as jnp
from jax import lax
from jax.experimental import pallas as pl
from jax.experimental.pallas import tpu as pltpu

F32 = jnp.float32
BF16 = jnp.bfloat16
EPS = 1e-6

SSM_HEADS = 16
SSM_HEAD_DIM = 64
SSM_GROUPS = 2
SSM_HG = SSM_HEADS // SSM_GROUPS
SSM_STATE = 128
SSM_WIDTH = SSM_HEADS * SSM_HEAD_DIM
SSM_GW = SSM_WIDTH // SSM_GROUPS
SSM_CHUNK = 128
CONV_W = 4
CONV_DIM = SSM_WIDTH + 2 * SSM_GROUPS * SSM_STATE
ATT_HEADS = 8
ATT_HEAD_DIM = 64
ATT_WIDTH = ATT_HEADS * ATT_HEAD_DIM
MOBA_BLOCK = 256
MOBA_TOPK = 3
PAGE_SIZE = 128
PAGES_PER_BLOCK = MOBA_BLOCK // PAGE_SIZE
RPE_BUCKETS = 32
RPE_MAX_DIST = 128
X_HEADS = 4
X_HEAD_DIM = 128
X_WIDTH = X_HEADS * X_HEAD_DIM

LANES = 128
SUBLANES = 8
VMEM_LIMIT = 56 * 1024 * 1024
NEG = -1e30


def _cparams(sem, vmem=VMEM_LIMIT):
    return pltpu.CompilerParams(dimension_semantics=sem, vmem_limit_bytes=vmem)


def _split2(x):
    hi = x.astype(BF16)
    lo = (x - hi.astype(F32)).astype(BF16)
    return hi, lo


def _split3(x):
    hi = x.astype(BF16)
    r = x - hi.astype(F32)
    mid = r.astype(BF16)
    lo = (r - mid.astype(F32)).astype(BF16)
    return hi, mid, lo


def _dot(a, b):
    return jnp.dot(a, b, preferred_element_type=F32)


def _dot_nt(a, b):
    return lax.dot_general(a, b, (((1,), (1,)), ((), ())), preferred_element_type=F32)


def _dot_tn(a, b):
    return lax.dot_general(a, b, (((0,), (0,)), ((), ())), preferred_element_type=F32)


def _dot_f32_exactrhs(a, b_bf16, passes=3):
    parts = _split3(a) if passes == 3 else _split2(a)
    out = _dot(parts[0], b_bf16)
    for p in parts[1:]:
        out = out + _dot(p, b_bf16)
    return out


def _dot_nt_hi(a, b):
    ah, al = _split2(a)
    bh, bl = _split2(b)
    return _dot_nt(ah, bh) + _dot_nt(al, bh) + _dot_nt(ah, bl)


def _silu(x):
    return x / (1.0 + jnp.exp(-x))


def _softplus(x):
    return jnp.maximum(x, 0.0) + jnp.log1p(jnp.exp(-jnp.abs(x)))


def _rms(x, g):
    ms = jnp.mean(x * x, axis=-1, keepdims=True)
    return (x * lax.rsqrt(ms + EPS)) * g


_MAIN_SPLITS = (SSM_WIDTH, CONV_DIM, ATT_WIDTH, ATT_WIDTH, ATT_WIDTH, ATT_WIDTH, X_WIDTH, X_WIDTH)
_MAIN_WIDTH = sum(_MAIN_SPLITS)


def _inproj_kernel(x_ref, nw_ref, wh_ref, wl_ref, wdth_ref, wdtl_ref,
                   z_ref, xbc_ref, q_ref, k_ref, v_ref, ga_ref, qx_ref, gx_ref,
                   qb_ref, kb_ref, vb_ref, dtT_ref, *, hi):
    xn = _rms(x_ref[...], nw_ref[...])
    xh = xn.astype(BF16)
    xl = (xn - xh.astype(F32)).astype(BF16) if hi else None

    def proj(lo, width):
        w = wh_ref[:, lo:lo + width]
        out = _dot(xh, w)
        if hi:
            out = out + _dot(xl, w) + _dot(xh, wl_ref[:, lo:lo + width])
        return out

    outs = (z_ref, xbc_ref, q_ref, k_ref, v_ref, ga_ref, qx_ref, gx_ref)
    casts = {2: (qb_ref, ATT_HEAD_DIM ** -0.5), 3: (kb_ref, 1.0), 4: (vb_ref, 1.0)}
    lo = 0
    for n, (o_ref, width) in enumerate(zip(outs, _MAIN_SPLITS)):
        val = proj(lo, width)
        o_ref[...] = val
        if n in casts:
            b_ref, scale = casts[n]
            b_ref[...] = (val * scale).astype(BF16)
        lo += width
    dt = _dot_nt(wdth_ref[...], xh)
    if hi:
        dt = dt + _dot_nt(wdth_ref[...], xl) + _dot_nt(wdtl_ref[...], xh)
    dtT_ref[...] = dt


def _in_proj(x, norm_w, w_in, *, tm, hi):
    m, d = x.shape
    cut0 = SSM_WIDTH + CONV_DIM
    w_main = jnp.concatenate([w_in[:, :cut0], w_in[:, cut0 + SSM_HEADS:]], axis=1)
    w_dt = w_in[:, cut0:cut0 + SSM_HEADS].T
    wh = w_main.astype(BF16)
    wdth = w_dt.astype(BF16)
    if hi:
        wl = (w_main - wh.astype(F32)).astype(BF16)
        wdtl = (w_dt - wdth.astype(F32)).astype(BF16)
    else:
        wl = jnp.zeros((SUBLANES, LANES), BF16)
        wdtl = jnp.zeros((SSM_HEADS, d), BF16)
    row = lambda w: pl.BlockSpec((tm, w), lambda i: (i, 0))
    full = lambda a: pl.BlockSpec(a.shape, lambda i: (0,) * a.ndim)
    out_shapes = [jax.ShapeDtypeStruct((m, w), F32) for w in _MAIN_SPLITS]
    out_shapes += [jax.ShapeDtypeStruct((m, ATT_WIDTH), BF16)] * 3
    out_shapes += [jax.ShapeDtypeStruct((SSM_HEADS, m), F32)]
    out_specs = [row(w) for w in _MAIN_SPLITS] + [row(ATT_WIDTH)] * 3
    out_specs += [pl.BlockSpec((SSM_HEADS, tm), lambda i: (0, i))]
    nw = norm_w.reshape(1, d)
    return pl.pallas_call(
        functools.partial(_inproj_kernel, hi=hi),
        grid=(m // tm,),
        in_specs=[row(d), full(nw), full(wh), full(wl), full(wdth), full(wdtl)],
        out_specs=out_specs,
        out_shape=out_shapes,
        compiler_params=_cparams(("parallel",)),
        name="in_proj",
    )(x, nw, wh, wl, wdth, wdtl)


def _ssd_kernel(xbc_ref, dtT_ref, z_ref, convw_ref, convb_ref, dtb_ref, alog_ref, dskip_ref, nw_ref,
                e_ref, y_ref, hT_ref, tail_sc, h_sc):
    c = pl.program_id(0)
    L = SSM_CHUNK

    @pl.when(c == 0)
    def _():
        tail_sc[...] = jnp.zeros_like(tail_sc)
        h_sc[...] = jnp.zeros_like(h_sc)

    x = xbc_ref[...]
    tail = tail_sc[...]
    w = convw_ref[...]
    row8 = lax.broadcasted_iota(jnp.int32, (SUBLANES, CONV_DIM), 0)
    acc = x * w[CONV_W - 1:CONV_W, :] + convb_ref[...]
    for k in range(1, CONV_W):
        rolled = pltpu.roll(x, k, 0)
        fix = pltpu.roll(tail, k, 0)
        top = jnp.where(row8 < k, fix, rolled[0:SUBLANES])
        shifted = jnp.concatenate([top, rolled[SUBLANES:]], axis=0)
        acc = acc + shifted * w[CONV_W - 1 - k:CONV_W - k, :]
    tail_sc[...] = x[L - SUBLANES:L]
    u = _silu(acc)
    xs = u[:, :SSM_WIDTH]
    bm = u[:, SSM_WIDTH:SSM_WIDTH + SSM_GROUPS * SSM_STATE]
    cm = u[:, SSM_WIDTH + SSM_GROUPS * SSM_STATE:]

    dtT = _softplus(dtT_ref[...] + dtb_ref[...])
    aT = dtT * (-jnp.exp(alog_ref[...]))
    r_i = lax.broadcasted_iota(jnp.int32, (L, L), 0)
    c_i = lax.broadcasted_iota(jnp.int32, (L, L), 1)
    upper = (r_i <= c_i).astype(BF16)
    acumT = _dot_f32_exactrhs(aT, upper)
    a_last = acumT[:, L - 1:L]
    to_endT = jnp.exp(a_last - acumT) * dtT
    eacT = jnp.exp(acumT)
    stack = jnp.concatenate(
        [acumT, dtT, to_endT, eacT, jnp.zeros((L - 4 * SSM_HEADS, L), F32)], axis=0)
    st = stack.T
    e = e_ref[...]
    dt_e = _dot_f32_exactrhs(st[:, SSM_HEADS:2 * SSM_HEADS], e, 2)
    toend_e = _dot_f32_exactrhs(st[:, 2 * SSM_HEADS:3 * SSM_HEADS], e, 2)
    eac_e = _dot_f32_exactrhs(st[:, 3 * SSM_HEADS:4 * SSM_HEADS], e, 2)
    cd_e = eac_e[L - 1:L, :]

    xd = (xs * dt_e).astype(BF16)
    xw = (xs * toend_e).astype(BF16)
    causal = r_i >= c_i
    ys = []
    yoffs = []
    for g in range(SSM_GROUPS):
        bg = bm[:, g * SSM_STATE:(g + 1) * SSM_STATE].astype(BF16)
        cg = cm[:, g * SSM_STATE:(g + 1) * SSM_STATE].astype(BF16)
        cb = jnp.where(causal, _dot_nt(cg, bg), 0.0)
        for hh in range(SSM_HG):
            h = g * SSM_HG + hh
            seg = st[:, h:h + 1] - acumT[h:h + 1, :]
            mix = (cb * jnp.exp(jnp.minimum(seg, 0.0))).astype(BF16)
            ys.append(_dot(mix, xd[:, h * SSM_HEAD_DIM:(h + 1) * SSM_HEAD_DIM]))
        hprev = h_sc[g]
        yoffs.append(_dot(cg, hprev.astype(BF16)))
        s_new = _dot_tn(bg, xw[:, g * SSM_GW:(g + 1) * SSM_GW])
        h_sc[g] = hprev * cd_e[:, g * SSM_GW:(g + 1) * SSM_GW] + s_new

    y = jnp.concatenate(ys, axis=1) + jnp.concatenate(yoffs, axis=1) * eac_e + dskip_ref[...] * xs
    gt = y * _silu(z_ref[...])
    outs = []
    for g in range(SSM_GROUPS):
        gg = gt[:, g * SSM_GW:(g + 1) * SSM_GW]
        outs.append(gg * lax.rsqrt(jnp.mean(gg * gg, axis=-1, keepdims=True) + EPS))
    y_ref[...] = jnp.concatenate(outs, axis=1) * nw_ref[...]

    @pl.when(c == pl.num_programs(0) - 1)
    def _():
        hT_ref[...] = h_sc[...]


def _head_expand():
    return jnp.asarray(np.repeat(np.eye(SSM_HEADS, dtype=np.float32), SSM_HEAD_DIM, axis=1), BF16)


def _ssd_prompt(xbc, dtT, z, conv_w, conv_b, dt_bias, a_log, d_skip, ssm_norm_w):
    m = xbc.shape[0]
    L = SSM_CHUNK
    row = lambda w: pl.BlockSpec((L, w), lambda i: (i, 0))
    full = lambda a: pl.BlockSpec(a.shape, lambda i: (0,) * a.ndim)
    convb = conv_b.reshape(1, CONV_DIM)
    dtb = dt_bias.reshape(SSM_HEADS, 1)
    alog = a_log.reshape(SSM_HEADS, 1)
    dskip = jnp.repeat(d_skip, SSM_HEAD_DIM).reshape(1, SSM_WIDTH)
    nw = ssm_norm_w.reshape(1, SSM_WIDTH)
    e = _head_expand()
    y, hT = pl.pallas_call(
        _ssd_kernel,
        grid=(m // L,),
        in_specs=[row(CONV_DIM), pl.BlockSpec((SSM_HEADS, L), lambda i: (0, i)), row(SSM_WIDTH),
                  full(conv_w), full(convb), full(dtb), full(alog), full(dskip), full(nw), full(e)],
        out_specs=[row(SSM_WIDTH), pl.BlockSpec((SSM_GROUPS, SSM_STATE, SSM_GW), lambda i: (0, 0, 0))],
        out_shape=[jax.ShapeDtypeStruct((m, SSM_WIDTH), F32),
                   jax.ShapeDtypeStruct((SSM_GROUPS, SSM_STATE, SSM_GW), F32)],
        scratch_shapes=[pltpu.VMEM((SUBLANES, CONV_DIM), F32),
                        pltpu.VMEM((SSM_GROUPS, SSM_STATE, SSM_GW), F32)],
        compiler_params=_cparams(("arbitrary",)),
        name="ssd_prompt",
    )(xbc, dtT, z, conv_w, convb, dtb, alog, dskip, nw, e)
    state = hT.reshape(SSM_GROUPS, SSM_STATE, SSM_HG, SSM_HEAD_DIM).transpose(0, 2, 3, 1)
    return y, state.reshape(SSM_HEADS, SSM_HEAD_DIM, SSM_STATE)


def _bucket_np(dist):
    n = np.maximum(dist, 0)
    exact = RPE_BUCKETS // 2
    nf = np.maximum(n, 1).astype(np.float32)
    scaled = (np.log(nf / np.float32(exact)) / np.float32(math.log(RPE_MAX_DIST / exact))
              * np.float32(RPE_BUCKETS - exact))
    large = exact + scaled.astype(np.int32)
    return np.where(n < exact, n, np.minimum(large, RPE_BUCKETS - 1)).astype(np.int32)


def _bias_kernel(rb_ref, map_ref, out_ref):
    h = pl.program_id(0)
    bmap = map_ref[...]
    acc = jnp.zeros(bmap.shape, F32)
    for b in range(RPE_BUCKETS):
        acc = jnp.where(bmap == b, rb_ref[b * ATT_HEADS + h], acc)
    out_ref[0] = acc


def _bias_table(rel_bias, bucket_map):
    bmap = jnp.asarray(bucket_map, jnp.int32)
    nd = bmap.ndim
    return pl.pallas_call(
        _bias_kernel,
        grid=(ATT_HEADS,),
        in_specs=[pl.BlockSpec(memory_space=pltpu.SMEM),
                  pl.BlockSpec(bmap.shape, lambda h: (0,) * nd)],
        out_specs=pl.BlockSpec((1,) + bmap.shape, lambda h: (h,) + (0,) * nd),
        out_shape=jax.ShapeDtypeStruct((ATT_HEADS,) + bmap.shape, F32),
        compiler_params=_cparams(("parallel",)),
        name="rel_bias_table",
    )(rel_bias.reshape(-1), bmap)


_KM_BLOCKS = 8


def _kmean_kernel(k_ref, o_ref):
    k = k_ref[...].reshape(_KM_BLOCKS, MOBA_BLOCK, ATT_WIDTH)
    o_ref[...] = jnp.sum(k, axis=1) * (1.0 / MOBA_BLOCK)


def _kmean_prompt(k):
    m = k.shape[0]
    nb = m // MOBA_BLOCK
    return pl.pallas_call(
        _kmean_kernel,
        grid=(nb // _KM_BLOCKS,),
        in_specs=[pl.BlockSpec((_KM_BLOCKS * MOBA_BLOCK, ATT_WIDTH), lambda i: (i, 0))],
        out_specs=pl.BlockSpec((_KM_BLOCKS, ATT_WIDTH), lambda i: (i, 0)),
        out_shape=jax.ShapeDtypeStruct((nb, ATT_WIDTH), F32),
        compiler_params=_cparams(("parallel",)),
        name="moba_kmean",
    )(k)


_NO_BLOCK = -2.0


def _topk_blocks(s, blk):
    picks = []
    for _ in range(MOBA_TOPK):
        mx = jnp.max(s, axis=1, keepdims=True)
        idx = jnp.min(jnp.where(s == mx, blk, float(2 * LANES)), axis=1, keepdims=True)
        picks.append(jnp.where(mx > -jnp.inf, idx, _NO_BLOCK))
        s = jnp.where(blk == idx, -jnp.inf, s)
    return picks


def _moba_kernel(cfar_ref, qf_ref, qb_ref, kb_ref, vb_ref, kmt_ref, d0_ref, d1_ref, o_ref):
    hp = pl.program_id(0)
    i = pl.program_id(1)
    W = MOBA_BLOCK
    D = ATT_HEAD_DIM
    r_i = lax.broadcasted_iota(jnp.int32, (W, W), 0)
    c_i = lax.broadcasted_iota(jnp.int32, (W, W), 1)
    blk = lax.broadcasted_iota(jnp.int32, (W, LANES), 1).astype(F32)
    i_f = i.astype(F32)
    outs = []
    for hh in range(2):
        lo = hh * D
        cfar = cfar_ref[hp * 2 + hh]
        s_sel = _dot_nt_hi(qf_ref[:, lo:lo + D], kmt_ref[hh])
        s_sel = jnp.where(blk < i_f, s_sel, -jnp.inf)
        picks = _topk_blocks(s_sel, blk)
        q = qb_ref[:, lo:lo + D]

        def penalty(j_f):
            hit = jnp.maximum(jnp.maximum((picks[0] == j_f).astype(F32), (picks[1] == j_f).astype(F32)),
                              (picks[2] == j_f).astype(F32))
            return (1.0 - hit) * NEG

        own = pl.multiple_of(i * W, W)
        s = _dot_nt(q, kb_ref[pl.ds(own, W), lo:lo + D]) + d0_ref[hh]
        s = jnp.where(c_i <= r_i, s, NEG)
        m = jnp.max(s, axis=1, keepdims=True)
        p = jnp.exp(s - m)
        l = jnp.sum(p, axis=1, keepdims=True)
        acc = _dot(p.astype(BF16), vb_ref[pl.ds(own, W), lo:lo + D])

        def update(carry, start, bias):
            m, l, acc = carry
            s = _dot_nt(q, kb_ref[pl.ds(start, W), lo:lo + D]) + bias
            m_new = jnp.maximum(m, jnp.max(s, axis=1, keepdims=True))
            alpha = jnp.exp(m - m_new)
            p = jnp.exp(s - m_new)
            l = alpha * l + jnp.sum(p, axis=1, keepdims=True)
            acc = alpha * acc + _dot(p.astype(BF16), vb_ref[pl.ds(start, W), lo:lo + D])
            return m_new, l, acc

        near = jnp.maximum(i - 1, 0)
        carry = update((m, l, acc), pl.multiple_of(near * W, W), d1_ref[hh] + penalty(i_f - 1.0))

        def far_body(j, carry):
            return update(carry, pl.multiple_of(j * W, W), cfar + penalty(j.astype(F32)))

        m, l, acc = lax.fori_loop(0, near, far_body, carry)
        outs.append(acc / l)
    o_ref[...] = jnp.concatenate(outs, axis=1)


def _moba_prompt(qf, qb, kb, vb, kmean, d0, d1, rel_bias):
    m = qf.shape[0]
    W = MOBA_BLOCK
    nb = m // W
    assert nb - 1 >= MOBA_TOPK and nb <= LANES
    kmt = jnp.pad(kmean.reshape(nb, ATT_HEADS, ATT_HEAD_DIM).transpose(1, 0, 2),
                  ((0, 0), (0, LANES - nb), (0, 0)))
    cfar = rel_bias[RPE_BUCKETS - 1]
    return pl.pallas_call(
        _moba_kernel,
        grid=(ATT_HEADS // 2, nb),
        in_specs=[pl.BlockSpec(memory_space=pltpu.SMEM),
                  pl.BlockSpec((W, LANES), lambda hp, i: (i, hp)),
                  pl.BlockSpec((W, LANES), lambda hp, i: (i, hp)),
                  pl.BlockSpec((m, LANES), lambda hp, i: (0, hp)),
                  pl.BlockSpec((m, LANES), lambda hp, i: (0, hp)),
                  pl.BlockSpec((2, LANES, ATT_HEAD_DIM), lambda hp, i: (hp, 0, 0)),
                  pl.BlockSpec((2, W, W), lambda hp, i: (hp, 0, 0)),
                  pl.BlockSpec((2, W, W), lambda hp, i: (hp, 0, 0))],
        out_specs=pl.BlockSpec((W, LANES), lambda hp, i: (i, hp)),
        out_shape=jax.ShapeDtypeStruct((m, ATT_WIDTH), F32),
        compiler_params=_cparams(("parallel", "arbitrary")),
        name="moba_prompt",
    )(cfar, qf, qb, kb, vb, kmt, d0, d1)


def _memkv_kernel(mem_ref, g_ref, w_ref, k_ref, v_ref):
    xn = _rms(mem_ref[...], g_ref[...]).astype(BF16)
    kv = _dot(xn, w_ref[...])
    k_ref[...] = kv[:, :X_WIDTH]
    v_ref[...] = kv[:, X_WIDTH:]


def _memory_kv(mem, g, w):
    nm, d = mem.shape
    full = lambda a: pl.BlockSpec(a.shape, lambda: (0,) * a.ndim)
    g2 = g.reshape(1, d)
    wb = w.astype(BF16)
    return pl.pallas_call(
        _memkv_kernel,
        in_specs=[full(mem), full(g2), full(wb)],
        out_specs=[pl.BlockSpec((nm, X_WIDTH), lambda: (0, 0))] * 2,
        out_shape=[jax.ShapeDtypeStruct((nm, X_WIDTH), F32)] * 2,
        compiler_params=pltpu.CompilerParams(vmem_limit_bytes=VMEM_LIMIT),
        name="memory_kv",
    )(mem, g2, wb)


def _xattn_kernel(q_ref, gx_ref, mk_ref, mv_ref, o_ref):
    q = q_ref[...].reshape(-1, X_WIDTH)
    gx = gx_ref[...].reshape(-1, X_WIDTH)
    mk = mk_ref[...].reshape(-1, X_WIDTH).astype(BF16)
    mv = mv_ref[...].reshape(-1, X_WIDTH).astype(BF16)
    outs = []
    for h in range(X_HEADS):
        sl = slice(h * X_HEAD_DIM, (h + 1) * X_HEAD_DIM)
        s = _dot_nt((q[:, sl] * (X_HEAD_DIM ** -0.5)).astype(BF16), mk[:, sl])
        m = jnp.max(s, axis=1, keepdims=True)
        p = jnp.exp(s - m)
        l = jnp.sum(p, axis=1, keepdims=True)
        outs.append(_dot(p.astype(BF16), mv[:, sl]) / l)
    o_ref[...] = (jnp.concatenate(outs, axis=1) * _silu(gx)).reshape(o_ref.shape)


def _xattn_prompt(qx, gx, mk, mv, *, tq):
    m = qx.shape[0]
    row = pl.BlockSpec((tq, X_WIDTH), lambda i: (i, 0))
    full = pl.BlockSpec(mk.shape, lambda i: (0, 0))
    return pl.pallas_call(
        _xattn_kernel,
        grid=(m // tq,),
        in_specs=[row, row, full, full],
        out_specs=row,
        out_shape=jax.ShapeDtypeStruct((m, X_WIDTH), F32),
        compiler_params=_cparams(("parallel",)),
        name="xattn_prompt",
    )(qx, gx, mk, mv)


def _xattn_sample(qx, gx, mk, mv):
    b, t8, _ = qx.shape
    nm = mk.shape[1]
    qs = pl.BlockSpec((1, t8, X_WIDTH), lambda i: (i, 0, 0))
    ms = pl.BlockSpec((1, nm, X_WIDTH), lambda i: (i, 0, 0))
    return pl.pallas_call(
        _xattn_kernel,
        grid=(b,),
        in_specs=[qs, qs, ms, ms],
        out_specs=qs,
        out_shape=jax.ShapeDtypeStruct((b, t8, X_WIDTH), F32),
        compiler_params=_cparams(("parallel",)),
        name="xattn_sample",
    )(qx, gx, mk, mv)


def _outproj_kernel(x_ref, ys_ref, ya_ref, ga_ref, yx_ref, w_ref, fw_ref, o_ref):
    att = (ya_ref[...] * _silu(ga_ref[...])).astype(BF16)
    h = x_ref[...] + _dot(ys_ref[...].astype(BF16), w_ref[0:SSM_WIDTH, :])
    h = h + _dot(att, w_ref[SSM_WIDTH:SSM_WIDTH + ATT_WIDTH, :])
    h = h + _dot(yx_ref[...].astype(BF16), w_ref[SSM_WIDTH + ATT_WIDTH:, :])
    o_ref[...] = _rms(h, fw_ref[...])


def _out_proj(x, y_ssd, y_att, ga, y_x, w_out, final_norm_w, *, tm):
    m, d = x.shape
    row = lambda w: pl.BlockSpec((tm, w), lambda i: (i, 0))
    full = lambda a: pl.BlockSpec(a.shape, lambda i: (0,) * a.ndim)
    wb = w_out.astype(BF16)
    fw = final_norm_w.reshape(1, d)
    return pl.pallas_call(
        _outproj_kernel,
        grid=(m // tm,),
        in_specs=[row(d), row(SSM_WIDTH), row(ATT_WIDTH), row(ATT_WIDTH), row(X_WIDTH), full(wb), full(fw)],
        out_specs=row(d),
        out_shape=jax.ShapeDtypeStruct((m, d), F32),
        compiler_params=_cparams(("parallel",)),
        name="out_proj",
    )(x, y_ssd, y_att, ga, y_x, wb, fw)


def _ssd_sample_kernel(xT_ref, cT_ref, xr_ref, cr_ref, dt_ref, zT_ref, h0_ref,
                       wT_ref, bT_ref, wr_ref, br_ref, dtb_ref, alog_ref, dskT_ref, nwT_ref,
                       yT_ref, h_ref, *, T):
    cT = cT_ref[0]
    xT = xT_ref[0]
    ext_col = lambda i: cT[:, i:i + 1] if i < CONV_W - 1 else xT[:, i - (CONV_W - 1):i - (CONV_W - 2)]
    wT = wT_ref[...]
    cr = cr_ref[0]
    xr = xr_ref[0]
    ext_row = lambda i: cr[i:i + 1, :] if i < CONV_W - 1 else xr[i - (CONV_W - 1):i - (CONV_W - 2), :]
    wr = wr_ref[...]
    dt = _softplus(dt_ref[0] + dtb_ref[...])
    a = dt * (-jnp.exp(alog_ref[...]))
    dec = jnp.exp(a)
    for g in range(SSM_GROUPS):
        for hh in range(SSM_HG):
            h_ref[0, g * SSM_HG + hh] = h0_ref[0, g * SSM_HG + hh]
    for t in range(T):
        uT = bT_ref[...]
        uR = br_ref[...]
        for j in range(CONV_W):
            uT = uT + wT[:, j:j + 1] * ext_col(t + j)
            uR = uR + wr[j:j + 1, :] * ext_row(t + j)
        uT = _silu(uT)
        uR = _silu(uR)
        ycols = []
        for h in range(SSM_HEADS):
            g = h // SSM_HG
            xcol = uT[h * SSM_HEAD_DIM:(h + 1) * SSM_HEAD_DIM, :]
            brow = uR[:, SSM_WIDTH + g * SSM_STATE:SSM_WIDTH + (g + 1) * SSM_STATE]
            crow = uR[:, SSM_WIDTH + (SSM_GROUPS + g) * SSM_STATE:
                      SSM_WIDTH + (SSM_GROUPS + g + 1) * SSM_STATE]
            hs = h_ref[0, h] * dec[h:h + 1, t:t + 1] + (xcol * dt[h:h + 1, t:t + 1]) * brow
            h_ref[0, h] = hs
            ycols.append(jnp.sum(hs * crow, axis=1, keepdims=True))
        y = jnp.concatenate(ycols, axis=0) + dskT_ref[...] * uT[:SSM_WIDTH, :]
        zt = zT_ref[0][:, t:t + 1]
        gt = y * _silu(zt)
        outs = []
        for g in range(SSM_GROUPS):
            gg = gt[g * SSM_GW:(g + 1) * SSM_GW, :]
            outs.append(gg * lax.rsqrt(jnp.mean(gg * gg, axis=0, keepdims=True) + EPS))
        yT_ref[0, :, t:t + 1] = jnp.concatenate(outs, axis=0) * nwT_ref[...]


def _ssd_sample(xbc, state_conv, dt_raw, z, state_ssm, conv_w, conv_b, dt_bias, a_log, d_skip, ssm_norm_w):
    b, T, _ = xbc.shape
    xT = xbc.transpose(0, 2, 1)
    cT = state_conv.transpose(0, 2, 1)
    dtT = dt_raw.transpose(0, 2, 1)
    zT = z.transpose(0, 2, 1)
    per = lambda a: pl.BlockSpec((1,) + a.shape[1:], lambda i: (i,) + (0,) * (a.ndim - 1))
    full = lambda a: pl.BlockSpec(a.shape, lambda i: (0,) * a.ndim)
    wT = conv_w.T
    bT = conv_b.reshape(CONV_DIM, 1)
    br = conv_b.reshape(1, CONV_DIM)
    dtb = dt_bias.reshape(SSM_HEADS, 1)
    alog = a_log.reshape(SSM_HEADS, 1)
    dskT = jnp.repeat(d_skip, SSM_HEAD_DIM).reshape(SSM_WIDTH, 1)
    nwT = ssm_norm_w.reshape(SSM_WIDTH, 1)
    args = (xT, cT, xbc, state_conv, dtT, zT, state_ssm, wT, bT, conv_w, br, dtb, alog, dskT, nwT)
    in_specs = [per(a) for a in args[:7]] + [full(a) for a in args[7:]]
    yT, h = pl.pallas_call(
        functools.partial(_ssd_sample_kernel, T=T),
        grid=(b,),
        in_specs=in_specs,
        out_specs=[pl.BlockSpec((1, SSM_WIDTH, T), lambda i: (i, 0, 0)), per(state_ssm)],
        out_shape=[jax.ShapeDtypeStruct((b, SSM_WIDTH, T), F32),
                   jax.ShapeDtypeStruct(state_ssm.shape, F32)],
        compiler_params=_cparams(("parallel",)),
        name="ssd_sample",
    )(*args)
    return yT.transpose(0, 2, 1), h


_KM_PAGES = 16


def _kmean_pages_kernel(pt_ref, *refs):
    o_ref = refs[-1]
    rows = []
    for blk in range(_KM_PAGES // PAGES_PER_BLOCK):
        s = jnp.sum(refs[blk * PAGES_PER_BLOCK][0], axis=0, keepdims=True)
        for pg in range(1, PAGES_PER_BLOCK):
            s = s + jnp.sum(refs[blk * PAGES_PER_BLOCK + pg][0], axis=0, keepdims=True)
        rows.append(s)
    o_ref[0] = jnp.concatenate(rows, axis=0) * (1.0 / MOBA_BLOCK)


def _kmean_sample(pool_k, page_table):
    b, n_pages = page_table.shape
    nb = n_pages // PAGES_PER_BLOCK
    steps = n_pages // _KM_PAGES
    page_spec = lambda r: pl.BlockSpec(
        (1, PAGE_SIZE, ATT_WIDTH), lambda bi, si, pt: (pt[bi * n_pages + si * _KM_PAGES + r], 0, 0))
    grid_spec = pltpu.PrefetchScalarGridSpec(
        num_scalar_prefetch=1,
        grid=(b, steps),
        in_specs=[page_spec(r) for r in range(_KM_PAGES)],
        out_specs=pl.BlockSpec((1, _KM_PAGES // PAGES_PER_BLOCK, ATT_WIDTH), lambda bi, si, pt: (bi, si, 0)),
    )
    return pl.pallas_call(
        _kmean_pages_kernel,
        grid_spec=grid_spec,
        out_shape=jax.ShapeDtypeStruct((b, nb, ATT_WIDTH), F32),
        compiler_params=_cparams(("parallel", "parallel")),
        name="moba_kmean_pages",
    )(page_table.reshape(-1), *([pool_k] * _KM_PAGES))


def _select_sample_kernel(q_ref, km_ref, o_ref):
    q = q_ref[0]
    km = km_ref[0]
    nb = km.shape[0]
    t8 = q.shape[0]
    blk = lax.broadcasted_iota(jnp.int32, (t8, nb), 1).astype(F32)
    lane = lax.broadcasted_iota(jnp.int32, (t8, LANES), 1)
    for h in range(ATT_HEADS):
        sl = slice(h * ATT_HEAD_DIM, (h + 1) * ATT_HEAD_DIM)
        picks = _topk_blocks(_dot_nt_hi(q[:, sl], km[:, sl]), blk)
        tile = jnp.zeros((t8, LANES), F32)
        for r in range(MOBA_TOPK):
            tile = jnp.where(lane == r, picks[r], tile)
        o_ref[0, h] = tile.astype(jnp.int32)


def _select_sample(q8, kmean):
    b, t8, _ = q8.shape
    nb = kmean.shape[1]
    assert nb >= MOBA_TOPK
    return pl.pallas_call(
        _select_sample_kernel,
        grid=(b,),
        in_specs=[pl.BlockSpec((1, t8, ATT_WIDTH), lambda i: (i, 0, 0)),
                  pl.BlockSpec((1, nb, ATT_WIDTH), lambda i: (i, 0, 0))],
        out_specs=pl.BlockSpec((1, ATT_HEADS, t8, LANES), lambda i: (i, 0, 0, 0)),
        out_shape=jax.ShapeDtypeStruct((b, ATT_HEADS, t8, LANES), jnp.int32),
        compiler_params=_cparams(("parallel",)),
        name="moba_select_sample",
    )(q8, kmean)


_N_SLAB = MOBA_TOPK * PAGES_PER_BLOCK


def _attend_sample_kernel(sel_ref, pt_ref, rb_ref, q_ref, kn_ref, vn_ref, near_ref, pk_ref, pv_ref,
                          o_ref, kbuf, vbuf, sem, *, T, n_pages):
    bi = pl.program_id(0)
    t = pl.program_id(1)
    step = bi * T + t
    nsteps = pl.num_programs(0) * T
    nb = n_pages // PAGES_PER_BLOCK
    D = ATT_HEAD_DIM

    def copies(s, slot):
        b_s = s // T
        out = []
        for h in range(ATT_HEADS):
            for r in range(MOBA_TOPK):
                blk = sel_ref[(s * ATT_HEADS + h) * MOBA_TOPK + r]
                for pg in range(PAGES_PER_BLOCK):
                    phys = pt_ref[b_s * n_pages + blk * PAGES_PER_BLOCK + pg]
                    i = r * PAGES_PER_BLOCK + pg
                    out.append(pltpu.make_async_copy(pk_ref.at[phys, :, h, :], kbuf.at[slot, h, i], sem.at[0, slot]))
                    out.append(pltpu.make_async_copy(pv_ref.at[phys, :, h, :], vbuf.at[slot, h, i], sem.at[1, slot]))
        return out

    @pl.when(step == 0)
    def _():
        for cp in copies(step, 0):
            cp.start()

    @pl.when(step + 1 < nsteps)
    def _():
        for cp in copies(step + 1, (step + 1) % 2):
            cp.start()

    slot = step % 2
    for cp in copies(step, slot):
        cp.wait()

    q = q_ref[0]
    kn = kn_ref[0]
    vn = vn_ref[0]
    trow = lax.broadcasted_iota(jnp.int32, (T, 1), 0)
    outs = []
    for h in range(ATT_HEADS):
        sl = slice(h * D, (h + 1) * D)
        qh = q[:, sl]
        cfar = rb_ref[(RPE_BUCKETS - 1) * ATT_HEADS + h]
        kk = kbuf[slot, h].reshape(_N_SLAB * PAGE_SIZE, D)
        vv = vbuf[slot, h].reshape(_N_SLAB * PAGE_SIZE, D)
        s_sel = jnp.sum(kk * qh, axis=1, keepdims=True)
        biases = []
        for r in range(MOBA_TOPK):
            blk = sel_ref[(step * ATT_HEADS + h) * MOBA_TOPK + r]
            biases.append(jnp.where(blk == nb - 1, near_ref[h, 0], cfar))
        s_sel = s_sel + jnp.concatenate(biases, axis=0)
        s_own = jnp.sum(kn[:, sl] * qh, axis=1, keepdims=True)
        b_own = jnp.zeros((T, 1), F32)
        for tp in range(T):
            d = jnp.maximum(t - tp, 0)
            b_own = jnp.where(trow == tp, rb_ref[d * ATT_HEADS + h], b_own)
        s_own = jnp.where(trow <= t, s_own + b_own, NEG)
        m = jnp.maximum(jnp.max(s_sel, axis=0, keepdims=True), jnp.max(s_own, axis=0, keepdims=True))
        p_sel = jnp.exp(s_sel - m)
        p_own = jnp.exp(s_own - m)
        l = jnp.sum(p_sel, axis=0, keepdims=True) + jnp.sum(p_own, axis=0, keepdims=True)
        o = jnp.sum(vv * p_sel, axis=0, keepdims=True) + jnp.sum(vn[:, sl] * p_own, axis=0, keepdims=True)
        outs.append(o / l)
    o_ref[0] = jnp.concatenate(outs, axis=1)


def _attend_sample(sel, page_table, rel_bias, q_scaled, k_new, v_new, near, pool_k, pool_v):
    b, T, _ = k_new.shape
    n_pages = page_table.shape[1]
    assert T <= RPE_BUCKETS // 2
    grid_spec = pltpu.PrefetchScalarGridSpec(
        num_scalar_prefetch=3,
        grid=(b, T),
        in_specs=[pl.BlockSpec((1, 1, ATT_WIDTH), lambda bi, t, *_: (bi * T + t, 0, 0)),
                  pl.BlockSpec((1, T, ATT_WIDTH), lambda bi, t, *_: (bi, 0, 0)),
                  pl.BlockSpec((1, T, ATT_WIDTH), lambda bi, t, *_: (bi, 0, 0)),
                  pl.BlockSpec((ATT_HEADS, 1, MOBA_BLOCK, 1), lambda bi, t, *_: (0, t, 0, 0)),
                  pl.BlockSpec(memory_space=pl.ANY),
                  pl.BlockSpec(memory_space=pl.ANY)],
        out_specs=pl.BlockSpec((1, 1, ATT_WIDTH), lambda bi, t, *_: (bi * T + t, 0, 0)),
        scratch_shapes=[pltpu.VMEM((2, ATT_HEADS, _N_SLAB, PAGE_SIZE, ATT_HEAD_DIM), F32),
                        pltpu.VMEM((2, ATT_HEADS, _N_SLAB, PAGE_SIZE, ATT_HEAD_DIM), F32),
                        pltpu.SemaphoreType.DMA((2, 2))],
    )
    return pl.pallas_call(
        functools.partial(_attend_sample_kernel, T=T, n_pages=n_pages),
        grid_spec=grid_spec,
        out_shape=jax.ShapeDtypeStruct((b * T, 1, ATT_WIDTH), F32),
        compiler_params=_cparams(("arbitrary", "arbitrary")),
        name="moba_attend_sample",
    )(sel, page_table.reshape(-1), rel_bias.reshape(-1), q_scaled, k_new, v_new, near, pool_k, pool_v)


def _prompt_layer(x, mem, norm_w, w_in, conv_w, conv_b, dt_bias, a_log, d_skip, ssm_norm_w,
                  mem_norm_w, w_mem_kv, w_out, rel_bias, final_norm_w):
    s = x.shape[0]
    W = MOBA_BLOCK
    z, xbc, q, k, v, ga, qx, gx, qb, kb, vb, dtT = _in_proj(x, norm_w, w_in, tm=256, hi=False)
    y_ssd, state = _ssd_prompt(xbc, dtT, z, conv_w, conv_b, dt_bias, a_log, d_skip, ssm_norm_w)
    kmean = _kmean_prompt(k)
    qi = np.arange(W)[:, None]
    ki = np.arange(W)[None, :]
    d01 = _bias_table(rel_bias, np.stack([_bucket_np(qi - ki), _bucket_np(W + qi - ki)]))
    y_att = _moba_prompt(q, qb, kb, vb, kmean, d01[:, 0], d01[:, 1], rel_bias)
    mk, mv = _memory_kv(mem, mem_norm_w, w_mem_kv)
    y_x = _xattn_prompt(qx, gx, mk, mv, tq=512)
    y = _out_proj(x, y_ssd, y_att, ga, y_x, w_out, final_norm_w, tm=512)
    conv_state = xbc[s - (CONV_W - 1):]
    return y, k, v, mk, mv, state, conv_state


def _sample_layer(x, pool_k, pool_v, page_table, mem_k, mem_v, state_ssm, state_conv, norm_w, w_in,
                  conv_w, conv_b, dt_bias, a_log, d_skip, ssm_norm_w, w_out, rel_bias, final_norm_w):
    b, T, d = x.shape
    n_pages = page_table.shape[1]
    assert (n_pages * PAGE_SIZE) % MOBA_BLOCK == 0 and T >= CONV_W - 1 and T <= SUBLANES
    xf = x.reshape(b * T, d)
    z, xbc, q, k, v, ga, qx, gx, _, _, _, dtT = _in_proj(xf, norm_w, w_in, tm=b * T, hi=True)
    per = lambda a: a.reshape(b, T, a.shape[-1])
    y_ssd, h_new = _ssd_sample(per(xbc), state_conv, per(dtT.T), per(z), state_ssm,
                               conv_w, conv_b, dt_bias, a_log, d_skip, ssm_norm_w)
    pad8 = lambda a: jnp.pad(per(a), ((0, 0), (0, SUBLANES - T), (0, 0)))
    kmean = _kmean_sample(pool_k.reshape(pool_k.shape[0], PAGE_SIZE, ATT_WIDTH), page_table)
    sel = _select_sample(pad8(q), kmean)[:, :, :T, :MOBA_TOPK]
    sel = sel.transpose(0, 2, 1, 3).reshape(-1)
    kk = np.arange(MOBA_BLOCK)[None, :, None]
    tt = np.arange(T)[:, None, None]
    near = _bias_table(rel_bias, _bucket_np(MOBA_BLOCK + tt - kk))
    q_scaled = (q * (ATT_HEAD_DIM ** -0.5)).reshape(b * T, 1, ATT_WIDTH)
    y_att = _attend_sample(sel, page_table, rel_bias, q_scaled, per(k), per(v), near, pool_k, pool_v)
    y_att = y_att.reshape(b * T, ATT_WIDTH)
    y_x = _xattn_sample(pad8(qx), pad8(gx), mem_k.reshape(b, -1, X_WIDTH), mem_v.reshape(b, -1, X_WIDTH))
    y_x = y_x[:, :T].reshape(b * T, X_WIDTH)
    y = _out_proj(xf, y_ssd.reshape(b * T, SSM_WIDTH), y_att, ga, y_x, w_out, final_norm_w, tm=b * T)
    conv_state = per(xbc)[:, T - (CONV_W - 1):]
    return y.reshape(b, T, d), per(k), per(v), h_new, conv_state


def kernel(x_prompt, x_sample, mem_prompt, cache_k, cache_v, page_table, cache_mem_k, cache_mem_v,
           state_ssm, state_conv, norm_w, w_in, conv_w, conv_b, dt_bias, a_log, d_skip, ssm_norm_w,
           mem_norm_w, w_mem_kv, w_out, rel_bias, final_norm_w):
    bp, s, d = x_prompt.shape
    bs, T, _ = x_sample.shape
    depth = w_in.shape[0]
    assert bp == 1 and depth == 1
    l = 0
    y_p, k_p, v_p, mk, mv, ssm_p, conv_p = _prompt_layer(
        x_prompt[0], mem_prompt[0], norm_w[l], w_in[l], conv_w[l], conv_b[l], dt_bias[l], a_log[l],
        d_skip[l], ssm_norm_w[l], mem_norm_w[l], w_mem_kv[l], w_out[l], rel_bias, final_norm_w)
    y_s, k_s, v_s, ssm_s, conv_s = _sample_layer(
        x_sample, cache_k[l], cache_v[l], page_table, cache_mem_k[l], cache_mem_v[l], state_ssm[l],
        state_conv[l], norm_w[l], w_in[l], conv_w[l], conv_b[l], dt_bias[l], a_log[l], d_skip[l],
        ssm_norm_w[l], w_out[l], rel_bias, final_norm_w)
    nm = mem_prompt.shape[1]
    return (y_p[None],
            y_s,
            k_p.reshape(1, 1, s, ATT_HEADS, ATT_HEAD_DIM),
            v_p.reshape(1, 1, s, ATT_HEADS, ATT_HEAD_DIM),
            mk.reshape(1, 1, nm, X_HEADS, X_HEAD_DIM),
            mv.reshape(1, 1, nm, X_HEADS, X_HEAD_DIM),
            ssm_p[None, None],
            conv_p[None, None],
            k_s.reshape(1, bs, T, ATT_HEADS, ATT_HEAD_DIM),
            v_s.reshape(1, bs, T, ATT_HEADS, ATT_HEAD_DIM),
            ssm_s[None],
            conv_s[None])
```

```python
import functools
import math

import numpy as np
import jax
import jax.numpy as jnp
from jax import lax
from jax.experimental import pallas as pl
from jax.experimental.pallas import tpu as pltpu

F32 = jnp.float32
BF16 = jnp.bfloat16
EPS = 1e-6

SSM_HEADS = 16
SSM_HEAD_DIM = 64
SSM_GROUPS = 2
SSM_HG = SSM_HEADS // SSM_GROUPS
SSM_STATE = 128
SSM_WIDTH = SSM_HEADS * SSM_HEAD_DIM
SSM_GW = SSM_WIDTH // SSM_GROUPS
SSM_CHUNK = 128
CONV_W = 4
CONV_DIM = SSM_WIDTH + 2 * SSM_GROUPS * SSM_STATE
ATT_HEADS = 8
ATT_HEAD_DIM = 64
ATT_WIDTH = ATT_HEADS * ATT_HEAD_DIM
MOBA_BLOCK = 256
MOBA_TOPK = 3
PAGE_SIZE = 128
PAGES_PER_BLOCK = MOBA_BLOCK // PAGE_SIZE
RPE_BUCKETS = 32
RPE_MAX_DIST = 128
X_HEADS = 4
X_HEAD_DIM = 128
X_WIDTH = X_HEADS * X_HEAD_DIM

LANES = 128
SUBLANES = 8
VMEM_LIMIT = 56 * 1024 * 1024
NEG = -1e30
BIG = 2.0 ** 100
LOG2E = math.log2(math.e)


def _cparams(sem, vmem=VMEM_LIMIT):
    return pltpu.CompilerParams(dimension_semantics=sem, vmem_limit_bytes=vmem)


def _split2(x):
    hi = x.astype(BF16)
    lo = (x - hi.astype(F32)).astype(BF16)
    return hi, lo


def _split3(x):
    hi = x.astype(BF16)
    r = x - hi.astype(F32)
    mid = r.astype(BF16)
    lo = (r - mid.astype(F32)).astype(BF16)
    return hi, mid, lo


def _dot(a, b):
    return jnp.dot(a, b, preferred_element_type=F32)


def _dot_nt(a, b):
    return lax.dot_general(a, b, (((1,), (1,)), ((), ())), preferred_element_type=F32)


def _dot_tn(a, b):
    return lax.dot_general(a, b, (((0,), (0,)), ((), ())), preferred_element_type=F32)


def _dot_f32_exactrhs(a, b_bf16, passes=3):
    parts = _split3(a) if passes == 3 else _split2(a)
    out = _dot(parts[0], b_bf16)
    for p in parts[1:]:
        out = out + _dot(p, b_bf16)
    return out


def _dot_hi(a, b):
    ah, al = _split2(a)
    bh, bl = _split2(b)
    return _dot(ah, bh) + _dot(al, bh) + _dot(ah, bl)


def _silu(x):
    return x / (1.0 + jnp.exp(-x))


def _softplus(x):
    return jnp.maximum(x, 0.0) + jnp.log1p(jnp.exp(-jnp.abs(x)))


def _rms(x, g):
    ms = jnp.mean(x * x, axis=-1, keepdims=True)
    return (x * lax.rsqrt(ms + EPS)) * g


_MAIN_SPLITS = (SSM_WIDTH, CONV_DIM, ATT_WIDTH, ATT_WIDTH, ATT_WIDTH, ATT_WIDTH, X_WIDTH, X_WIDTH)


def _inproj_kernel(x_ref, nw_ref, wh_ref, wl_ref, wdth_ref, wdtl_ref,
                   z_ref, xbc_ref, q_ref, k_ref, v_ref, ga_ref, qx_ref, gx_ref, dtT_ref):
    xn = _rms(x_ref[...], nw_ref[...])
    xh, xl = _split2(xn)
    outs = (z_ref, xbc_ref, q_ref, k_ref, v_ref, ga_ref, qx_ref, gx_ref)
    lo = 0
    for o_ref, width in zip(outs, _MAIN_SPLITS):
        w = wh_ref[:, lo:lo + width]
        o_ref[...] = _dot(xh, w) + _dot(xl, w) + _dot(xh, wl_ref[:, lo:lo + width])
        lo += width
    dtT_ref[...] = (_dot_nt(wdth_ref[...], xh) + _dot_nt(wdth_ref[...], xl) + _dot_nt(wdtl_ref[...], xh))


def _in_proj_sample(x, norm_w, w_in):
    m, d = x.shape
    cut0 = SSM_WIDTH + CONV_DIM
    w_main = jnp.concatenate([w_in[:, :cut0], w_in[:, cut0 + SSM_HEADS:]], axis=1)
    w_dt = w_in[:, cut0:cut0 + SSM_HEADS].T
    wh = w_main.astype(BF16)
    wdth = w_dt.astype(BF16)
    wl = (w_main - wh.astype(F32)).astype(BF16)
    wdtl = (w_dt - wdth.astype(F32)).astype(BF16)
    full = lambda a: pl.BlockSpec(a.shape, lambda: (0,) * a.ndim)
    out_shapes = [jax.ShapeDtypeStruct((m, w), F32) for w in _MAIN_SPLITS]
    out_shapes += [jax.ShapeDtypeStruct((SSM_HEADS, m), F32)]
    nw = norm_w.reshape(1, d)
    return pl.pallas_call(
        _inproj_kernel,
        in_specs=[full(x), full(nw), full(wh), full(wl), full(wdth), full(wdtl)],
        out_specs=[pl.BlockSpec(s.shape, lambda: (0, 0)) for s in out_shapes],
        out_shape=out_shapes,
        compiler_params=pltpu.CompilerParams(vmem_limit_bytes=VMEM_LIMIT),
        name="in_proj_sample",
    )(x, nw, wh, wl, wdth, wdtl)


_PROMPT_NN = (SSM_WIDTH, CONV_DIM, ATT_WIDTH, X_WIDTH, X_WIDTH, ATT_WIDTH)


def _inproj_prompt_kernel(x_ref, nw_ref, wn_ref, wt_ref, z_ref, xbc_ref, ga_ref, qx_ref, gx_ref, kb_ref,
                          km_ref, qT_ref, kT_ref, vT_ref, vTb_ref, dtT_ref):
    xh = _rms(x_ref[...], nw_ref[...]).astype(BF16)
    lo = 0
    for o_ref, width in zip((z_ref, xbc_ref, ga_ref, qx_ref, gx_ref), _PROMPT_NN[:-1]):
        o_ref[...] = _dot(xh, wn_ref[:, lo:lo + width])
        lo += width
    k = _dot(xh, wn_ref[:, lo:lo + ATT_WIDTH])
    kb_ref[0] = k.astype(BF16)
    km_ref[0] = jnp.mean(k, axis=0, keepdims=True)
    qT_ref[0] = _dot_nt(wt_ref[0:ATT_WIDTH, :], xh)
    kT_ref[...] = _dot_nt(wt_ref[ATT_WIDTH:2 * ATT_WIDTH, :], xh)
    vT = _dot_nt(wt_ref[2 * ATT_WIDTH:3 * ATT_WIDTH, :], xh)
    vT_ref[...] = vT
    vTb_ref[0] = vT.astype(BF16)
    dtT_ref[...] = _dot_nt(wt_ref[3 * ATT_WIDTH:, :], xh)


def _in_proj_prompt(x, norm_w, w_in):
    m, d = x.shape
    tm = MOBA_BLOCK
    nb = m // tm
    c_z, c_dt = 0, SSM_WIDTH + CONV_DIM
    c_q = c_dt + SSM_HEADS
    c_k, c_ga = c_q + ATT_WIDTH, c_q + 3 * ATT_WIDTH
    cols = lambda lo, w: w_in[:, lo:lo + w]
    wn = jnp.concatenate([cols(c_z, SSM_WIDTH + CONV_DIM), cols(c_ga, ATT_WIDTH + 2 * X_WIDTH),
                          cols(c_k, ATT_WIDTH)], axis=1).astype(BF16)
    wt = jnp.concatenate([cols(c_q, 3 * ATT_WIDTH), cols(c_dt, SSM_HEADS)], axis=1).T.astype(BF16)
    nw = norm_w.reshape(1, d)
    row = lambda w: pl.BlockSpec((tm, w), lambda i: (i, 0))
    col = lambda h: pl.BlockSpec((h, tm), lambda i: (0, i))
    full = lambda a: pl.BlockSpec(a.shape, lambda i: (0,) * a.ndim)
    tile3 = lambda a, b: pl.BlockSpec((1, a, b), lambda i: (i, 0, 0))
    out_specs = [row(SSM_WIDTH), row(CONV_DIM), row(ATT_WIDTH), row(X_WIDTH), row(X_WIDTH),
                 tile3(tm, ATT_WIDTH), tile3(1, ATT_WIDTH), tile3(ATT_WIDTH, tm),
                 col(ATT_WIDTH), col(ATT_WIDTH), tile3(ATT_WIDTH, tm), col(SSM_HEADS)]
    sds = jax.ShapeDtypeStruct
    out_shape = [sds((m, SSM_WIDTH), F32), sds((m, CONV_DIM), F32), sds((m, ATT_WIDTH), F32),
                 sds((m, X_WIDTH), F32), sds((m, X_WIDTH), F32),
                 sds((nb, tm, ATT_WIDTH), BF16), sds((nb, 1, ATT_WIDTH), F32), sds((nb, ATT_WIDTH, tm), F32),
                 sds((ATT_WIDTH, m), F32), sds((ATT_WIDTH, m), F32), sds((nb, ATT_WIDTH, tm), BF16),
                 sds((SSM_HEADS, m), F32)]
    return pl.pallas_call(
        _inproj_prompt_kernel,
        grid=(nb,),
        in_specs=[row(d), full(nw), full(wn), full(wt)],
        out_specs=out_specs,
        out_shape=out_shape,
        compiler_params=_cparams(("parallel",)),
        name="in_proj_prompt",
    )(x, nw, wn, wt)


def _ssd_kernel(xbc_ref, dtT_ref, z_ref, convw_ref, convb_ref, dtb_ref, alog_ref, dskip_ref, nw_ref,
                e_ref, y_ref, hT_ref, tail_sc, h_sc):
    c = pl.program_id(0)
    L = SSM_CHUNK

    @pl.when(c == 0)
    def _():
        tail_sc[...] = jnp.zeros_like(tail_sc)
        h_sc[...] = jnp.zeros_like(h_sc)

    x = xbc_ref[...]
    tail = tail_sc[...]
    w = convw_ref[...]
    row8 = lax.broadcasted_iota(jnp.int32, (SUBLANES, CONV_DIM), 0)
    acc = x * w[CONV_W - 1:CONV_W, :] + convb_ref[...]
    for k in range(1, CONV_W):
        rolled = pltpu.roll(x, k, 0)
        fix = pltpu.roll(tail, k, 0)
        top = jnp.where(row8 < k, fix, rolled[0:SUBLANES])
        shifted = jnp.concatenate([top, rolled[SUBLANES:]], axis=0)
        acc = acc + shifted * w[CONV_W - 1 - k:CONV_W - k, :]
    tail_sc[...] = x[L - SUBLANES:L]
    u = _silu(acc)
    xs = u[:, :SSM_WIDTH]
    bm = u[:, SSM_WIDTH:SSM_WIDTH + SSM_GROUPS * SSM_STATE]
    cm = u[:, SSM_WIDTH + SSM_GROUPS * SSM_STATE:]

    dtT = _softplus(dtT_ref[...] + dtb_ref[...])
    aT = dtT * (-jnp.exp(alog_ref[...]))
    r_i = lax.broadcasted_iota(jnp.int32, (L, L), 0)
    c_i = lax.broadcasted_iota(jnp.int32, (L, L), 1)
    upper = (r_i <= c_i).astype(BF16)
    acumT = _dot_f32_exactrhs(aT, upper)
    a_last = acumT[:, L - 1:L]
    to_endT = jnp.exp(a_last - acumT) * dtT
    eacT = jnp.exp(acumT)
    stack = jnp.concatenate(
        [acumT, dtT, to_endT, eacT, jnp.zeros((L - 4 * SSM_HEADS, L), F32)], axis=0)
    st = stack.T
    e = e_ref[...]
    dt_e = _dot_f32_exactrhs(st[:, SSM_HEADS:2 * SSM_HEADS], e, 2)
    toend_e = _dot_f32_exactrhs(st[:, 2 * SSM_HEADS:3 * SSM_HEADS], e, 2)
    eac_e = _dot_f32_exactrhs(st[:, 3 * SSM_HEADS:4 * SSM_HEADS], e, 2)
    cd_e = eac_e[L - 1:L, :]

    xd = (xs * dt_e).astype(BF16)
    xw = (xs * toend_e).astype(BF16)
    causal = r_i >= c_i
    ys = []
    yoffs = []
    for g in range(SSM_GROUPS):
        bg = bm[:, g * SSM_STATE:(g + 1) * SSM_STATE].astype(BF16)
        cg = cm[:, g * SSM_STATE:(g + 1) * SSM_STATE].astype(BF16)
        cb = jnp.where(causal, _dot_nt(cg, bg), 0.0)
        for hh in range(SSM_HG):
            h = g * SSM_HG + hh
            seg = st[:, h:h + 1] - acumT[h:h + 1, :]
            mix = (cb * jnp.exp(jnp.minimum(seg, 0.0))).astype(BF16)
            ys.append(_dot(mix, xd[:, h * SSM_HEAD_DIM:(h + 1) * SSM_HEAD_DIM]))
        hprev = h_sc[g]
        yoffs.append(_dot(cg, hprev.astype(BF16)))
        s_new = _dot_tn(bg, xw[:, g * SSM_GW:(g + 1) * SSM_GW])
        h_sc[g] = hprev * cd_e[:, g * SSM_GW:(g + 1) * SSM_GW] + s_new

    y = jnp.concatenate(ys, axis=1) + jnp.concatenate(yoffs, axis=1) * eac_e + dskip_ref[...] * xs
    gt = y * _silu(z_ref[...])
    outs = []
    for g in range(SSM_GROUPS):
        gg = gt[:, g * SSM_GW:(g + 1) * SSM_GW]
        outs.append(gg * lax.rsqrt(jnp.mean(gg * gg, axis=-1, keepdims=True) + EPS))
    y_ref[...] = jnp.concatenate(outs, axis=1) * nw_ref[...]

    @pl.when(c == pl.num_programs(0) - 1)
    def _():
        hT_ref[...] = h_sc[...]


def _head_expand():
    return jnp.asarray(np.repeat(np.eye(SSM_HEADS, dtype=np.float32), SSM_HEAD_DIM, axis=1), BF16)


def _ssd_prompt(xbc, dtT, z, conv_w, conv_b, dt_bias, a_log, d_skip, ssm_norm_w):
    m = xbc.shape[0]
    L = SSM_CHUNK
    row = lambda w: pl.BlockSpec((L, w), lambda i: (i, 0))
    full = lambda a: pl.BlockSpec(a.shape, lambda i: (0,) * a.ndim)
    convb = conv_b.reshape(1, CONV_DIM)
    dtb = dt_bias.reshape(SSM_HEADS, 1)
    alog = a_log.reshape(SSM_HEADS, 1)
    dskip = jnp.repeat(d_skip, SSM_HEAD_DIM).reshape(1, SSM_WIDTH)
    nw = ssm_norm_w.reshape(1, SSM_WIDTH)
    e = _head_expand()
    y, hT = pl.pallas_call(
        _ssd_kernel,
        grid=(m // L,),
        in_specs=[row(CONV_DIM), pl.BlockSpec((SSM_HEADS, L), lambda i: (0, i)), row(SSM_WIDTH),
                  full(conv_w), full(convb), full(dtb), full(alog), full(dskip), full(nw), full(e)],
        out_specs=[row(SSM_WIDTH), pl.BlockSpec((SSM_GROUPS, SSM_STATE, SSM_GW), lambda i: (0, 0, 0))],
        out_shape=[jax.ShapeDtypeStruct((m, SSM_WIDTH), F32),
                   jax.ShapeDtypeStruct((SSM_GROUPS, SSM_STATE, SSM_GW), F32)],
        scratch_shapes=[pltpu.VMEM((SUBLANES, CONV_DIM), F32),
                        pltpu.VMEM((SSM_GROUPS, SSM_STATE, SSM_GW), F32)],
        compiler_params=_cparams(("arbitrary",)),
        name="ssd_prompt",
    )(xbc, dtT, z, conv_w, convb, dtb, alog, dskip, nw, e)
    state = hT.reshape(SSM_GROUPS, SSM_STATE, SSM_HG, SSM_HEAD_DIM).transpose(0, 2, 3, 1)
    return y, state.reshape(SSM_HEADS, SSM_HEAD_DIM, SSM_STATE)


def _bucket_np(dist):
    n = np.maximum(dist, 0)
    exact = RPE_BUCKETS // 2
    nf = np.maximum(n, 1).astype(np.float32)
    scaled = (np.log(nf / np.float32(exact)) / np.float32(math.log(RPE_MAX_DIST / exact))
              * np.float32(RPE_BUCKETS - exact))
    large = exact + scaled.astype(np.int32)
    return np.where(n < exact, n, np.minimum(large, RPE_BUCKETS - 1)).astype(np.int32)


def _bias_kernel(rb_ref, map_ref, out_ref, *, scale):
    h = pl.program_id(0)
    bmap = map_ref[...]
    acc = jnp.full(bmap.shape, -2.0 * BIG, F32)
    for b in range(RPE_BUCKETS):
        acc = jnp.where(bmap == b, rb_ref[b * ATT_HEADS + h] * scale, acc)
    out_ref[0] = acc


def _bias_table(rel_bias, bucket_map, scale=1.0):
    bmap = jnp.asarray(bucket_map, jnp.int32)
    nd = bmap.ndim
    return pl.pallas_call(
        functools.partial(_bias_kernel, scale=scale),
        grid=(ATT_HEADS,),
        in_specs=[pl.BlockSpec(memory_space=pltpu.SMEM),
                  pl.BlockSpec(bmap.shape, lambda h: (0,) * nd)],
        out_specs=pl.BlockSpec((1,) + bmap.shape, lambda h: (h,) + (0,) * nd),
        out_shape=jax.ShapeDtypeStruct((ATT_HEADS,) + bmap.shape, F32),
        compiler_params=_cparams(("parallel",)),
        name="rel_bias_table",
    )(rel_bias.reshape(-1), bmap)


def _topk_hits(s, blk, axis):
    picks = []
    for _ in range(MOBA_TOPK):
        mx = jnp.max(s, axis=axis, keepdims=True)
        idx = jnp.min(jnp.where(s == mx, blk, float(2 * LANES)), axis=axis, keepdims=True)
        picks.append((idx, jnp.where(mx > -jnp.inf, 1.0, 0.0)))
        s = jnp.where(blk == idx, -jnp.inf, s)
    return picks


_SPLIT = 4


def _moba_kernel(cfar_ref, qT_ref, kb_ref, vT_ref, km_ref, d0_ref, d1_ref, o_ref,
                 pen_sc, s_sc, p_sc, acc_sc, ml_sc):
    hp = pl.program_id(0)
    i = pl.program_id(1)
    W = MOBA_BLOCK
    D = ATT_HEAD_DIM
    nb = km_ref.shape[0]
    blk = lax.broadcasted_iota(jnp.int32, (nb, W), 0).astype(F32)
    i_f = i.astype(F32)
    near = jnp.maximum(i - 1, 0)
    q_augs, cfars = [], []
    acc_sc[...] = jnp.zeros_like(acc_sc)
    for hh in range(2):
        rows = slice(hh * D, (hh + 1) * D)
        cfar2 = cfar_ref[hp * 2 + hh] * LOG2E
        qT = qT_ref[0, rows, :]
        sT = _dot_hi(km_ref[:, rows], qT)
        sT = jnp.where(blk < i_f, sT, -jnp.inf)
        hit = jnp.zeros((nb, W), F32)
        for idx, valid in _topk_hits(sT, blk, 0):
            hit = jnp.maximum(hit, jnp.where(blk == idx, valid, 0.0))
        pen_sc[hh] = jnp.where(hit > 0.0, cfar2, -2.0 * BIG)

        qs = (qT * (D ** -0.5 * LOG2E)).astype(BF16)
        zero = jnp.zeros((D, W), BF16)
        q_augs.append(jnp.concatenate([qs, zero] if hh == 0 else [zero, qs], axis=0))
        cfars.append(cfar2)

    ml_sc[:, :, 0:1, :] = jnp.full((2, _SPLIT, 1, W), -BIG, F32)
    ml_sc[:, :, 1:2, :] = jnp.zeros((2, _SPLIT, 1, W), F32)

    def group(chains):
        for hh, a, j, bias_ref, _ in chains:
            s = _dot(kb_ref[j], q_augs[hh])
            s_sc[hh, a] = s if bias_ref is None else s + bias_ref[hh]
        for hh, a, j, _, pen in chains:
            s = s_sc[hh, a]
            m = ml_sc[hh, a, 0:1, :]
            m_new = jnp.maximum(m, jnp.max(s, axis=0, keepdims=True) + pen)
            alpha = jnp.exp2(m - m_new)
            p = jnp.exp2(s - (m_new - pen))
            ml_sc[hh, a, 1:2, :] = alpha * ml_sc[hh, a, 1:2, :] + jnp.sum(p, axis=0, keepdims=True)
            ml_sc[hh, a, 0:1, :] = m_new
            ml_sc[hh, a, 2:3, :] = alpha
            p_sc[hh, a] = p.astype(BF16)
        for hh, a, j, _, _ in chains:
            pv = _dot(vT_ref[j, hh * D:(hh + 1) * D, :], p_sc[hh, a])
            acc_sc[hh, a] = ml_sc[hh, a, 2:3, :] * acc_sc[hh, a] + pv

    zero_row = jnp.zeros((1, W), F32)
    group([(hh, 0, i, d0_ref, zero_row) for hh in range(2)]
          + [(hh, 1, near, d1_ref, pen_sc[hh, pl.ds(near, 1), :] - cfars[hh]) for hh in range(2)])

    def far_body(jj, carry):
        chains = []
        for a in range(_SPLIT):
            j = _SPLIT * jj + a
            jc = jnp.minimum(j, nb - 1)
            for hh in range(2):
                pen = jnp.where(j < near, pen_sc[hh, pl.ds(jc, 1), :], -2.0 * BIG)
                chains.append((hh, a, jc, None, pen))
        group(chains)
        return carry

    lax.fori_loop(0, (near + _SPLIT - 1) // _SPLIT, far_body, 0)

    outs = []
    for hh in range(2):
        m = ml_sc[hh, 0, 0:1, :]
        for a in range(1, _SPLIT):
            m = jnp.maximum(m, ml_sc[hh, a, 0:1, :])
        l = jnp.zeros((1, W), F32)
        acc = jnp.zeros((D, W), F32)
        for a in range(_SPLIT):
            wgt = jnp.exp2(ml_sc[hh, a, 0:1, :] - m)
            l = l + wgt * ml_sc[hh, a, 1:2, :]
            acc = acc + wgt * acc_sc[hh, a]
        outs.append(acc / l)
    o_ref[...] = jnp.concatenate(outs, axis=0)


def _moba_prompt(qT3, kb3, vT3, kmean, d0T, d1T, rel_bias):
    nb = qT3.shape[0]
    W = MOBA_BLOCK
    assert nb - 1 >= MOBA_TOPK and nb <= LANES
    cfar = rel_bias[RPE_BUCKETS - 1]
    return pl.pallas_call(
        _moba_kernel,
        grid=(ATT_HEADS // 2, nb),
        in_specs=[pl.BlockSpec(memory_space=pltpu.SMEM),
                  pl.BlockSpec((1, LANES, W), lambda hp, i: (i, hp, 0)),
                  pl.BlockSpec((nb, W, LANES), lambda hp, i: (0, 0, hp)),
                  pl.BlockSpec((nb, LANES, W), lambda hp, i: (0, hp, 0)),
                  pl.BlockSpec((nb, LANES), lambda hp, i: (0, hp)),
                  pl.BlockSpec((2, W, W), lambda hp, i: (hp, 0, 0)),
                  pl.BlockSpec((2, W, W), lambda hp, i: (hp, 0, 0))],
        out_specs=pl.BlockSpec((LANES, W), lambda hp, i: (hp, i)),
        out_shape=jax.ShapeDtypeStruct((ATT_WIDTH, nb * W), F32),
        scratch_shapes=[pltpu.VMEM((2, nb, W), F32),
                        pltpu.VMEM((2, _SPLIT, W, W), F32),
                        pltpu.VMEM((2, _SPLIT, W, W), BF16),
                        pltpu.VMEM((2, _SPLIT, ATT_HEAD_DIM, W), F32),
                        pltpu.VMEM((2, _SPLIT, SUBLANES, W), F32)],
        compiler_params=_cparams(("parallel", "arbitrary")),
        name="moba_prompt",
    )(cfar, qT3, kb3, vT3, kmean, d0T, d1T)


def _memkv_kernel(mem_ref, g_ref, w_ref, k_ref, v_ref):
    xn = _rms(mem_ref[...], g_ref[...]).astype(BF16)
    kv = _dot(xn, w_ref[...])
    k_ref[...] = kv[:, :X_WIDTH]
    v_ref[...] = kv[:, X_WIDTH:]


def _memory_kv(mem, g, w):
    nm, d = mem.shape
    full = lambda a: pl.BlockSpec(a.shape, lambda: (0,) * a.ndim)
    g2 = g.reshape(1, d)
    wb = w.astype(BF16)
    return pl.pallas_call(
        _memkv_kernel,
        in_specs=[full(mem), full(g2), full(wb)],
        out_specs=[pl.BlockSpec((nm, X_WIDTH), lambda: (0, 0))] * 2,
        out_shape=[jax.ShapeDtypeStruct((nm, X_WIDTH), F32)] * 2,
        compiler_params=pltpu.CompilerParams(vmem_limit_bytes=VMEM_LIMIT),
        name="memory_kv",
    )(mem, g2, wb)


def _xattn_kernel(q_ref, gx_ref, mk_ref, mv_ref, o_ref):
    q = q_ref[...].reshape(-1, X_WIDTH)
    gx = gx_ref[...].reshape(-1, X_WIDTH)
    mk = mk_ref[...].reshape(-1, X_WIDTH).astype(BF16)
    mv = mv_ref[...].reshape(-1, X_WIDTH).astype(BF16)
    outs = []
    for h in range(X_HEADS):
        sl = slice(h * X_HEAD_DIM, (h + 1) * X_HEAD_DIM)
        s = _dot_nt((q[:, sl] * (X_HEAD_DIM ** -0.5)).astype(BF16), mk[:, sl])
        m = jnp.max(s, axis=1, keepdims=True)
        p = jnp.exp(s - m)
        l = jnp.sum(p, axis=1, keepdims=True)
        outs.append(_dot(p.astype(BF16), mv[:, sl]) / l)
    o_ref[...] = (jnp.concatenate(outs, axis=1) * _silu(gx)).reshape(o_ref.shape)


def _xattn_prompt(qx, gx, mk, mv, *, tq):
    m = qx.shape[0]
    row = pl.BlockSpec((tq, X_WIDTH), lambda i: (i, 0))
    full = pl.BlockSpec(mk.shape, lambda i: (0, 0))
    return pl.pallas_call(
        _xattn_kernel,
        grid=(m // tq,),
        in_specs=[row, row, full, full],
        out_specs=row,
        out_shape=jax.ShapeDtypeStruct((m, X_WIDTH), F32),
        compiler_params=_cparams(("parallel",)),
        name="xattn_prompt",
    )(qx, gx, mk, mv)


def _xattn_sample(qx, gx, mk, mv):
    b, t8, _ = qx.shape
    nm = mk.shape[1]
    qs = pl.BlockSpec((1, t8, X_WIDTH), lambda i: (i, 0, 0))
    ms = pl.BlockSpec((1, nm, X_WIDTH), lambda i: (i, 0, 0))
    return pl.pallas_call(
        _xattn_kernel,
        grid=(b,),
        in_specs=[qs, qs, ms, ms],
        out_specs=qs,
        out_shape=jax.ShapeDtypeStruct((b, t8, X_WIDTH), F32),
        compiler_params=_cparams(("parallel",)),
        name="xattn_sample",
    )(qx, gx, mk, mv)


def _outproj_kernel(x_ref, ys_ref, ya_ref, ga_ref, yx_ref, w_ref, fw_ref, o_ref):
    att = (ya_ref[...] * _silu(ga_ref[...])).astype(BF16)
    h = x_ref[...] + _dot(ys_ref[...].astype(BF16), w_ref[0:SSM_WIDTH, :])
    h = h + _dot(att, w_ref[SSM_WIDTH:SSM_WIDTH + ATT_WIDTH, :])
    h = h + _dot(yx_ref[...].astype(BF16), w_ref[SSM_WIDTH + ATT_WIDTH:, :])
    o_ref[...] = _rms(h, fw_ref[...])


def _out_proj(x, y_ssd, y_att, ga, y_x, w_out, final_norm_w, *, tm):
    m, d = x.shape
    row = lambda w: pl.BlockSpec((tm, w), lambda i: (i, 0))
    full = lambda a: pl.BlockSpec(a.shape, lambda i: (0,) * a.ndim)
    wb = w_out.astype(BF16)
    fw = final_norm_w.reshape(1, d)
    return pl.pallas_call(
        _outproj_kernel,
        grid=(m // tm,),
        in_specs=[row(d), row(SSM_WIDTH), row(ATT_WIDTH), row(ATT_WIDTH), row(X_WIDTH), full(wb), full(fw)],
        out_specs=row(d),
        out_shape=jax.ShapeDtypeStruct((m, d), F32),
        compiler_params=_cparams(("parallel",)),
        name="out_proj",
    )(x, y_ssd, y_att, ga, y_x, wb, fw)


def _ssd_sample_kernel(xT_ref, cT_ref, xr_ref, cr_ref, dt_ref, zT_ref, h0_ref,
                       wT_ref, bT_ref, wr_ref, br_ref, dtb_ref, alog_ref, dskT_ref, nwT_ref,
                       yT_ref, h_ref, *, T):
    cT = cT_ref[0]
    xT = xT_ref[0]
    ext_col = lambda i: cT[:, i:i + 1] if i < CONV_W - 1 else xT[:, i - (CONV_W - 1):i - (CONV_W - 2)]
    wT = wT_ref[...]
    cr = cr_ref[0]
    xr = xr_ref[0]
    ext_row = lambda i: cr[i:i + 1, :] if i < CONV_W - 1 else xr[i - (CONV_W - 1):i - (CONV_W - 2), :]
    wr = wr_ref[...]
    dt = _softplus(dt_ref[0] + dtb_ref[...])
    a = dt * (-jnp.exp(alog_ref[...]))
    dec = jnp.exp(a)
    for g in range(SSM_GROUPS):
        for hh in range(SSM_HG):
            h_ref[0, g * SSM_HG + hh] = h0_ref[0, g * SSM_HG + hh]
    for t in range(T):
        uT = bT_ref[...]
        uR = br_ref[...]
        for j in range(CONV_W):
            uT = uT + wT[:, j:j + 1] * ext_col(t + j)
            uR = uR + wr[j:j + 1, :] * ext_row(t + j)
        uT = _silu(uT)
        uR = _silu(uR)
        ycols = []
        for h in range(SSM_HEADS):
            g = h // SSM_HG
            xcol = uT[h * SSM_HEAD_DIM:(h + 1) * SSM_HEAD_DIM, :]
            brow = uR[:, SSM_WIDTH + g * SSM_STATE:SSM_WIDTH + (g + 1) * SSM_STATE]
            crow = uR[:, SSM_WIDTH + (SSM_GROUPS + g) * SSM_STATE:
                      SSM_WIDTH + (SSM_GROUPS + g + 1) * SSM_STATE]
            hs = h_ref[0, h] * dec[h:h + 1, t:t + 1] + (xcol * dt[h:h + 1, t:t + 1]) * brow
            h_ref[0, h] = hs
            ycols.append(jnp.sum(hs * crow, axis=1, keepdims=True))
        y = jnp.concatenate(ycols, axis=0) + dskT_ref[...] * uT[:SSM_WIDTH, :]
        zt = zT_ref[0][:, t:t + 1]
        gt = y * _silu(zt)
        outs = []
        for g in range(SSM_GROUPS):
            gg = gt[g * SSM_GW:(g + 1) * SSM_GW, :]
            outs.append(gg * lax.rsqrt(jnp.mean(gg * gg, axis=0, keepdims=True) + EPS))
        yT_ref[0, :, t:t + 1] = jnp.concatenate(outs, axis=0) * nwT_ref[...]


def _ssd_sample(xbc, state_conv, dt_raw, z, state_ssm, conv_w, conv_b, dt_bias, a_log, d_skip, ssm_norm_w):
    b, T, _ = xbc.shape
    xT = xbc.transpose(0, 2, 1)
    cT = state_conv.transpose(0, 2, 1)
    dtT = dt_raw.transpose(0, 2, 1)
    zT = z.transpose(0, 2, 1)
    per = lambda a: pl.BlockSpec((1,) + a.shape[1:], lambda i: (i,) + (0,) * (a.ndim - 1))
    full = lambda a: pl.BlockSpec(a.shape, lambda i: (0,) * a.ndim)
    wT = conv_w.T
    bT = conv_b.reshape(CONV_DIM, 1)
    br = conv_b.reshape(1, CONV_DIM)
    dtb = dt_bias.reshape(SSM_HEADS, 1)
    alog = a_log.reshape(SSM_HEADS, 1)
    dskT = jnp.repeat(d_skip, SSM_HEAD_DIM).reshape(SSM_WIDTH, 1)
    nwT = ssm_norm_w.reshape(SSM_WIDTH, 1)
    args = (xT, cT, xbc, state_conv, dtT, zT, state_ssm, wT, bT, conv_w, br, dtb, alog, dskT, nwT)
    in_specs = [per(a) for a in args[:7]] + [full(a) for a in args[7:]]
    yT, h = pl.pallas_call(
        functools.partial(_ssd_sample_kernel, T=T),
        grid=(b,),
        in_specs=in_specs,
        out_specs=[pl.BlockSpec((1, SSM_WIDTH, T), lambda i: (i, 0, 0)), per(state_ssm)],
        out_shape=[jax.ShapeDtypeStruct((b, SSM_WIDTH, T), F32),
                   jax.ShapeDtypeStruct(state_ssm.shape, F32)],
        compiler_params=_cparams(("parallel",)),
        name="ssd_sample",
    )(*args)
    return yT.transpose(0, 2, 1), h


_KM_PAGES = 16


def _kmean_pages_kernel(pt_ref, *refs):
    o_ref = refs[-1]
    si = pl.program_id(1)
    nb = o_ref.shape[2]

    @pl.when(si == 0)
    def _():
        o_ref[...] = jnp.zeros_like(o_ref)

    lane = lax.broadcasted_iota(jnp.int32, (ATT_WIDTH, nb), 1)
    acc = o_ref[0]
    for b in range(_KM_PAGES // PAGES_PER_BLOCK):
        tok = refs[b * PAGES_PER_BLOCK][0].reshape(ATT_WIDTH, PAGE_SIZE)
        for pg in range(1, PAGES_PER_BLOCK):
            tok = tok + refs[b * PAGES_PER_BLOCK + pg][0].reshape(ATT_WIDTH, PAGE_SIZE)
        mean = jnp.sum(tok, axis=1, keepdims=True) * (1.0 / MOBA_BLOCK)
        acc = jnp.where(lane == si * (_KM_PAGES // PAGES_PER_BLOCK) + b, mean, acc)
    o_ref[0] = acc


def _kmean_sample(pool_kt, page_table):
    b, n_pages = page_table.shape
    nb = n_pages // PAGES_PER_BLOCK
    steps = n_pages // _KM_PAGES
    page_spec = lambda r: pl.BlockSpec(
        (1, ATT_HEADS, ATT_HEAD_DIM, PAGE_SIZE),
        lambda bi, si, pt: (pt[bi * n_pages + si * _KM_PAGES + r], 0, 0, 0))
    grid_spec = pltpu.PrefetchScalarGridSpec(
        num_scalar_prefetch=1,
        grid=(b, steps),
        in_specs=[page_spec(r) for r in range(_KM_PAGES)],
        out_specs=pl.BlockSpec((1, ATT_WIDTH, nb), lambda bi, si, pt: (bi, 0, 0)),
    )
    return pl.pallas_call(
        _kmean_pages_kernel,
        grid_spec=grid_spec,
        out_shape=jax.ShapeDtypeStruct((b, ATT_WIDTH, nb), F32),
        compiler_params=_cparams(("parallel", "arbitrary")),
        name="moba_kmean_pages",
    )(page_table.reshape(-1), *([pool_kt] * _KM_PAGES))


def _select_sample_kernel(q_ref, km_ref, o_ref):
    q = q_ref[0]
    nb = km_ref.shape[2]
    t8 = q.shape[0]
    blk = lax.broadcasted_iota(jnp.int32, (t8, nb), 1).astype(F32)
    lane = lax.broadcasted_iota(jnp.int32, (t8, LANES), 1)
    for h in range(ATT_HEADS):
        rows = slice(h * ATT_HEAD_DIM, (h + 1) * ATT_HEAD_DIM)
        s = _dot_hi(q[:, rows], km_ref[0, rows, :])
        tile = jnp.zeros((t8, LANES), F32)
        for r, (idx, _) in enumerate(_topk_hits(s, blk, 1)):
            tile = jnp.where(lane == r, idx, tile)
        o_ref[0, h] = tile.astype(jnp.int32)


def _select_sample(q8, kmeanT):
    b, t8, _ = q8.shape
    nb = kmeanT.shape[2]
    assert nb >= MOBA_TOPK
    return pl.pallas_call(
        _select_sample_kernel,
        grid=(b,),
        in_specs=[pl.BlockSpec((1, t8, ATT_WIDTH), lambda i: (i, 0, 0)),
                  pl.BlockSpec((1, ATT_WIDTH, nb), lambda i: (i, 0, 0))],
        out_specs=pl.BlockSpec((1, ATT_HEADS, t8, LANES), lambda i: (i, 0, 0, 0)),
        out_shape=jax.ShapeDtypeStruct((b, ATT_HEADS, t8, LANES), jnp.int32),
        compiler_params=_cparams(("parallel",)),
        name="moba_select_sample",
    )(q8, kmeanT)


_N_SLAB = MOBA_TOPK * PAGES_PER_BLOCK


def _attend_sample_kernel(sel_ref, pt_ref, rb_ref, q_ref, kn_ref, vn_ref, near_ref, pk_ref, pv_ref,
                          o_ref, kbuf, vbuf, sem, *, T, n_pages):
    bi = pl.program_id(0)
    t = pl.program_id(1)
    step = bi * T + t
    nsteps = pl.num_programs(0) * T
    nb = n_pages // PAGES_PER_BLOCK
    D = ATT_HEAD_DIM

    def copies(s, slot):
        b_s = s // T
        out = []
        for h in range(ATT_HEADS):
            for r in range(MOBA_TOPK):
                blk = sel_ref[(s * ATT_HEADS + h) * MOBA_TOPK + r]
                for pg in range(PAGES_PER_BLOCK):
                    phys = pt_ref[b_s * n_pages + blk * PAGES_PER_BLOCK + pg]
                    i = r * PAGES_PER_BLOCK + pg
                    out.append(pltpu.make_async_copy(pk_ref.at[phys, h], kbuf.at[slot, h, i], sem.at[0, slot]))
                    out.append(pltpu.make_async_copy(pv_ref.at[phys, h], vbuf.at[slot, h, i], sem.at[1, slot]))
        return out

    @pl.when(step == 0)
    def _():
        for cp in copies(step, 0):
            cp.start()

    @pl.when(step + 1 < nsteps)
    def _():
        for cp in copies(step + 1, (step + 1) % 2):
            cp.start()

    slot = step % 2
    for cp in copies(step, slot):
        cp.wait()

    lane_t = lax.broadcasted_iota(jnp.int32, (1, T), 1)
    for h in range(ATT_HEADS):
        rows = slice(h * D, (h + 1) * D)
        qc = q_ref[0, rows, :]
        cfar = rb_ref[(RPE_BUCKETS - 1) * ATT_HEADS + h]
        s_rows = []
        for r in range(MOBA_TOPK):
            blk = sel_ref[(step * ATT_HEADS + h) * MOBA_TOPK + r]
            for pg in range(PAGES_PER_BLOCK):
                s = jnp.sum(kbuf[slot, h, r * PAGES_PER_BLOCK + pg] * qc, axis=0, keepdims=True)
                s_rows.append(s + jnp.where(blk == nb - 1, near_ref[h, 0, pg:pg + 1, :], cfar))
        s_own = jnp.sum(kn_ref[0, rows, :] * qc, axis=0, keepdims=True)
        b_own = jnp.zeros((1, T), F32)
        for tp in range(T):
            b_own = jnp.where(lane_t == tp, rb_ref[jnp.maximum(t - tp, 0) * ATT_HEADS + h], b_own)
        s_own = jnp.where(lane_t <= t, s_own + b_own, NEG)
        m_row = s_rows[0]
        for s in s_rows[1:]:
            m_row = jnp.maximum(m_row, s)
        m = jnp.maximum(jnp.max(m_row, axis=1, keepdims=True), jnp.max(s_own, axis=1, keepdims=True))
        p_own = jnp.exp(s_own - m)
        l_row = jnp.zeros((1, PAGE_SIZE), F32)
        acc = jnp.zeros((D, PAGE_SIZE), F32)
        for n, s in enumerate(s_rows):
            p = jnp.exp(s - m)
            l_row = l_row + p
            acc = acc + vbuf[slot, h, n] * p
        l = jnp.sum(l_row, axis=1, keepdims=True) + jnp.sum(p_own, axis=1, keepdims=True)
        o = jnp.sum(acc, axis=1, keepdims=True)
        vn = vn_ref[0, rows, :]
        for tp in range(T):
            o = o + vn[:, tp:tp + 1] * p_own[:, tp:tp + 1]
        o_ref[0, rows, :] = o / l


def _attend_sample(sel, page_table, rel_bias, q_col, k_newT, v_newT, near, pool_kt, pool_vt):
    b, _, T = k_newT.shape
    n_pages = page_table.shape[1]
    assert T <= RPE_BUCKETS // 2
    slab = (2, ATT_HEADS, _N_SLAB, ATT_HEAD_DIM, PAGE_SIZE)
    grid_spec = pltpu.PrefetchScalarGridSpec(
        num_scalar_prefetch=3,
        grid=(b, T),
        in_specs=[pl.BlockSpec((1, ATT_WIDTH, 1), lambda bi, t, *_: (bi * T + t, 0, 0)),
                  pl.BlockSpec((1, ATT_WIDTH, T), lambda bi, t, *_: (bi, 0, 0)),
                  pl.BlockSpec((1, ATT_WIDTH, T), lambda bi, t, *_: (bi, 0, 0)),
                  pl.BlockSpec((ATT_HEADS, 1, PAGES_PER_BLOCK, PAGE_SIZE), lambda bi, t, *_: (0, t, 0, 0)),
                  pl.BlockSpec(memory_space=pl.ANY),
                  pl.BlockSpec(memory_space=pl.ANY)],
        out_specs=pl.BlockSpec((1, ATT_WIDTH, 1), lambda bi, t, *_: (bi * T + t, 0, 0)),
        scratch_shapes=[pltpu.VMEM(slab, F32), pltpu.VMEM(slab, F32), pltpu.SemaphoreType.DMA((2, 2))],
    )
    return pl.pallas_call(
        functools.partial(_attend_sample_kernel, T=T, n_pages=n_pages),
        grid_spec=grid_spec,
        out_shape=jax.ShapeDtypeStruct((b * T, ATT_WIDTH, 1), F32),
        compiler_params=_cparams(("arbitrary", "arbitrary")),
        name="moba_attend_sample",
    )(sel, page_table.reshape(-1), rel_bias.reshape(-1), q_col, k_newT, v_newT, near, pool_kt, pool_vt)


def _prompt_layer(x, mem, norm_w, w_in, conv_w, conv_b, dt_bias, a_log, d_skip, ssm_norm_w,
                  mem_norm_w, w_mem_kv, w_out, rel_bias, final_norm_w):
    s = x.shape[0]
    W = MOBA_BLOCK
    z, xbc, ga, qx, gx, kb3, km3, qT3, kT, vT, vT3, dtT = _in_proj_prompt(x, norm_w, w_in)
    y_ssd, state = _ssd_prompt(xbc, dtT, z, conv_w, conv_b, dt_bias, a_log, d_skip, ssm_norm_w)
    ki = np.arange(W)[:, None]
    qi = np.arange(W)[None, :]
    own_map = np.where(ki <= qi, _bucket_np(qi - ki), -1)
    d01 = _bias_table(rel_bias, np.stack([own_map, _bucket_np(W + qi - ki)]), LOG2E)
    y_attT = _moba_prompt(qT3, kb3, vT3, km3.reshape(-1, ATT_WIDTH), d01[:, 0], d01[:, 1], rel_bias)
    mk, mv = _memory_kv(mem, mem_norm_w, w_mem_kv)
    y_x = _xattn_prompt(qx, gx, mk, mv, tq=512)
    y = _out_proj(x, y_ssd, y_attT.T, ga, y_x, w_out, final_norm_w, tm=512)
    conv_state = xbc[s - (CONV_W - 1):]
    heads = lambda aT: aT.reshape(ATT_HEADS, ATT_HEAD_DIM, s).transpose(2, 0, 1)
    return y, heads(kT), heads(vT), mk, mv, state, conv_state


def _sample_layer(x, cache_k, cache_v, page_table, mem_k, mem_v, state_ssm, state_conv, norm_w, w_in,
                  conv_w, conv_b, dt_bias, a_log, d_skip, ssm_norm_w, w_out, rel_bias, final_norm_w):
    b, T, d = x.shape
    n_pages = page_table.shape[1]
    assert (n_pages * PAGE_SIZE) % MOBA_BLOCK == 0 and n_pages % _KM_PAGES == 0
    assert CONV_W - 1 <= T <= SUBLANES
    xf = x.reshape(b * T, d)
    z, xbc, q, k, v, ga, qx, gx, dtT = _in_proj_sample(xf, norm_w, w_in)
    per = lambda a: a.reshape(b, T, a.shape[-1])
    y_ssd, h_new = _ssd_sample(per(xbc), state_conv, per(dtT.T), per(z), state_ssm,
                               conv_w, conv_b, dt_bias, a_log, d_skip, ssm_norm_w)
    pad8 = lambda a: jnp.pad(per(a), ((0, 0), (0, SUBLANES - T), (0, 0)))
    pool_kt = cache_k.transpose(0, 2, 3, 1)
    pool_vt = cache_v.transpose(0, 2, 3, 1)
    kmeanT = _kmean_sample(pool_kt, page_table)
    sel = _select_sample(pad8(q), kmeanT)[:, :, :T, :MOBA_TOPK]
    sel = sel.transpose(0, 2, 1, 3).reshape(-1)
    kk = (np.arange(PAGES_PER_BLOCK)[:, None] * PAGE_SIZE + np.arange(PAGE_SIZE)[None, :])[None]
    tt = np.arange(T)[:, None, None]
    near = _bias_table(rel_bias, _bucket_np(MOBA_BLOCK + tt - kk))
    q_col = (q * (ATT_HEAD_DIM ** -0.5)).reshape(b * T, ATT_WIDTH, 1)
    y_att = _attend_sample(sel, page_table, rel_bias, q_col, per(k).transpose(0, 2, 1),
                           per(v).transpose(0, 2, 1), near, pool_kt, pool_vt)
    y_att = y_att.reshape(b * T, ATT_WIDTH)
    y_x = _xattn_sample(pad8(qx), pad8(gx), mem_k.reshape(b, -1, X_WIDTH), mem_v.reshape(b, -1, X_WIDTH))
    y_x = y_x[:, :T].reshape(b * T, X_WIDTH)
    y = _out_proj(xf, y_ssd.reshape(b * T, SSM_WIDTH), y_att, ga, y_x, w_out, final_norm_w, tm=b * T)
    conv_state = per(xbc)[:, T - (CONV_W - 1):]
    return y.reshape(b, T, d), per(k), per(v), h_new, conv_state


def kernel(x_prompt, x_sample, mem_prompt, cache_k, cache_v, page_table, cache_mem_k, cache_mem_v,
           state_ssm, state_conv, norm_w, w_in, conv_w, conv_b, dt_bias, a_log, d_skip, ssm_norm_w,
           mem_norm_w, w_mem_kv, w_out, rel_bias, final_norm_w):
    bp, s, d = x_prompt.shape
    bs, T, _ = x_sample.shape
    depth = w_in.shape[0]
    assert bp == 1 and depth == 1
    l = 0
    y_p, k_p, v_p, mk, mv, ssm_p, conv_p = _prompt_layer(
        x_prompt[0], mem_prompt[0], norm_w[l], w_in[l], conv_w[l], conv_b[l], dt_bias[l], a_log[l],
        d_skip[l], ssm_norm_w[l], mem_norm_w[l], w_mem_kv[l], w_out[l], rel_bias, final_norm_w)
    y_s, k_s, v_s, ssm_s, conv_s = _sample_layer(
        x_sample, cache_k[l], cache_v[l], page_table, cache_mem_k[l], cache_mem_v[l], state_ssm[l],
        state_conv[l], norm_w[l], w_in[l], conv_w[l], conv_b[l], dt_bias[l], a_log[l], d_skip[l],
        ssm_norm_w[l], w_out[l], rel_bias, final_norm_w)
    nm = mem_prompt.shape[1]
    return (y_p[None],
            y_s,
            k_p[None, None],
            v_p[None, None],
            mk.reshape(1, 1, nm, X_HEADS, X_HEAD_DIM),
            mv.reshape(1, 1, nm, X_HEADS, X_HEAD_DIM),
            ssm_p[None, None],
            conv_p[None, None],
            k_s.reshape(1, bs, T, ATT_HEADS, ATT_HEAD_DIM),
            v_s.reshape(1, bs, T, ATT_HEADS, ATT_HEAD_DIM),
            ssm_s[None],
            conv_s[None])
```

```python
import functools
import math

import numpy as np
import jax
import jax.numpy as jnp
from jax import lax
from jax.experimental import pallas as pl
from jax.experimental.pallas import tpu as pltpu

F32 = jnp.float32
BF16 = jnp.bfloat16
EPS = 1e-6

SSM_HEADS = 16
SSM_HEAD_DIM = 64
SSM_GROUPS = 2
SSM_HG = SSM_HEADS // SSM_GROUPS
SSM_STATE = 128
SSM_WIDTH = SSM_HEADS * SSM_HEAD_DIM
SSM_GW = SSM_WIDTH // SSM_GROUPS
SSM_CHUNK = 128
CONV_W = 4
CONV_DIM = SSM_WIDTH + 2 * SSM_GROUPS * SSM_STATE
ATT_HEADS = 8
ATT_HEAD_DIM = 64
ATT_WIDTH = ATT_HEADS * ATT_HEAD_DIM
MOBA_BLOCK = 256
MOBA_TOPK = 3
PAGE_SIZE = 128
PAGES_PER_BLOCK = MOBA_BLOCK // PAGE_SIZE
RPE_BUCKETS = 32
RPE_MAX_DIST = 128
X_HEADS = 4
X_HEAD_DIM = 128
X_WIDTH = X_HEADS * X_HEAD_DIM

LANES = 128
SUBLANES = 8
VMEM_LIMIT = 56 * 1024 * 1024
NEG = -1e30
BIG = 2.0 ** 100
LOG2E = math.log2(math.e)


def _cparams(sem, vmem=VMEM_LIMIT):
    return pltpu.CompilerParams(dimension_semantics=sem, vmem_limit_bytes=vmem)


def _split2(x):
    hi = x.astype(BF16)
    lo = (x - hi.astype(F32)).astype(BF16)
    return hi, lo


def _split3(x):
    hi = x.astype(BF16)
    r = x - hi.astype(F32)
    mid = r.astype(BF16)
    lo = (r - mid.astype(F32)).astype(BF16)
    return hi, mid, lo


def _dot(a, b):
    return jnp.dot(a, b, preferred_element_type=F32)


def _dot_nt(a, b):
    return lax.dot_general(a, b, (((1,), (1,)), ((), ())), preferred_element_type=F32)


def _dot_tn(a, b):
    return lax.dot_general(a, b, (((0,), (0,)), ((), ())), preferred_element_type=F32)


def _dot_f32_exactrhs(a, b_bf16, passes=3):
    parts = _split3(a) if passes == 3 else _split2(a)
    out = _dot(parts[0], b_bf16)
    for p in parts[1:]:
        out = out + _dot(p, b_bf16)
    return out


def _dot_hi(a, b):
    ah, al = _split2(a)
    bh, bl = _split2(b)
    return _dot(ah, bh) + _dot(al, bh) + _dot(ah, bl)


def _silu(x):
    return x / (1.0 + jnp.exp(-x))


def _softplus(x):
    return jnp.maximum(x, 0.0) + jnp.log1p(jnp.exp(-jnp.abs(x)))


def _rms(x, g):
    ms = jnp.mean(x * x, axis=-1, keepdims=True)
    return (x * lax.rsqrt(ms + EPS)) * g


_MAIN_SPLITS = (SSM_WIDTH, CONV_DIM, ATT_WIDTH, ATT_WIDTH, ATT_WIDTH, ATT_WIDTH, X_WIDTH, X_WIDTH)


def _inproj_kernel(x_ref, nw_ref, wh_ref, wl_ref, wdth_ref, wdtl_ref,
                   z_ref, xbc_ref, q_ref, k_ref, v_ref, ga_ref, qx_ref, gx_ref, dtT_ref):
    xn = _rms(x_ref[...], nw_ref[...])
    xh, xl = _split2(xn)
    outs = (z_ref, xbc_ref, q_ref, k_ref, v_ref, ga_ref, qx_ref, gx_ref)
    lo = 0
    for o_ref, width in zip(outs, _MAIN_SPLITS):
        w = wh_ref[:, lo:lo + width]
        o_ref[...] = _dot(xh, w) + _dot(xl, w) + _dot(xh, wl_ref[:, lo:lo + width])
        lo += width
    dtT_ref[...] = (_dot_nt(wdth_ref[...], xh) + _dot_nt(wdth_ref[...], xl) + _dot_nt(wdtl_ref[...], xh))


def _in_proj_sample(x, norm_w, w_in):
    m, d = x.shape
    cut0 = SSM_WIDTH + CONV_DIM
    w_main = jnp.concatenate([w_in[:, :cut0], w_in[:, cut0 + SSM_HEADS:]], axis=1)
    w_dt = w_in[:, cut0:cut0 + SSM_HEADS].T
    wh = w_main.astype(BF16)
    wdth = w_dt.astype(BF16)
    wl = (w_main - wh.astype(F32)).astype(BF16)
    wdtl = (w_dt - wdth.astype(F32)).astype(BF16)
    full = lambda a: pl.BlockSpec(a.shape, lambda: (0,) * a.ndim)
    out_shapes = [jax.ShapeDtypeStruct((m, w), F32) for w in _MAIN_SPLITS]
    out_shapes += [jax.ShapeDtypeStruct((SSM_HEADS, m), F32)]
    nw = norm_w.reshape(1, d)
    return pl.pallas_call(
        _inproj_kernel,
        in_specs=[full(x), full(nw), full(wh), full(wl), full(wdth), full(wdtl)],
        out_specs=[pl.BlockSpec(s.shape, lambda: (0, 0)) for s in out_shapes],
        out_shape=out_shapes,
        compiler_params=pltpu.CompilerParams(vmem_limit_bytes=VMEM_LIMIT),
        name="in_proj_sample",
    )(x, nw, wh, wl, wdth, wdtl)


_PROMPT_NN = (SSM_WIDTH, CONV_DIM, ATT_WIDTH, X_WIDTH, X_WIDTH, ATT_WIDTH)


def _inproj_prompt_kernel(x_ref, nw_ref, wn_ref, wt_ref, z_ref, xbc_ref, ga_ref, qx_ref, gx_ref, kb_ref,
                          km_ref, qT_ref, kT_ref, vT_ref, vTb_ref, dtT_ref):
    xh = _rms(x_ref[...], nw_ref[...]).astype(BF16)
    lo = 0
    for o_ref, width in zip((z_ref, xbc_ref, ga_ref, qx_ref, gx_ref), _PROMPT_NN[:-1]):
        o_ref[...] = _dot(xh, wn_ref[:, lo:lo + width])
        lo += width
    k = _dot(xh, wn_ref[:, lo:lo + ATT_WIDTH])
    kb_ref[0] = k.astype(BF16)
    km_ref[0] = jnp.mean(k, axis=0, keepdims=True)
    qT_ref[0] = _dot_nt(wt_ref[0:ATT_WIDTH, :], xh)
    kT_ref[...] = _dot_nt(wt_ref[ATT_WIDTH:2 * ATT_WIDTH, :], xh)
    vT = _dot_nt(wt_ref[2 * ATT_WIDTH:3 * ATT_WIDTH, :], xh)
    vT_ref[...] = vT
    vTb_ref[0] = vT.astype(BF16)
    dtT_ref[...] = _dot_nt(wt_ref[3 * ATT_WIDTH:, :], xh)


def _in_proj_prompt(x, norm_w, w_in):
    m, d = x.shape
    tm = MOBA_BLOCK
    nb = m // tm
    c_z, c_dt = 0, SSM_WIDTH + CONV_DIM
    c_q = c_dt + SSM_HEADS
    c_k, c_ga = c_q + ATT_WIDTH, c_q + 3 * ATT_WIDTH
    cols = lambda lo, w: w_in[:, lo:lo + w]
    wn = jnp.concatenate([cols(c_z, SSM_WIDTH + CONV_DIM), cols(c_ga, ATT_WIDTH + 2 * X_WIDTH),
                          cols(c_k, ATT_WIDTH)], axis=1).astype(BF16)
    wt = jnp.concatenate([cols(c_q, 3 * ATT_WIDTH), cols(c_dt, SSM_HEADS)], axis=1).T.astype(BF16)
    nw = norm_w.reshape(1, d)
    row = lambda w: pl.BlockSpec((tm, w), lambda i: (i, 0))
    col = lambda h: pl.BlockSpec((h, tm), lambda i: (0, i))
    full = lambda a: pl.BlockSpec(a.shape, lambda i: (0,) * a.ndim)
    tile3 = lambda a, b: pl.BlockSpec((1, a, b), lambda i: (i, 0, 0))
    out_specs = [row(SSM_WIDTH), row(CONV_DIM), row(ATT_WIDTH), row(X_WIDTH), row(X_WIDTH),
                 tile3(tm, ATT_WIDTH), tile3(1, ATT_WIDTH), tile3(ATT_WIDTH, tm),
                 col(ATT_WIDTH), col(ATT_WIDTH), tile3(ATT_WIDTH, tm), col(SSM_HEADS)]
    sds = jax.ShapeDtypeStruct
    out_shape = [sds((m, SSM_WIDTH), F32), sds((m, CONV_DIM), F32), sds((m, ATT_WIDTH), F32),
                 sds((m, X_WIDTH), F32), sds((m, X_WIDTH), F32),
                 sds((nb, tm, ATT_WIDTH), BF16), sds((nb, 1, ATT_WIDTH), F32), sds((nb, ATT_WIDTH, tm), F32),
                 sds((ATT_WIDTH, m), F32), sds((ATT_WIDTH, m), F32), sds((nb, ATT_WIDTH, tm), BF16),
                 sds((SSM_HEADS, m), F32)]
    return pl.pallas_call(
        _inproj_prompt_kernel,
        grid=(nb,),
        in_specs=[row(d), full(nw), full(wn), full(wt)],
        out_specs=out_specs,
        out_shape=out_shape,
        compiler_params=_cparams(("parallel",)),
        name="in_proj_prompt",
    )(x, nw, wn, wt)


def _ssd_kernel(xbc_ref, dtT_ref, z_ref, convw_ref, convb_ref, dtb_ref, alog_ref, dskip_ref, nw_ref,
                e_ref, y_ref, hT_ref, tail_sc, h_sc):
    c = pl.program_id(0)
    L = SSM_CHUNK

    @pl.when(c == 0)
    def _():
        tail_sc[...] = jnp.zeros_like(tail_sc)
        h_sc[...] = jnp.zeros_like(h_sc)

    x = xbc_ref[...]
    tail = tail_sc[...]
    w = convw_ref[...]
    row8 = lax.broadcasted_iota(jnp.int32, (SUBLANES, CONV_DIM), 0)
    acc = x * w[CONV_W - 1:CONV_W, :] + convb_ref[...]
    for k in range(1, CONV_W):
        rolled = pltpu.roll(x, k, 0)
        fix = pltpu.roll(tail, k, 0)
        top = jnp.where(row8 < k, fix, rolled[0:SUBLANES])
        shifted = jnp.concatenate([top, rolled[SUBLANES:]], axis=0)
        acc = acc + shifted * w[CONV_W - 1 - k:CONV_W - k, :]
    tail_sc[...] = x[L - SUBLANES:L]
    u = _silu(acc)
    xs = u[:, :SSM_WIDTH]
    bm = u[:, SSM_WIDTH:SSM_WIDTH + SSM_GROUPS * SSM_STATE]
    cm = u[:, SSM_WIDTH + SSM_GROUPS * SSM_STATE:]

    dtT = _softplus(dtT_ref[...] + dtb_ref[...])
    aT = dtT * (-jnp.exp(alog_ref[...]))
    r_i = lax.broadcasted_iota(jnp.int32, (L, L), 0)
    c_i = lax.broadcasted_iota(jnp.int32, (L, L), 1)
    upper = (r_i <= c_i).astype(BF16)
    acumT = _dot_f32_exactrhs(aT, upper)
    a_last = acumT[:, L - 1:L]
    to_endT = jnp.exp(a_last - acumT) * dtT
    eacT = jnp.exp(acumT)
    stack = jnp.concatenate(
        [acumT, dtT, to_endT, eacT, jnp.zeros((L - 4 * SSM_HEADS, L), F32)], axis=0)
    st = stack.T
    e = e_ref[...]
    dt_e = _dot_f32_exactrhs(st[:, SSM_HEADS:2 * SSM_HEADS], e, 2)
    toend_e = _dot_f32_exactrhs(st[:, 2 * SSM_HEADS:3 * SSM_HEADS], e, 2)
    eac_e = _dot_f32_exactrhs(st[:, 3 * SSM_HEADS:4 * SSM_HEADS], e, 2)
    cd_e = eac_e[L - 1:L, :]

    xd = (xs * dt_e).astype(BF16)
    xw = (xs * toend_e).astype(BF16)
    causal = r_i >= c_i
    ys = []
    yoffs = []
    for g in range(SSM_GROUPS):
        bg = bm[:, g * SSM_STATE:(g + 1) * SSM_STATE].astype(BF16)
        cg = cm[:, g * SSM_STATE:(g + 1) * SSM_STATE].astype(BF16)
        cb = jnp.where(causal, _dot_nt(cg, bg), 0.0)
        for hh in range(SSM_HG):
            h = g * SSM_HG + hh
            seg = st[:, h:h + 1] - acumT[h:h + 1, :]
            mix = (cb * jnp.exp(jnp.minimum(seg, 0.0))).astype(BF16)
            ys.append(_dot(mix, xd[:, h * SSM_HEAD_DIM:(h + 1) * SSM_HEAD_DIM]))
        hprev = h_sc[g]
        yoffs.append(_dot(cg, hprev.astype(BF16)))
        s_new = _dot_tn(bg, xw[:, g * SSM_GW:(g + 1) * SSM_GW])
        h_sc[g] = hprev * cd_e[:, g * SSM_GW:(g + 1) * SSM_GW] + s_new

    y = jnp.concatenate(ys, axis=1) + jnp.concatenate(yoffs, axis=1) * eac_e + dskip_ref[...] * xs
    gt = y * _silu(z_ref[...])
    outs = []
    for g in range(SSM_GROUPS):
        gg = gt[:, g * SSM_GW:(g + 1) * SSM_GW]
        outs.append(gg * lax.rsqrt(jnp.mean(gg * gg, axis=-1, keepdims=True) + EPS))
    y_ref[...] = jnp.concatenate(outs, axis=1) * nw_ref[...]

    @pl.when(c == pl.num_programs(0) - 1)
    def _():
        hT_ref[...] = h_sc[...]


def _head_expand():
    return jnp.asarray(np.repeat(np.eye(SSM_HEADS, dtype=np.float32), SSM_HEAD_DIM, axis=1), BF16)


def _ssd_prompt(xbc, dtT, z, conv_w, conv_b, dt_bias, a_log, d_skip, ssm_norm_w):
    m = xbc.shape[0]
    L = SSM_CHUNK
    row = lambda w: pl.BlockSpec((L, w), lambda i: (i, 0))
    full = lambda a: pl.BlockSpec(a.shape, lambda i: (0,) * a.ndim)
    convb = conv_b.reshape(1, CONV_DIM)
    dtb = dt_bias.reshape(SSM_HEADS, 1)
    alog = a_log.reshape(SSM_HEADS, 1)
    dskip = jnp.repeat(d_skip, SSM_HEAD_DIM).reshape(1, SSM_WIDTH)
    nw = ssm_norm_w.reshape(1, SSM_WIDTH)
    e = _head_expand()
    y, hT = pl.pallas_call(
        _ssd_kernel,
        grid=(m // L,),
        in_specs=[row(CONV_DIM), pl.BlockSpec((SSM_HEADS, L), lambda i: (0, i)), row(SSM_WIDTH),
                  full(conv_w), full(convb), full(dtb), full(alog), full(dskip), full(nw), full(e)],
        out_specs=[row(SSM_WIDTH), pl.BlockSpec((SSM_GROUPS, SSM_STATE, SSM_GW), lambda i: (0, 0, 0))],
        out_shape=[jax.ShapeDtypeStruct((m, SSM_WIDTH), F32),
                   jax.ShapeDtypeStruct((SSM_GROUPS, SSM_STATE, SSM_GW), F32)],
        scratch_shapes=[pltpu.VMEM((SUBLANES, CONV_DIM), F32),
                        pltpu.VMEM((SSM_GROUPS, SSM_STATE, SSM_GW), F32)],
        compiler_params=_cparams(("arbitrary",)),
        name="ssd_prompt",
    )(xbc, dtT, z, conv_w, convb, dtb, alog, dskip, nw, e)
    state = hT.reshape(SSM_GROUPS, SSM_STATE, SSM_HG, SSM_HEAD_DIM).transpose(0, 2, 3, 1)
    return y, state.reshape(SSM_HEADS, SSM_HEAD_DIM, SSM_STATE)


def _bucket_np(dist):
    n = np.maximum(dist, 0)
    exact = RPE_BUCKETS // 2
    nf = np.maximum(n, 1).astype(np.float32)
    scaled = (np.log(nf / np.float32(exact)) / np.float32(math.log(RPE_MAX_DIST / exact))
              * np.float32(RPE_BUCKETS - exact))
    large = exact + scaled.astype(np.int32)
    return np.where(n < exact, n, np.minimum(large, RPE_BUCKETS - 1)).astype(np.int32)


def _bias_kernel(rb_ref, map_ref, out_ref, *, scale):
    h = pl.program_id(0)
    bmap = map_ref[...]
    acc = jnp.full(bmap.shape, -2.0 * BIG, F32)
    for b in range(RPE_BUCKETS):
        acc = jnp.where(bmap == b, rb_ref[b * ATT_HEADS + h] * scale, acc)
    out_ref[0] = acc


def _bias_table(rel_bias, bucket_map, scale=1.0):
    bmap = jnp.asarray(bucket_map, jnp.int32)
    nd = bmap.ndim
    return pl.pallas_call(
        functools.partial(_bias_kernel, scale=scale),
        grid=(ATT_HEADS,),
        in_specs=[pl.BlockSpec(memory_space=pltpu.SMEM),
                  pl.BlockSpec(bmap.shape, lambda h: (0,) * nd)],
        out_specs=pl.BlockSpec((1,) + bmap.shape, lambda h: (h,) + (0,) * nd),
        out_shape=jax.ShapeDtypeStruct((ATT_HEADS,) + bmap.shape, F32),
        compiler_params=_cparams(("parallel",)),
        name="rel_bias_table",
    )(rel_bias.reshape(-1), bmap)


def _topk_hits(s, blk, axis):
    picks = []
    for _ in range(MOBA_TOPK):
        mx = jnp.max(s, axis=axis, keepdims=True)
        idx = jnp.min(jnp.where(s == mx, blk, float(2 * LANES)), axis=axis, keepdims=True)
        picks.append((idx, jnp.where(mx > -jnp.inf, 1.0, 0.0)))
        s = jnp.where(blk == idx, -jnp.inf, s)
    return picks


_SPLIT = 4
_ONES_ROWS = 16


def _moba_kernel(cfar_ref, qT_ref, kb_ref, vT_ref, km_ref, d0_ref, d1_ref, o_ref, pen_sc, acc_sc, m_sc):
    hp = pl.program_id(0)
    i = pl.program_id(1)
    W = MOBA_BLOCK
    D = ATT_HEAD_DIM
    nb = km_ref.shape[0]
    blk = lax.broadcasted_iota(jnp.int32, (nb, W), 0).astype(F32)
    i_f = i.astype(F32)
    near = jnp.maximum(i - 1, 0)
    q_augs, cfars = [], []
    acc_sc[...] = jnp.zeros_like(acc_sc)
    for hh in range(2):
        rows = slice(hh * D, (hh + 1) * D)
        cfar2 = cfar_ref[hp * 2 + hh] * LOG2E
        qT = qT_ref[0, rows, :]
        sT = _dot_hi(km_ref[:, rows], qT)
        sT = jnp.where(blk < i_f, sT, -jnp.inf)
        hit = jnp.zeros((nb, W), F32)
        for idx, valid in _topk_hits(sT, blk, 0):
            hit = jnp.maximum(hit, jnp.where(blk == idx, valid, 0.0))
        pen_sc[hh] = jnp.where(hit > 0.0, cfar2, -2.0 * BIG)

        qs = (qT * (D ** -0.5 * LOG2E)).astype(BF16)
        zero = jnp.zeros((D, W), BF16)
        q_augs.append(jnp.concatenate([qs, zero] if hh == 0 else [zero, qs], axis=0))
        cfars.append(cfar2)

    m_sc[...] = jnp.full(m_sc.shape, -BIG, F32)
    ones = jnp.ones((_ONES_ROWS, W), BF16)

    def group(chains):
        ss = []
        for hh, a, j, bias_ref, _ in chains:
            s = _dot(kb_ref[j], q_augs[hh])
            ss.append(s if bias_ref is None else s + bias_ref[hh])
        ps = []
        for (hh, a, j, _, pen), s in zip(chains, ss):
            m = m_sc[hh, a, 0:1, :]
            m_new = jnp.maximum(m, jnp.max(s, axis=0, keepdims=True) + pen)
            m_sc[hh, a, 0:1, :] = m_new
            ps.append((jnp.exp2(m - m_new), jnp.exp2(s - (m_new - pen)).astype(BF16)))
        for (hh, a, j, _, _), (alpha, p) in zip(chains, ps):
            v1 = jnp.concatenate([vT_ref[j, hh * D:(hh + 1) * D, :], ones], axis=0)
            acc_sc[hh, a] = alpha * acc_sc[hh, a] + _dot(v1, p)

    zero_row = jnp.zeros((1, W), F32)
    group([(hh, 0, i, d0_ref, zero_row) for hh in range(2)]
          + [(hh, 1, near, d1_ref, pen_sc[hh, pl.ds(near, 1), :] - cfars[hh]) for hh in range(2)])

    def far_body(jj, carry):
        chains = []
        for a in range(_SPLIT):
            j = _SPLIT * jj + a
            jc = jnp.minimum(j, nb - 1)
            for hh in range(2):
                pen = jnp.where(j < near, pen_sc[hh, pl.ds(jc, 1), :], -2.0 * BIG)
                chains.append((hh, a, jc, None, pen))
        group(chains)
        return carry

    lax.fori_loop(0, (near + _SPLIT - 1) // _SPLIT, far_body, 0)

    outs = []
    for hh in range(2):
        m = m_sc[hh, 0, 0:1, :]
        for a in range(1, _SPLIT):
            m = jnp.maximum(m, m_sc[hh, a, 0:1, :])
        acc = jnp.zeros((D + _ONES_ROWS, W), F32)
        for a in range(_SPLIT):
            acc = acc + jnp.exp2(m_sc[hh, a, 0:1, :] - m) * acc_sc[hh, a]
        outs.append(acc[:D] / acc[D:D + 1])
    o_ref[...] = jnp.concatenate(outs, axis=0)


def _moba_prompt(qT3, kb3, vT3, kmean, d0T, d1T, rel_bias):
    nb = qT3.shape[0]
    W = MOBA_BLOCK
    assert nb - 1 >= MOBA_TOPK and nb <= LANES
    cfar = rel_bias[RPE_BUCKETS - 1]
    return pl.pallas_call(
        _moba_kernel,
        grid=(ATT_HEADS // 2, nb),
        in_specs=[pl.BlockSpec(memory_space=pltpu.SMEM),
                  pl.BlockSpec((1, LANES, W), lambda hp, i: (i, hp, 0)),
                  pl.BlockSpec((nb, W, LANES), lambda hp, i: (0, 0, hp)),
                  pl.BlockSpec((nb, LANES, W), lambda hp, i: (0, hp, 0)),
                  pl.BlockSpec((nb, LANES), lambda hp, i: (0, hp)),
                  pl.BlockSpec((2, W, W), lambda hp, i: (hp, 0, 0)),
                  pl.BlockSpec((2, W, W), lambda hp, i: (hp, 0, 0))],
        out_specs=pl.BlockSpec((LANES, W), lambda hp, i: (hp, i)),
        out_shape=jax.ShapeDtypeStruct((ATT_WIDTH, nb * W), F32),
        scratch_shapes=[pltpu.VMEM((2, nb, W), F32),
                        pltpu.VMEM((2, _SPLIT, ATT_HEAD_DIM + _ONES_ROWS, W), F32),
                        pltpu.VMEM((2, _SPLIT, SUBLANES, W), F32)],
        compiler_params=_cparams(("parallel", "arbitrary")),
        name="moba_prompt",
    )(cfar, qT3, kb3, vT3, kmean, d0T, d1T)


def _memkv_kernel(mem_ref, g_ref, w_ref, k_ref, v_ref):
    xn = _rms(mem_ref[...], g_ref[...]).astype(BF16)
    kv = _dot(xn, w_ref[...])
    k_ref[...] = kv[:, :X_WIDTH]
    v_ref[...] = kv[:, X_WIDTH:]


def _memory_kv(mem, g, w):
    nm, d = mem.shape
    full = lambda a: pl.BlockSpec(a.shape, lambda: (0,) * a.ndim)
    g2 = g.reshape(1, d)
    wb = w.astype(BF16)
    return pl.pallas_call(
        _memkv_kernel,
        in_specs=[full(mem), full(g2), full(wb)],
        out_specs=[pl.BlockSpec((nm, X_WIDTH), lambda: (0, 0))] * 2,
        out_shape=[jax.ShapeDtypeStruct((nm, X_WIDTH), F32)] * 2,
        compiler_params=pltpu.CompilerParams(vmem_limit_bytes=VMEM_LIMIT),
        name="memory_kv",
    )(mem, g2, wb)


def _xattn_kernel(q_ref, gx_ref, mk_ref, mv_ref, o_ref):
    q = q_ref[...].reshape(-1, X_WIDTH)
    gx = gx_ref[...].reshape(-1, X_WIDTH)
    mk = mk_ref[...].reshape(-1, X_WIDTH).astype(BF16)
    mv = mv_ref[...].reshape(-1, X_WIDTH).astype(BF16)
    outs = []
    for h in range(X_HEADS):
        sl = slice(h * X_HEAD_DIM, (h + 1) * X_HEAD_DIM)
        s = _dot_nt((q[:, sl] * (X_HEAD_DIM ** -0.5)).astype(BF16), mk[:, sl])
        m = jnp.max(s, axis=1, keepdims=True)
        p = jnp.exp(s - m)
        l = jnp.sum(p, axis=1, keepdims=True)
        outs.append(_dot(p.astype(BF16), mv[:, sl]) / l)
    o_ref[...] = (jnp.concatenate(outs, axis=1) * _silu(gx)).reshape(o_ref.shape)


def _xattn_prompt(qx, gx, mk, mv, *, tq):
    m = qx.shape[0]
    row = pl.BlockSpec((tq, X_WIDTH), lambda i: (i, 0))
    full = pl.BlockSpec(mk.shape, lambda i: (0, 0))
    return pl.pallas_call(
        _xattn_kernel,
        grid=(m // tq,),
        in_specs=[row, row, full, full],
        out_specs=row,
        out_shape=jax.ShapeDtypeStruct((m, X_WIDTH), F32),
        compiler_params=_cparams(("parallel",)),
        name="xattn_prompt",
    )(qx, gx, mk, mv)


def _xattn_sample(qx, gx, mk, mv):
    b, t8, _ = qx.shape
    nm = mk.shape[1]
    qs = pl.BlockSpec((1, t8, X_WIDTH), lambda i: (i, 0, 0))
    ms = pl.BlockSpec((1, nm, X_WIDTH), lambda i: (i, 0, 0))
    return pl.pallas_call(
        _xattn_kernel,
        grid=(b,),
        in_specs=[qs, qs, ms, ms],
        out_specs=qs,
        out_shape=jax.ShapeDtypeStruct((b, t8, X_WIDTH), F32),
        compiler_params=_cparams(("parallel",)),
        name="xattn_sample",
    )(qx, gx, mk, mv)


def _outproj_kernel(x_ref, ys_ref, ya_ref, ga_ref, yx_ref, w_ref, fw_ref, o_ref):
    att = (ya_ref[...] * _silu(ga_ref[...])).astype(BF16)
    h = x_ref[...] + _dot(ys_ref[...].astype(BF16), w_ref[0:SSM_WIDTH, :])
    h = h + _dot(att, w_ref[SSM_WIDTH:SSM_WIDTH + ATT_WIDTH, :])
    h = h + _dot(yx_ref[...].astype(BF16), w_ref[SSM_WIDTH + ATT_WIDTH:, :])
    o_ref[...] = _rms(h, fw_ref[...])


def _out_proj(x, y_ssd, y_att, ga, y_x, w_out, final_norm_w, *, tm):
    m, d = x.shape
    row = lambda w: pl.BlockSpec((tm, w), lambda i: (i, 0))
    full = lambda a: pl.BlockSpec(a.shape, lambda i: (0,) * a.ndim)
    wb = w_out.astype(BF16)
    fw = final_norm_w.reshape(1, d)
    return pl.pallas_call(
        _outproj_kernel,
        grid=(m // tm,),
        in_specs=[row(d), row(SSM_WIDTH), row(ATT_WIDTH), row(ATT_WIDTH), row(X_WIDTH), full(wb), full(fw)],
        out_specs=row(d),
        out_shape=jax.ShapeDtypeStruct((m, d), F32),
        compiler_params=_cparams(("parallel",)),
        name="out_proj",
    )(x, y_ssd, y_att, ga, y_x, wb, fw)


def _ssd_sample_kernel(x_ref, c_ref, dt_ref, z_ref, h0_ref, w_ref, b_ref, dtb_ref, alog_ref, dsk_ref,
                       nw_ref, sel_ref, y_ref, h_ref, y_sc, *, T):
    P = SSM_HEAD_DIM
    N = SSM_STATE
    cr = c_ref[0]
    xr = x_ref[0]
    ext = lambda i: cr[i:i + 1, :] if i < CONV_W - 1 else xr[i - (CONV_W - 1):i - (CONV_W - 2), :]
    w = w_ref[...]
    rows = []
    for t in range(T):
        u = b_ref[...]
        for j in range(CONV_W):
            u = u + w[j:j + 1, :] * ext(t + j)
        rows.append(u)
    u = _silu(jnp.concatenate(rows + [jnp.zeros((SUBLANES - T, CONV_DIM), F32)], axis=0))
    xs = u[:, :SSM_WIDTH]
    dt = _softplus(dt_ref[0] + dtb_ref[...])
    dec = jnp.exp(dt * (-jnp.exp(alog_ref[...])))
    xdt = xs[0:T] * dt
    sel = sel_ref[...]
    bc = [_dot_tn_exactrhs(u[:, SSM_WIDTH + k * N:SSM_WIDTH + (k + 1) * N], sel)
          for k in range(2 * SSM_GROUPS)]
    for pair in range(SSM_HEADS // 2):
        g = (2 * pair) // SSM_HG
        lanes = slice(pair * 2 * P, (pair + 1) * 2 * P)
        hT = jnp.concatenate([h0_ref[0, 2 * pair], h0_ref[0, 2 * pair + 1]], axis=0).T
        for t in range(T):
            tl = slice(t * N, (t + 1) * N)
            hT = hT * dec[t:t + 1, lanes] + bc[g][:, tl] * xdt[t:t + 1, lanes]
            y_sc[t:t + 1, lanes] = jnp.sum(hT * bc[SSM_GROUPS + g][:, tl], axis=0, keepdims=True)
        h2 = hT.T
        h_ref[0, 2 * pair] = h2[:P]
        h_ref[0, 2 * pair + 1] = h2[P:]
    y = y_sc[0:T, :] + dsk_ref[...] * xs[0:T]
    gt = y * _silu(z_ref[0])
    outs = []
    for g in range(SSM_GROUPS):
        gg = gt[:, g * SSM_GW:(g + 1) * SSM_GW]
        outs.append(gg * lax.rsqrt(jnp.mean(gg * gg, axis=-1, keepdims=True) + EPS))
    y_ref[0] = jnp.concatenate(outs, axis=1) * nw_ref[...]


def _dot_tn_exactrhs(a, b_bf16):
    out = None
    for part in _split3(a):
        term = _dot_tn(part, b_bf16)
        out = term if out is None else out + term
    return out


def _ssd_sample(xbc, state_conv, dt_raw, z, state_ssm, conv_w, conv_b, dt_bias, a_log, d_skip, ssm_norm_w):
    b, T, _ = xbc.shape
    per = lambda a: pl.BlockSpec((1,) + a.shape[1:], lambda i: (i,) + (0,) * (a.ndim - 1))
    full = lambda a: pl.BlockSpec(a.shape, lambda i: (0,) * a.ndim)
    per_head = lambda v: jnp.repeat(v, SSM_HEAD_DIM, axis=-1)
    dt_e = per_head(dt_raw)
    br = conv_b.reshape(1, CONV_DIM)
    dtb = per_head(dt_bias).reshape(1, SSM_WIDTH)
    alog = per_head(a_log).reshape(1, SSM_WIDTH)
    dsk = per_head(d_skip).reshape(1, SSM_WIDTH)
    nw = ssm_norm_w.reshape(1, SSM_WIDTH)
    sel = np.zeros((SUBLANES, T * SSM_STATE), np.float32)
    for t in range(T):
        sel[t, t * SSM_STATE:(t + 1) * SSM_STATE] = 1.0
    sel = jnp.asarray(sel, BF16)
    args = (xbc, state_conv, dt_e, z, state_ssm, conv_w, br, dtb, alog, dsk, nw, sel)
    in_specs = [per(a) for a in args[:5]] + [full(a) for a in args[5:]]
    return pl.pallas_call(
        functools.partial(_ssd_sample_kernel, T=T),
        grid=(b,),
        in_specs=in_specs,
        out_specs=[pl.BlockSpec((1, T, SSM_WIDTH), lambda i: (i, 0, 0)), per(state_ssm)],
        out_shape=[jax.ShapeDtypeStruct((b, T, SSM_WIDTH), F32),
                   jax.ShapeDtypeStruct(state_ssm.shape, F32)],
        scratch_shapes=[pltpu.VMEM((SUBLANES, SSM_WIDTH), F32)],
        compiler_params=_cparams(("parallel",)),
        name="ssd_sample",
    )(*args)


_KM_PAGES = 32


def _kmean_pages_kernel(pt_ref, *refs):
    o_ref = refs[-1]
    si = pl.program_id(1)
    nb = o_ref.shape[2]

    @pl.when(si == 0)
    def _():
        o_ref[...] = jnp.zeros_like(o_ref)

    lane = lax.broadcasted_iota(jnp.int32, (ATT_WIDTH, nb), 1)
    acc = o_ref[0]
    for b in range(_KM_PAGES // PAGES_PER_BLOCK):
        tok = refs[b * PAGES_PER_BLOCK][0].reshape(ATT_WIDTH, PAGE_SIZE)
        for pg in range(1, PAGES_PER_BLOCK):
            tok = tok + refs[b * PAGES_PER_BLOCK + pg][0].reshape(ATT_WIDTH, PAGE_SIZE)
        mean = jnp.sum(tok, axis=1, keepdims=True) * (1.0 / MOBA_BLOCK)
        acc = jnp.where(lane == si * (_KM_PAGES // PAGES_PER_BLOCK) + b, mean, acc)
    o_ref[0] = acc


def _kmean_sample(pool_kt, page_table):
    b, n_pages = page_table.shape
    nb = n_pages // PAGES_PER_BLOCK
    steps = n_pages // _KM_PAGES
    page_spec = lambda r: pl.BlockSpec(
        (1, ATT_HEADS, ATT_HEAD_DIM, PAGE_SIZE),
        lambda bi, si, pt: (pt[bi * n_pages + si * _KM_PAGES + r], 0, 0, 0))
    grid_spec = pltpu.PrefetchScalarGridSpec(
        num_scalar_prefetch=1,
        grid=(b, steps),
        in_specs=[page_spec(r) for r in range(_KM_PAGES)],
        out_specs=pl.BlockSpec((1, ATT_WIDTH, nb), lambda bi, si, pt: (bi, 0, 0)),
    )
    return pl.pallas_call(
        _kmean_pages_kernel,
        grid_spec=grid_spec,
        out_shape=jax.ShapeDtypeStruct((b, ATT_WIDTH, nb), F32),
        compiler_params=_cparams(("parallel", "arbitrary")),
        name="moba_kmean_pages",
    )(page_table.reshape(-1), *([pool_kt] * _KM_PAGES))


def _select_sample_kernel(q_ref, km_ref, o_ref):
    q = q_ref[0]
    nb = km_ref.shape[2]
    t8 = q.shape[0]
    blk = lax.broadcasted_iota(jnp.int32, (t8, nb), 1).astype(F32)
    lane = lax.broadcasted_iota(jnp.int32, (t8, LANES), 1)
    for h in range(ATT_HEADS):
        rows = slice(h * ATT_HEAD_DIM, (h + 1) * ATT_HEAD_DIM)
        s = _dot_hi(q[:, rows], km_ref[0, rows, :])
        tile = jnp.zeros((t8, LANES), F32)
        for r, (idx, _) in enumerate(_topk_hits(s, blk, 1)):
            tile = jnp.where(lane == r, idx, tile)
        o_ref[0, h] = tile.astype(jnp.int32)


def _select_sample(q8, kmeanT):
    b, t8, _ = q8.shape
    nb = kmeanT.shape[2]
    assert nb >= MOBA_TOPK
    return pl.pallas_call(
        _select_sample_kernel,
        grid=(b,),
        in_specs=[pl.BlockSpec((1, t8, ATT_WIDTH), lambda i: (i, 0, 0)),
                  pl.BlockSpec((1, ATT_WIDTH, nb), lambda i: (i, 0, 0))],
        out_specs=pl.BlockSpec((1, ATT_HEADS, t8, LANES), lambda i: (i, 0, 0, 0)),
        out_shape=jax.ShapeDtypeStruct((b, ATT_HEADS, t8, LANES), jnp.int32),
        compiler_params=_cparams(("parallel",)),
        name="moba_select_sample",
    )(q8, kmeanT)


_N_SLAB = MOBA_TOPK * PAGES_PER_BLOCK


def _attend_sample_kernel(sel_ref, pt_ref, rb_ref, q_ref, kn_ref, vn_ref, near_ref, pk_ref, pv_ref,
                          o_ref, kbuf, vbuf, sem, *, T, n_pages):
    bi = pl.program_id(0)
    t = pl.program_id(1)
    step = bi * T + t
    nsteps = pl.num_programs(0) * T
    nb = n_pages // PAGES_PER_BLOCK
    D = ATT_HEAD_DIM

    def copies(s, slot):
        b_s = s // T
        out = []
        for h in range(ATT_HEADS):
            for r in range(MOBA_TOPK):
                blk = sel_ref[(s * ATT_HEADS + h) * MOBA_TOPK + r]
                for pg in range(PAGES_PER_BLOCK):
                    phys = pt_ref[b_s * n_pages + blk * PAGES_PER_BLOCK + pg]
                    i = r * PAGES_PER_BLOCK + pg
                    out.append(pltpu.make_async_copy(pk_ref.at[phys, h], kbuf.at[slot, h, i], sem.at[0, slot]))
                    out.append(pltpu.make_async_copy(pv_ref.at[phys, h], vbuf.at[slot, h, i], sem.at[1, slot]))
        return out

    @pl.when(step == 0)
    def _():
        for cp in copies(step, 0):
            cp.start()

    @pl.when(step + 1 < nsteps)
    def _():
        for cp in copies(step + 1, (step + 1) % 2):
            cp.start()

    slot = step % 2
    for cp in copies(step, slot):
        cp.wait()

    lane_t = lax.broadcasted_iota(jnp.int32, (1, T), 1)
    for h in range(ATT_HEADS):
        rows = slice(h * D, (h + 1) * D)
        qc = q_ref[0, rows, :]
        cfar = rb_ref[(RPE_BUCKETS - 1) * ATT_HEADS + h]
        s_rows = []
        for r in range(MOBA_TOPK):
            blk = sel_ref[(step * ATT_HEADS + h) * MOBA_TOPK + r]
            for pg in range(PAGES_PER_BLOCK):
                s = jnp.sum(kbuf[slot, h, r * PAGES_PER_BLOCK + pg] * qc, axis=0, keepdims=True)
                s_rows.append(s + jnp.where(blk == nb - 1, near_ref[h, 0, pg:pg + 1, :], cfar))
        s_own = jnp.sum(kn_ref[0, rows, :] * qc, axis=0, keepdims=True)
        b_own = jnp.zeros((1, T), F32)
        for tp in range(T):
            b_own = jnp.where(lane_t == tp, rb_ref[jnp.maximum(t - tp, 0) * ATT_HEADS + h], b_own)
        s_own = jnp.where(lane_t <= t, s_own + b_own, NEG)
        m_row = s_rows[0]
        for s in s_rows[1:]:
            m_row = jnp.maximum(m_row, s)
        m = jnp.maximum(jnp.max(m_row, axis=1, keepdims=True), jnp.max(s_own, axis=1, keepdims=True))
        p_own = jnp.exp(s_own - m)
        l_row = jnp.zeros((1, PAGE_SIZE), F32)
        acc = jnp.zeros((D, PAGE_SIZE), F32)
        for n, s in enumerate(s_rows):
            p = jnp.exp(s - m)
            l_row = l_row + p
            acc = acc + vbuf[slot, h, n] * p
        l = jnp.sum(l_row, axis=1, keepdims=True) + jnp.sum(p_own, axis=1, keepdims=True)
        o = jnp.sum(acc, axis=1, keepdims=True)
        vn = vn_ref[0, rows, :]
        for tp in range(T):
            o = o + vn[:, tp:tp + 1] * p_own[:, tp:tp + 1]
        o_ref[0, rows, :] = o / l


def _attend_sample(sel, page_table, rel_bias, q_col, k_newT, v_newT, near, pool_kt, pool_vt):
    b, _, T = k_newT.shape
    n_pages = page_table.shape[1]
    assert T <= RPE_BUCKETS // 2
    slab = (2, ATT_HEADS, _N_SLAB, ATT_HEAD_DIM, PAGE_SIZE)
    grid_spec = pltpu.PrefetchScalarGridSpec(
        num_scalar_prefetch=3,
        grid=(b, T),
        in_specs=[pl.BlockSpec((1, ATT_WIDTH, 1), lambda bi, t, *_: (bi * T + t, 0, 0)),
                  pl.BlockSpec((1, ATT_WIDTH, T), lambda bi, t, *_: (bi, 0, 0)),
                  pl.BlockSpec((1, ATT_WIDTH, T), lambda bi, t, *_: (bi, 0, 0)),
                  pl.BlockSpec((ATT_HEADS, 1, PAGES_PER_BLOCK, PAGE_SIZE), lambda bi, t, *_: (0, t, 0, 0)),
                  pl.BlockSpec(memory_space=pl.ANY),
                  pl.BlockSpec(memory_space=pl.ANY)],
        out_specs=pl.BlockSpec((1, ATT_WIDTH, 1), lambda bi, t, *_: (bi * T + t, 0, 0)),
        scratch_shapes=[pltpu.VMEM(slab, F32), pltpu.VMEM(slab, F32), pltpu.SemaphoreType.DMA((2, 2))],
    )
    return pl.pallas_call(
        functools.partial(_attend_sample_kernel, T=T, n_pages=n_pages),
        grid_spec=grid_spec,
        out_shape=jax.ShapeDtypeStruct((b * T, ATT_WIDTH, 1), F32),
        compiler_params=_cparams(("arbitrary", "arbitrary")),
        name="moba_attend_sample",
    )(sel, page_table.reshape(-1), rel_bias.reshape(-1), q_col, k_newT, v_newT, near, pool_kt, pool_vt)


def _prompt_layer(x, mem, norm_w, w_in, conv_w, conv_b, dt_bias, a_log, d_skip, ssm_norm_w,
                  mem_norm_w, w_mem_kv, w_out, rel_bias, final_norm_w):
    s = x.shape[0]
    W = MOBA_BLOCK
    z, xbc, ga, qx, gx, kb3, km3, qT3, kT, vT, vT3, dtT = _in_proj_prompt(x, norm_w, w_in)
    y_ssd, state = _ssd_prompt(xbc, dtT, z, conv_w, conv_b, dt_bias, a_log, d_skip, ssm_norm_w)
    ki = np.arange(W)[:, None]
    qi = np.arange(W)[None, :]
    own_map = np.where(ki <= qi, _bucket_np(qi - ki), -1)
    d01 = _bias_table(rel_bias, np.stack([own_map, _bucket_np(W + qi - ki)]), LOG2E)
    y_attT = _moba_prompt(qT3, kb3, vT3, km3.reshape(-1, ATT_WIDTH), d01[:, 0], d01[:, 1], rel_bias)
    mk, mv = _memory_kv(mem, mem_norm_w, w_mem_kv)
    y_x = _xattn_prompt(qx, gx, mk, mv, tq=512)
    y = _out_proj(x, y_ssd, y_attT.T, ga, y_x, w_out, final_norm_w, tm=512)
    conv_state = xbc[s - (CONV_W - 1):]
    heads = lambda aT: aT.reshape(ATT_HEADS, ATT_HEAD_DIM, s).transpose(2, 0, 1)
    return y, heads(kT), heads(vT), mk, mv, state, conv_state


def _sample_layer(x, cache_k, cache_v, page_table, mem_k, mem_v, state_ssm, state_conv, norm_w, w_in,
                  conv_w, conv_b, dt_bias, a_log, d_skip, ssm_norm_w, w_out, rel_bias, final_norm_w):
    b, T, d = x.shape
    n_pages = page_table.shape[1]
    assert (n_pages * PAGE_SIZE) % MOBA_BLOCK == 0 and n_pages % _KM_PAGES == 0
    assert CONV_W - 1 <= T <= SUBLANES
    xf = x.reshape(b * T, d)
    z, xbc, q, k, v, ga, qx, gx, dtT = _in_proj_sample(xf, norm_w, w_in)
    per = lambda a: a.reshape(b, T, a.shape[-1])
    y_ssd, h_new = _ssd_sample(per(xbc), state_conv, per(dtT.T), per(z), state_ssm,
                               conv_w, conv_b, dt_bias, a_log, d_skip, ssm_norm_w)
    pad8 = lambda a: jnp.pad(per(a), ((0, 0), (0, SUBLANES - T), (0, 0)))
    pool_kt = cache_k.transpose(0, 2, 3, 1)
    pool_vt = cache_v.transpose(0, 2, 3, 1)
    kmeanT = _kmean_sample(pool_kt, page_table)
    sel = _select_sample(pad8(q), kmeanT)[:, :, :T, :MOBA_TOPK]
    sel = sel.transpose(0, 2, 1, 3).reshape(-1)
    kk = (np.arange(PAGES_PER_BLOCK)[:, None] * PAGE_SIZE + np.arange(PAGE_SIZE)[None, :])[None]
    tt = np.arange(T)[:, None, None]
    near = _bias_table(rel_bias, _bucket_np(MOBA_BLOCK + tt - kk))
    q_col = (q * (ATT_HEAD_DIM ** -0.5)).reshape(b * T, ATT_WIDTH, 1)
    y_att = _attend_sample(sel, page_table, rel_bias, q_col, per(k).transpose(0, 2, 1),
                           per(v).transpose(0, 2, 1), near, pool_kt, pool_vt)
    y_att = y_att.reshape(b * T, ATT_WIDTH)
    y_x = _xattn_sample(pad8(qx), pad8(gx), mem_k.reshape(b, -1, X_WIDTH), mem_v.reshape(b, -1, X_WIDTH))
    y_x = y_x[:, :T].reshape(b * T, X_WIDTH)
    y = _out_proj(xf, y_ssd.reshape(b * T, SSM_WIDTH), y_att, ga, y_x, w_out, final_norm_w, tm=b * T)
    conv_state = per(xbc)[:, T - (CONV_W - 1):]
    return y.reshape(b, T, d), per(k), per(v), h_new, conv_state


def kernel(x_prompt, x_sample, mem_prompt, cache_k, cache_v, page_table, cache_mem_k, cache_mem_v,
           state_ssm, state_conv, norm_w, w_in, conv_w, conv_b, dt_bias, a_log, d_skip, ssm_norm_w,
           mem_norm_w, w_mem_kv, w_out, rel_bias, final_norm_w):
    bp, s, d = x_prompt.shape
    bs, T, _ = x_sample.shape
    depth = w_in.shape[0]
    assert bp == 1 and depth == 1
    l = 0
    y_p, k_p, v_p, mk, mv, ssm_p, conv_p = _prompt_layer(
        x_prompt[0], mem_prompt[0], norm_w[l], w_in[l], conv_w[l], conv_b[l], dt_bias[l], a_log[l],
        d_skip[l], ssm_norm_w[l], mem_norm_w[l], w_mem_kv[l], w_out[l], rel_bias, final_norm_w)
    y_s, k_s, v_s, ssm_s, conv_s = _sample_layer(
        x_sample, cache_k[l], cache_v[l], page_table, cache_mem_k[l], cache_mem_v[l], state_ssm[l],
        state_conv[l], norm_w[l], w_in[l], conv_w[l], conv_b[l], dt_bias[l], a_log[l], d_skip[l],
        ssm_norm_w[l], w_out[l], rel_bias, final_norm_w)
    nm = mem_prompt.shape[1]
    return (y_p[None],
            y_s,
            k_p[None, None],
            v_p[None, None],
            mk.reshape(1, 1, nm, X_HEADS, X_HEAD_DIM),
            mv.reshape(1, 1, nm, X_HEADS, X_HEAD_DIM),
            ssm_p[None, None],
            conv_p[None, None],
            k_s.reshape(1, bs, T, ATT_HEADS, ATT_HEAD_DIM),
            v_s.reshape(1, bs, T, ATT_HEADS, ATT_HEAD_DIM),
            ssm_s[None],
            conv_s[None])
```

```python
import functools
import math

import numpy as np
import jax
import jax.numpy as jnp
from jax import lax
from jax.experimental import pallas as pl
from jax.experimental.pallas import tpu as pltpu

F32 = jnp.float32
BF16 = jnp.bfloat16
EPS = 1e-6

SSM_HEADS = 16
SSM_HEAD_DIM = 64
SSM_GROUPS = 2
SSM_HG = SSM_HEADS // SSM_GROUPS
SSM_STATE = 128
SSM_WIDTH = SSM_HEADS * SSM_HEAD_DIM
SSM_GW = SSM_WIDTH // SSM_GROUPS
SSM_CHUNK = 128
CONV_W = 4
CONV_DIM = SSM_WIDTH + 2 * SSM_GROUPS * SSM_STATE
ATT_HEADS = 8
ATT_HEAD_DIM = 64
ATT_WIDTH = ATT_HEADS * ATT_HEAD_DIM
MOBA_BLOCK = 256
MOBA_TOPK = 3
PAGE_SIZE = 128
PAGES_PER_BLOCK = MOBA_BLOCK // PAGE_SIZE
RPE_BUCKETS = 32
RPE_MAX_DIST = 128
X_HEADS = 4
X_HEAD_DIM = 128
X_WIDTH = X_HEADS * X_HEAD_DIM

LANES = 128
SUBLANES = 8
VMEM_LIMIT = 56 * 1024 * 1024
NEG = -1e30
BIG = 2.0 ** 100
LOG2E = math.log2(math.e)


def _cparams(sem, vmem=VMEM_LIMIT):
    return pltpu.CompilerParams(dimension_semantics=sem, vmem_limit_bytes=vmem)


def _split2(x):
    hi = x.astype(BF16)
    lo = (x - hi.astype(F32)).astype(BF16)
    return hi, lo


def _split3(x):
    hi = x.astype(BF16)
    r = x - hi.astype(F32)
    mid = r.astype(BF16)
    lo = (r - mid.astype(F32)).astype(BF16)
    return hi, mid, lo


def _dot(a, b):
    return jnp.dot(a, b, preferred_element_type=F32)


def _dot_nt(a, b):
    return lax.dot_general(a, b, (((1,), (1,)), ((), ())), preferred_element_type=F32)


def _dot_tn(a, b):
    return lax.dot_general(a, b, (((0,), (0,)), ((), ())), preferred_element_type=F32)


def _dot_f32_exactrhs(a, b_bf16, passes=3):
    parts = _split3(a) if passes == 3 else _split2(a)
    out = _dot(parts[0], b_bf16)
    for p in parts[1:]:
        out = out + _dot(p, b_bf16)
    return out


def _dot_hi(a, b):
    ah, al = _split2(a)
    bh, bl = _split2(b)
    return _dot(ah, bh) + _dot(al, bh) + _dot(ah, bl)


def _silu(x):
    return x / (1.0 + jnp.exp(-x))


def _softplus(x):
    return jnp.maximum(x, 0.0) + jnp.log1p(jnp.exp(-jnp.abs(x)))


def _rms(x, g):
    ms = jnp.mean(x * x, axis=-1, keepdims=True)
    return (x * lax.rsqrt(ms + EPS)) * g


_MAIN_SPLITS = (SSM_WIDTH, CONV_DIM, ATT_WIDTH, ATT_WIDTH, ATT_WIDTH, ATT_WIDTH, X_WIDTH, X_WIDTH)


def _inproj_kernel(x_ref, nw_ref, wh_ref, wl_ref, wdth_ref, wdtl_ref,
                   z_ref, xbc_ref, q_ref, k_ref, v_ref, ga_ref, qx_ref, gx_ref, dtT_ref):
    xn = _rms(x_ref[...], nw_ref[...])
    xh, xl = _split2(xn)
    outs = (z_ref, xbc_ref, q_ref, k_ref, v_ref, ga_ref, qx_ref, gx_ref)
    lo = 0
    for o_ref, width in zip(outs, _MAIN_SPLITS):
        w = wh_ref[:, lo:lo + width]
        o_ref[...] = _dot(xh, w) + _dot(xl, w) + _dot(xh, wl_ref[:, lo:lo + width])
        lo += width
    dtT_ref[...] = (_dot_nt(wdth_ref[...], xh) + _dot_nt(wdth_ref[...], xl) + _dot_nt(wdtl_ref[...], xh))


def _in_proj_sample(x, norm_w, w_in):
    m, d = x.shape
    cut0 = SSM_WIDTH + CONV_DIM
    w_main = jnp.concatenate([w_in[:, :cut0], w_in[:, cut0 + SSM_HEADS:]], axis=1)
    w_dt = w_in[:, cut0:cut0 + SSM_HEADS].T
    wh = w_main.astype(BF16)
    wdth = w_dt.astype(BF16)
    wl = (w_main - wh.astype(F32)).astype(BF16)
    wdtl = (w_dt - wdth.astype(F32)).astype(BF16)
    full = lambda a: pl.BlockSpec(a.shape, lambda: (0,) * a.ndim)
    out_shapes = [jax.ShapeDtypeStruct((m, w), F32) for w in _MAIN_SPLITS]
    out_shapes += [jax.ShapeDtypeStruct((SSM_HEADS, m), F32)]
    nw = norm_w.reshape(1, d)
    return pl.pallas_call(
        _inproj_kernel,
        in_specs=[full(x), full(nw), full(wh), full(wl), full(wdth), full(wdtl)],
        out_specs=[pl.BlockSpec(s.shape, lambda: (0, 0)) for s in out_shapes],
        out_shape=out_shapes,
        compiler_params=pltpu.CompilerParams(vmem_limit_bytes=VMEM_LIMIT),
        name="in_proj_sample",
    )(x, nw, wh, wl, wdth, wdtl)


_PROMPT_NN = (SSM_WIDTH, CONV_DIM, X_WIDTH, X_WIDTH, ATT_WIDTH)


def _inproj_prompt_kernel(x_ref, nw_ref, wn_ref, wt_ref, z_ref, xbc_ref, gaT_ref, qx_ref, gx_ref, kb_ref,
                          km_ref, qT_ref, kT_ref, vT_ref, vTb_ref, dtT_ref):
    xh = _rms(x_ref[...], nw_ref[...]).astype(BF16)
    lo = 0
    for o_ref, width in zip((z_ref, xbc_ref, qx_ref, gx_ref), _PROMPT_NN[:-1]):
        o_ref[...] = _dot(xh, wn_ref[:, lo:lo + width])
        lo += width
    k = _dot(xh, wn_ref[:, lo:lo + ATT_WIDTH])
    kb_ref[0] = k.astype(BF16)
    km_ref[0] = jnp.mean(k, axis=0, keepdims=True)
    qT_ref[0] = _dot_nt(wt_ref[0:ATT_WIDTH, :], xh)
    kT_ref[...] = _dot_nt(wt_ref[ATT_WIDTH:2 * ATT_WIDTH, :], xh)
    vT = _dot_nt(wt_ref[2 * ATT_WIDTH:3 * ATT_WIDTH, :], xh)
    vT_ref[...] = vT
    vTb_ref[0] = vT.astype(BF16)
    gaT_ref[...] = _dot_nt(wt_ref[3 * ATT_WIDTH:4 * ATT_WIDTH, :], xh)
    dtT_ref[...] = _dot_nt(wt_ref[4 * ATT_WIDTH:, :], xh)


def _in_proj_prompt(x, norm_w, w_in):
    m, d = x.shape
    tm = MOBA_BLOCK
    nb = m // tm
    c_z, c_dt = 0, SSM_WIDTH + CONV_DIM
    c_q = c_dt + SSM_HEADS
    c_k, c_qx = c_q + ATT_WIDTH, c_q + 4 * ATT_WIDTH
    cols = lambda lo, w: w_in[:, lo:lo + w]
    wn = jnp.concatenate([cols(c_z, SSM_WIDTH + CONV_DIM), cols(c_qx, 2 * X_WIDTH),
                          cols(c_k, ATT_WIDTH)], axis=1).astype(BF16)
    wt = jnp.concatenate([cols(c_q, 4 * ATT_WIDTH), cols(c_dt, SSM_HEADS)], axis=1).T.astype(BF16)
    nw = norm_w.reshape(1, d)
    row = lambda w: pl.BlockSpec((tm, w), lambda i: (i, 0))
    col = lambda h: pl.BlockSpec((h, tm), lambda i: (0, i))
    full = lambda a: pl.BlockSpec(a.shape, lambda i: (0,) * a.ndim)
    tile3 = lambda a, b: pl.BlockSpec((1, a, b), lambda i: (i, 0, 0))
    out_specs = [row(SSM_WIDTH), row(CONV_DIM), col(ATT_WIDTH), row(X_WIDTH), row(X_WIDTH),
                 tile3(tm, ATT_WIDTH), tile3(1, ATT_WIDTH), tile3(ATT_WIDTH, tm),
                 col(ATT_WIDTH), col(ATT_WIDTH), tile3(ATT_WIDTH, tm), col(SSM_HEADS)]
    sds = jax.ShapeDtypeStruct
    out_shape = [sds((m, SSM_WIDTH), F32), sds((m, CONV_DIM), F32), sds((ATT_WIDTH, m), F32),
                 sds((m, X_WIDTH), F32), sds((m, X_WIDTH), F32),
                 sds((nb, tm, ATT_WIDTH), BF16), sds((nb, 1, ATT_WIDTH), F32), sds((nb, ATT_WIDTH, tm), F32),
                 sds((ATT_WIDTH, m), F32), sds((ATT_WIDTH, m), F32), sds((nb, ATT_WIDTH, tm), BF16),
                 sds((SSM_HEADS, m), F32)]
    return pl.pallas_call(
        _inproj_prompt_kernel,
        grid=(nb,),
        in_specs=[row(d), full(nw), full(wn), full(wt)],
        out_specs=out_specs,
        out_shape=out_shape,
        compiler_params=_cparams(("parallel",)),
        name="in_proj_prompt",
    )(x, nw, wn, wt)


_SSD_CHUNKS = 2
def _ssd_kernel(xbc_ref, dtT_ref, z_ref, convw_ref, convb_ref, dtb_ref, alog_ref, dskip_ref, nw_ref,
                e_ref, y_ref, hT_ref, ext_sc, h_sc):
    c = pl.program_id(0)
    L = SSM_CHUNK

    @pl.when(c == 0)
    def _():
        ext_sc[0:SUBLANES, :] = jnp.zeros((SUBLANES, CONV_DIM), F32)
        h_sc[...] = jnp.zeros_like(h_sc)

    for sub in range(_SSD_CHUNKS):
        rows = slice(sub * L, (sub + 1) * L)
        x = xbc_ref[rows, :]
        ext_sc[SUBLANES:SUBLANES + L, :] = x
        w = convw_ref[...]
        acc = x * w[CONV_W - 1:CONV_W, :] + convb_ref[...]
        for k in range(1, CONV_W):
            acc = acc + ext_sc[SUBLANES - k:SUBLANES - k + L, :] * w[CONV_W - 1 - k:CONV_W - k, :]
        ext_sc[0:SUBLANES, :] = x[L - SUBLANES:L]
        u = _silu(acc)
        xs = u[:, :SSM_WIDTH]
        bm = u[:, SSM_WIDTH:SSM_WIDTH + SSM_GROUPS * SSM_STATE]
        cm = u[:, SSM_WIDTH + SSM_GROUPS * SSM_STATE:]

        dtT = _softplus(dtT_ref[:, rows] + dtb_ref[...])
        aT = dtT * (-jnp.exp(alog_ref[...]))
        r_i = lax.broadcasted_iota(jnp.int32, (L, L), 0)
        c_i = lax.broadcasted_iota(jnp.int32, (L, L), 1)
        upper = (r_i <= c_i).astype(BF16)
        acumT = _dot_f32_exactrhs(aT, upper)
        a_last = acumT[:, L - 1:L]
        to_endT = jnp.exp(a_last - acumT) * dtT
        eacT = jnp.exp(acumT)
        stack = jnp.concatenate(
            [acumT, dtT, to_endT, eacT, jnp.zeros((L - 4 * SSM_HEADS, L), F32)], axis=0)
        st = stack.T
        e = e_ref[...]
        dt_e = _dot_f32_exactrhs(st[:, SSM_HEADS:2 * SSM_HEADS], e, 2)
        toend_e = _dot_f32_exactrhs(st[:, 2 * SSM_HEADS:3 * SSM_HEADS], e, 2)
        eac_e = _dot_f32_exactrhs(st[:, 3 * SSM_HEADS:4 * SSM_HEADS], e, 2)
        cd_e = eac_e[L - 1:L, :]

        xd = (xs * dt_e).astype(BF16)
        xw = (xs * toend_e).astype(BF16)
        causal = r_i >= c_i
        ys = []
        yoffs = []
        for g in range(SSM_GROUPS):
            bg = bm[:, g * SSM_STATE:(g + 1) * SSM_STATE].astype(BF16)
            cg = cm[:, g * SSM_STATE:(g + 1) * SSM_STATE].astype(BF16)
            cb = jnp.where(causal, _dot_nt(cg, bg), 0.0)
            for hh in range(SSM_HG):
                h = g * SSM_HG + hh
                seg = st[:, h:h + 1] - acumT[h:h + 1, :]
                mix = (cb * jnp.exp(jnp.minimum(seg, 0.0))).astype(BF16)
                ys.append(_dot(mix, xd[:, h * SSM_HEAD_DIM:(h + 1) * SSM_HEAD_DIM]))
            hprev = h_sc[g]
            yoffs.append(_dot(cg, hprev.astype(BF16)))
            s_new = _dot_tn(bg, xw[:, g * SSM_GW:(g + 1) * SSM_GW])
            h_sc[g] = hprev * cd_e[:, g * SSM_GW:(g + 1) * SSM_GW] + s_new

        y = jnp.concatenate(ys, axis=1) + jnp.concatenate(yoffs, axis=1) * eac_e + dskip_ref[...] * xs
        gt = y * _silu(z_ref[rows, :])
        outs = []
        for g in range(SSM_GROUPS):
            gg = gt[:, g * SSM_GW:(g + 1) * SSM_GW]
            outs.append(gg * lax.rsqrt(jnp.mean(gg * gg, axis=-1, keepdims=True) + EPS))
        y_ref[rows, :] = jnp.concatenate(outs, axis=1) * nw_ref[...]

    @pl.when(c == pl.num_programs(0) - 1)
    def _():
        hT_ref[...] = h_sc[...]


def _head_expand():
    return jnp.asarray(np.repeat(np.eye(SSM_HEADS, dtype=np.float32), SSM_HEAD_DIM, axis=1), BF16)


def _ssd_prompt(xbc, dtT, z, conv_w, conv_b, dt_bias, a_log, d_skip, ssm_norm_w):
    m = xbc.shape[0]
    L = SSM_CHUNK
    step = L * _SSD_CHUNKS
    row = lambda w: pl.BlockSpec((step, w), lambda i: (i, 0))
    full = lambda a: pl.BlockSpec(a.shape, lambda i: (0,) * a.ndim)
    convb = conv_b.reshape(1, CONV_DIM)
    dtb = dt_bias.reshape(SSM_HEADS, 1)
    alog = a_log.reshape(SSM_HEADS, 1)
    dskip = jnp.repeat(d_skip, SSM_HEAD_DIM).reshape(1, SSM_WIDTH)
    nw = ssm_norm_w.reshape(1, SSM_WIDTH)
    e = _head_expand()
    y, hT = pl.pallas_call(
        _ssd_kernel,
        grid=(m // step,),
        in_specs=[row(CONV_DIM), pl.BlockSpec((SSM_HEADS, step), lambda i: (0, i)), row(SSM_WIDTH),
                  full(conv_w), full(convb), full(dtb), full(alog), full(dskip), full(nw), full(e)],
        out_specs=[row(SSM_WIDTH), pl.BlockSpec((SSM_GROUPS, SSM_STATE, SSM_GW), lambda i: (0, 0, 0))],
        out_shape=[jax.ShapeDtypeStruct((m, SSM_WIDTH), F32),
                   jax.ShapeDtypeStruct((SSM_GROUPS, SSM_STATE, SSM_GW), F32)],
        scratch_shapes=[pltpu.VMEM((SUBLANES + L, CONV_DIM), F32),
                        pltpu.VMEM((SSM_GROUPS, SSM_STATE, SSM_GW), F32)],
        compiler_params=_cparams(("arbitrary",)),
        name="ssd_prompt",
    )(xbc, dtT, z, conv_w, convb, dtb, alog, dskip, nw, e)
    state = hT.reshape(SSM_GROUPS, SSM_STATE, SSM_HG, SSM_HEAD_DIM).transpose(0, 2, 3, 1)
    return y, state.reshape(SSM_HEADS, SSM_HEAD_DIM, SSM_STATE)


def _bucket_np(dist):
    n = np.maximum(dist, 0)
    exact = RPE_BUCKETS // 2
    nf = np.maximum(n, 1).astype(np.float32)
    scaled = (np.log(nf / np.float32(exact)) / np.float32(math.log(RPE_MAX_DIST / exact))
              * np.float32(RPE_BUCKETS - exact))
    large = exact + scaled.astype(np.int32)
    return np.where(n < exact, n, np.minimum(large, RPE_BUCKETS - 1)).astype(np.int32)


def _bias_kernel(rb_ref, map_ref, out_ref, *, scale):
    h = pl.program_id(0)
    bmap = map_ref[...]
    acc = jnp.full(bmap.shape, -2.0 * BIG, F32)
    for b in range(RPE_BUCKETS):
        acc = jnp.where(bmap == b, rb_ref[b * ATT_HEADS + h] * scale, acc)
    out_ref[0] = acc


def _bias_table(rel_bias, bucket_map, scale=1.0):
    bmap = jnp.asarray(bucket_map, jnp.int32)
    nd = bmap.ndim
    return pl.pallas_call(
        functools.partial(_bias_kernel, scale=scale),
        grid=(ATT_HEADS,),
        in_specs=[pl.BlockSpec(memory_space=pltpu.SMEM),
                  pl.BlockSpec(bmap.shape, lambda h: (0,) * nd)],
        out_specs=pl.BlockSpec((1,) + bmap.shape, lambda h: (h,) + (0,) * nd),
        out_shape=jax.ShapeDtypeStruct((ATT_HEADS,) + bmap.shape, F32),
        compiler_params=_cparams(("parallel",)),
        name="rel_bias_table",
    )(rel_bias.reshape(-1), bmap)


def _topk_hits(s, blk, axis):
    picks = []
    for _ in range(MOBA_TOPK):
        mx = jnp.max(s, axis=axis, keepdims=True)
        idx = jnp.min(jnp.where(s == mx, blk, float(2 * LANES)), axis=axis, keepdims=True)
        picks.append((idx, jnp.where(mx > -jnp.inf, 1.0, 0.0)))
        s = jnp.where(blk == idx, -jnp.inf, s)
    return picks


_SPLIT = 8
_ONES_ROWS = 16


def _moba_kernel(cfar_ref, qT_ref, kb_ref, vT_ref, km_ref, d0_ref, d1_ref, gaT_ref, o_ref,
                 pen_sc, acc_sc, m_sc):
    hp = pl.program_id(0)
    i = pl.program_id(1)
    W = MOBA_BLOCK
    D = ATT_HEAD_DIM
    nb = km_ref.shape[0]
    blk = lax.broadcasted_iota(jnp.int32, (nb, W), 0).astype(F32)
    i_f = i.astype(F32)
    near = jnp.maximum(i - 1, 0)
    q_augs, cfars = [], []
    acc_sc[...] = jnp.zeros_like(acc_sc)
    for hh in range(2):
        rows = slice(hh * D, (hh + 1) * D)
        cfar2 = cfar_ref[hp * 2 + hh] * LOG2E
        qT = qT_ref[0, rows, :]
        sT = _dot_hi(km_ref[:, rows], qT)
        sT = jnp.where(blk < i_f, sT, -jnp.inf)
        hit = jnp.zeros((nb, W), F32)
        for idx, valid in _topk_hits(sT, blk, 0):
            hit = jnp.maximum(hit, jnp.where(blk == idx, valid, 0.0))
        pen_sc[hh] = jnp.where(hit > 0.0, cfar2, -2.0 * BIG)

        qs = (qT * (D ** -0.5 * LOG2E)).astype(BF16)
        zero = jnp.zeros((D, W), BF16)
        q_augs.append(jnp.concatenate([qs, zero] if hh == 0 else [zero, qs], axis=0))
        cfars.append(cfar2)

    m_sc[...] = jnp.full(m_sc.shape, -BIG, F32)
    ones = jnp.ones((_ONES_ROWS, W), BF16)

    def group(chains):
        ss = []
        for hh, a, j, bias_ref, _ in chains:
            s = _dot(kb_ref[j], q_augs[hh])
            ss.append(s if bias_ref is None else s + bias_ref[hh])
        ps = []
        for (hh, a, j, _, pen), s in zip(chains, ss):
            m = m_sc[hh, a, 0:1, :]
            m_new = jnp.maximum(m, jnp.max(s, axis=0, keepdims=True) + pen)
            m_sc[hh, a, 0:1, :] = m_new
            ps.append((jnp.exp2(m - m_new), jnp.exp2(s - (m_new - pen)).astype(BF16)))
        for (hh, a, j, _, _), (alpha, p) in zip(chains, ps):
            v1 = jnp.concatenate([vT_ref[j, hh * D:(hh + 1) * D, :], ones], axis=0)
            acc_sc[hh, a] = alpha * acc_sc[hh, a] + _dot(v1, p)

    zero_row = jnp.zeros((1, W), F32)
    group([(hh, 0, i, d0_ref, zero_row) for hh in range(2)]
          + [(hh, 1, near, d1_ref, pen_sc[hh, pl.ds(near, 1), :] - cfars[hh]) for hh in range(2)])

    def far_group(first, count):
        group([(hh, a, first + a, None, pen_sc[hh, pl.ds(first + a, 1), :])
               for a in range(count) for hh in range(2)])

    def far_body(jj, carry):
        far_group(_SPLIT * jj, _SPLIT)
        return carry

    lax.fori_loop(0, near // _SPLIT, far_body, 0)
    done = (near // _SPLIT) * _SPLIT
    count = _SPLIT // 2
    while count >= 1:
        take = (near & count) != 0

        @pl.when(take)
        def _(done=done, count=count):
            far_group(done, count)

        done = done + jnp.where(take, count, 0)
        count //= 2

    outs = []
    for hh in range(2):
        m = m_sc[hh, 0, 0:1, :]
        for a in range(1, _SPLIT):
            m = jnp.maximum(m, m_sc[hh, a, 0:1, :])
        acc = jnp.zeros((D + _ONES_ROWS, W), F32)
        for a in range(_SPLIT):
            acc = acc + jnp.exp2(m_sc[hh, a, 0:1, :] - m) * acc_sc[hh, a]
        outs.append(acc[:D] / acc[D:D + 1])
    o_ref[...] = jnp.concatenate(outs, axis=0) * _silu(gaT_ref[...])


def _moba_prompt(qT3, kb3, vT3, kmean, d0T, d1T, gaT, rel_bias):
    nb = qT3.shape[0]
    W = MOBA_BLOCK
    assert nb - 1 >= MOBA_TOPK and nb <= LANES
    cfar = rel_bias[RPE_BUCKETS - 1]
    return pl.pallas_call(
        _moba_kernel,
        grid=(ATT_HEADS // 2, nb),
        in_specs=[pl.BlockSpec(memory_space=pltpu.SMEM),
                  pl.BlockSpec((1, LANES, W), lambda hp, i: (i, hp, 0)),
                  pl.BlockSpec((nb, W, LANES), lambda hp, i: (0, 0, hp)),
                  pl.BlockSpec((nb, LANES, W), lambda hp, i: (0, hp, 0)),
                  pl.BlockSpec((nb, LANES), lambda hp, i: (0, hp)),
                  pl.BlockSpec((2, W, W), lambda hp, i: (hp, 0, 0)),
                  pl.BlockSpec((2, W, W), lambda hp, i: (hp, 0, 0)),
                  pl.BlockSpec((LANES, W), lambda hp, i: (hp, i))],
        out_specs=pl.BlockSpec((LANES, W), lambda hp, i: (hp, i)),
        out_shape=jax.ShapeDtypeStruct((ATT_WIDTH, nb * W), F32),
        scratch_shapes=[pltpu.VMEM((2, nb, W), F32),
                        pltpu.VMEM((2, _SPLIT, ATT_HEAD_DIM + _ONES_ROWS, W), F32),
                        pltpu.VMEM((2, _SPLIT, SUBLANES, W), F32)],
        compiler_params=_cparams(("parallel", "arbitrary")),
        name="moba_prompt",
    )(cfar, qT3, kb3, vT3, kmean, d0T, d1T, gaT)


def _memkv_kernel(mem_ref, g_ref, w_ref, k_ref, v_ref):
    xn = _rms(mem_ref[...], g_ref[...]).astype(BF16)
    kv = _dot(xn, w_ref[...])
    k_ref[...] = kv[:, :X_WIDTH]
    v_ref[...] = kv[:, X_WIDTH:]


def _memory_kv(mem, g, w):
    nm, d = mem.shape
    full = lambda a: pl.BlockSpec(a.shape, lambda: (0,) * a.ndim)
    g2 = g.reshape(1, d)
    wb = w.astype(BF16)
    return pl.pallas_call(
        _memkv_kernel,
        in_specs=[full(mem), full(g2), full(wb)],
        out_specs=[pl.BlockSpec((nm, X_WIDTH), lambda: (0, 0))] * 2,
        out_shape=[jax.ShapeDtypeStruct((nm, X_WIDTH), F32)] * 2,
        compiler_params=pltpu.CompilerParams(vmem_limit_bytes=VMEM_LIMIT),
        name="memory_kv",
    )(mem, g2, wb)


def _xattn_kernel(q_ref, gx_ref, mk_ref, mv_ref, o_ref):
    q = q_ref[...].reshape(-1, X_WIDTH)
    gx = gx_ref[...].reshape(-1, X_WIDTH)
    mk = mk_ref[...].reshape(-1, X_WIDTH).astype(BF16)
    mv = mv_ref[...].reshape(-1, X_WIDTH).astype(BF16)
    outs = []
    for h in range(X_HEADS):
        sl = slice(h * X_HEAD_DIM, (h + 1) * X_HEAD_DIM)
        s = _dot_nt((q[:, sl] * (X_HEAD_DIM ** -0.5)).astype(BF16), mk[:, sl])
        m = jnp.max(s, axis=1, keepdims=True)
        p = jnp.exp(s - m)
        l = jnp.sum(p, axis=1, keepdims=True)
        outs.append(_dot(p.astype(BF16), mv[:, sl]) / l)
    o_ref[...] = (jnp.concatenate(outs, axis=1) * _silu(gx)).reshape(o_ref.shape)


def _xattn_prompt(qx, gx, mk, mv, *, tq):
    m = qx.shape[0]
    row = pl.BlockSpec((tq, X_WIDTH), lambda i: (i, 0))
    full = pl.BlockSpec(mk.shape, lambda i: (0, 0))
    return pl.pallas_call(
        _xattn_kernel,
        grid=(m // tq,),
        in_specs=[row, row, full, full],
        out_specs=row,
        out_shape=jax.ShapeDtypeStruct((m, X_WIDTH), F32),
        compiler_params=_cparams(("parallel",)),
        name="xattn_prompt",
    )(qx, gx, mk, mv)


def _xattn_sample(qx, gx, mk, mv):
    b, t8, _ = qx.shape
    nm = mk.shape[1]
    qs = pl.BlockSpec((1, t8, X_WIDTH), lambda i: (i, 0, 0))
    ms = pl.BlockSpec((1, nm, X_WIDTH), lambda i: (i, 0, 0))
    return pl.pallas_call(
        _xattn_kernel,
        grid=(b,),
        in_specs=[qs, qs, ms, ms],
        out_specs=qs,
        out_shape=jax.ShapeDtypeStruct((b, t8, X_WIDTH), F32),
        compiler_params=_cparams(("parallel",)),
        name="xattn_sample",
    )(qx, gx, mk, mv)


def _outproj_tail(x_ref, ys_ref, att_proj, yx_ref, w_ref, fw_ref, o_ref):
    h = x_ref[...] + _dot(ys_ref[...].astype(BF16), w_ref[0:SSM_WIDTH, :])
    h = h + att_proj
    h = h + _dot(yx_ref[...].astype(BF16), w_ref[SSM_WIDTH + ATT_WIDTH:, :])
    o_ref[...] = _rms(h, fw_ref[...])


def _outproj_kernel(x_ref, ys_ref, ya_ref, ga_ref, yx_ref, w_ref, fw_ref, o_ref):
    att = (ya_ref[...] * _silu(ga_ref[...])).astype(BF16)
    att_proj = _dot(att, w_ref[SSM_WIDTH:SSM_WIDTH + ATT_WIDTH, :])
    _outproj_tail(x_ref, ys_ref, att_proj, yx_ref, w_ref, fw_ref, o_ref)


def _outproj_t_kernel(x_ref, ys_ref, yaT_ref, yx_ref, w_ref, fw_ref, o_ref):
    att_proj = _dot_tn(yaT_ref[...].astype(BF16), w_ref[SSM_WIDTH:SSM_WIDTH + ATT_WIDTH, :])
    _outproj_tail(x_ref, ys_ref, att_proj, yx_ref, w_ref, fw_ref, o_ref)


def _out_proj(x, y_ssd, y_att, ga, y_x, w_out, final_norm_w, *, tm):
    m, d = x.shape
    row = lambda w: pl.BlockSpec((tm, w), lambda i: (i, 0))
    full = lambda a: pl.BlockSpec(a.shape, lambda i: (0,) * a.ndim)
    wb = w_out.astype(BF16)
    fw = final_norm_w.reshape(1, d)
    if ga is None:
        body, att_args = _outproj_t_kernel, (y_att,)
        att_specs = [pl.BlockSpec((ATT_WIDTH, tm), lambda i: (0, i))]
    else:
        body, att_args = _outproj_kernel, (y_att, ga)
        att_specs = [row(ATT_WIDTH), row(ATT_WIDTH)]
    return pl.pallas_call(
        body,
        grid=(m // tm,),
        in_specs=[row(d), row(SSM_WIDTH)] + att_specs + [row(X_WIDTH), full(wb), full(fw)],
        out_specs=row(d),
        out_shape=jax.ShapeDtypeStruct((m, d), F32),
        compiler_params=_cparams(("parallel",)),
        name="out_proj",
    )(x, y_ssd, *att_args, y_x, wb, fw)


def _ssd_sample_kernel(x_ref, c_ref, dt_ref, z_ref, h0_ref, w_ref, b_ref, dtb_ref, alog_ref, dsk_ref,
                       nw_ref, sel_ref, y_ref, h_ref, y_sc, *, T):
    P = SSM_HEAD_DIM
    N = SSM_STATE
    cr = c_ref[0]
    xr = x_ref[0]
    ext = lambda i: cr[i:i + 1, :] if i < CONV_W - 1 else xr[i - (CONV_W - 1):i - (CONV_W - 2), :]
    w = w_ref[...]
    rows = []
    for t in range(T):
        u = b_ref[...]
        for j in range(CONV_W):
            u = u + w[j:j + 1, :] * ext(t + j)
        rows.append(u)
    u = _silu(jnp.concatenate(rows + [jnp.zeros((SUBLANES - T, CONV_DIM), F32)], axis=0))
    xs = u[:, :SSM_WIDTH]
    dt = _softplus(dt_ref[0] + dtb_ref[...])
    dec = jnp.exp(dt * (-jnp.exp(alog_ref[...])))
    xdt = xs[0:T] * dt
    sel = sel_ref[...]
    bc = [_dot_tn_exactrhs(u[:, SSM_WIDTH + k * N:SSM_WIDTH + (k + 1) * N], sel)
          for k in range(2 * SSM_GROUPS)]
    for pair in range(SSM_HEADS // 2):
        g = (2 * pair) // SSM_HG
        lanes = slice(pair * 2 * P, (pair + 1) * 2 * P)
        hT = jnp.concatenate([h0_ref[0, 2 * pair], h0_ref[0, 2 * pair + 1]], axis=0).T
        for t in range(T):
            tl = slice(t * N, (t + 1) * N)
            hT = hT * dec[t:t + 1, lanes] + bc[g][:, tl] * xdt[t:t + 1, lanes]
            y_sc[t:t + 1, lanes] = jnp.sum(hT * bc[SSM_GROUPS + g][:, tl], axis=0, keepdims=True)
        h2 = hT.T
        h_ref[0, 2 * pair] = h2[:P]
        h_ref[0, 2 * pair + 1] = h2[P:]
    y = y_sc[0:T, :] + dsk_ref[...] * xs[0:T]
    gt = y * _silu(z_ref[0])
    outs = []
    for g in range(SSM_GROUPS):
        gg = gt[:, g * SSM_GW:(g + 1) * SSM_GW]
        outs.append(gg * lax.rsqrt(jnp.mean(gg * gg, axis=-1, keepdims=True) + EPS))
    y_ref[0] = jnp.concatenate(outs, axis=1) * nw_ref[...]


def _dot_tn_exactrhs(a, b_bf16):
    out = None
    for part in _split3(a):
        term = _dot_tn(part, b_bf16)
        out = term if out is None else out + term
    return out


def _ssd_sample(xbc, state_conv, dt_raw, z, state_ssm, conv_w, conv_b, dt_bias, a_log, d_skip, ssm_norm_w):
    b, T, _ = xbc.shape
    per = lambda a: pl.BlockSpec((1,) + a.shape[1:], lambda i: (i,) + (0,) * (a.ndim - 1))
    full = lambda a: pl.BlockSpec(a.shape, lambda i: (0,) * a.ndim)
    per_head = lambda v: jnp.repeat(v, SSM_HEAD_DIM, axis=-1)
    dt_e = per_head(dt_raw)
    br = conv_b.reshape(1, CONV_DIM)
    dtb = per_head(dt_bias).reshape(1, SSM_WIDTH)
    alog = per_head(a_log).reshape(1, SSM_WIDTH)
    dsk = per_head(d_skip).reshape(1, SSM_WIDTH)
    nw = ssm_norm_w.reshape(1, SSM_WIDTH)
    sel = np.zeros((SUBLANES, T * SSM_STATE), np.float32)
    for t in range(T):
        sel[t, t * SSM_STATE:(t + 1) * SSM_STATE] = 1.0
    sel = jnp.asarray(sel, BF16)
    args = (xbc, state_conv, dt_e, z, state_ssm, conv_w, br, dtb, alog, dsk, nw, sel)
    in_specs = [per(a) for a in args[:5]] + [full(a) for a in args[5:]]
    return pl.pallas_call(
        functools.partial(_ssd_sample_kernel, T=T),
        grid=(b,),
        in_specs=in_specs,
        out_specs=[pl.BlockSpec((1, T, SSM_WIDTH), lambda i: (i, 0, 0)), per(state_ssm)],
        out_shape=[jax.ShapeDtypeStruct((b, T, SSM_WIDTH), F32),
                   jax.ShapeDtypeStruct(state_ssm.shape, F32)],
        scratch_shapes=[pltpu.VMEM((SUBLANES, SSM_WIDTH), F32)],
        compiler_params=_cparams(("parallel",)),
        name="ssd_sample",
    )(*args)


_KM_PAGES = 32


def _kmean_pages_kernel(pt_ref, *refs):
    o_ref = refs[-1]
    si = pl.program_id(1)
    nb = o_ref.shape[2]

    @pl.when(si == 0)
    def _():
        o_ref[...] = jnp.zeros_like(o_ref)

    lane = lax.broadcasted_iota(jnp.int32, (ATT_WIDTH, nb), 1)
    acc = o_ref[0]
    for b in range(_KM_PAGES // PAGES_PER_BLOCK):
        tok = refs[b * PAGES_PER_BLOCK][0].reshape(ATT_WIDTH, PAGE_SIZE)
        for pg in range(1, PAGES_PER_BLOCK):
            tok = tok + refs[b * PAGES_PER_BLOCK + pg][0].reshape(ATT_WIDTH, PAGE_SIZE)
        mean = jnp.sum(tok, axis=1, keepdims=True) * (1.0 / MOBA_BLOCK)
        acc = jnp.where(lane == si * (_KM_PAGES // PAGES_PER_BLOCK) + b, mean, acc)
    o_ref[0] = acc


def _kmean_sample(pool_kt, page_table):
    b, n_pages = page_table.shape
    nb = n_pages // PAGES_PER_BLOCK
    steps = n_pages // _KM_PAGES
    page_spec = lambda r: pl.BlockSpec(
        (1, ATT_HEADS, ATT_HEAD_DIM, PAGE_SIZE),
        lambda bi, si, pt: (pt[bi * n_pages + si * _KM_PAGES + r], 0, 0, 0))
    grid_spec = pltpu.PrefetchScalarGridSpec(
        num_scalar_prefetch=1,
        grid=(b, steps),
        in_specs=[page_spec(r) for r in range(_KM_PAGES)],
        out_specs=pl.BlockSpec((1, ATT_WIDTH, nb), lambda bi, si, pt: (bi, 0, 0)),
    )
    return pl.pallas_call(
        _kmean_pages_kernel,
        grid_spec=grid_spec,
        out_shape=jax.ShapeDtypeStruct((b, ATT_WIDTH, nb), F32),
        compiler_params=_cparams(("parallel", "arbitrary")),
        name="moba_kmean_pages",
    )(page_table.reshape(-1), *([pool_kt] * _KM_PAGES))


def _select_sample_kernel(q_ref, km_ref, o_ref):
    q = q_ref[0]
    nb = km_ref.shape[2]
    t8 = q.shape[0]
    blk = lax.broadcasted_iota(jnp.int32, (t8, nb), 1).astype(F32)
    lane = lax.broadcasted_iota(jnp.int32, (t8, LANES), 1)
    for h in range(ATT_HEADS):
        rows = slice(h * ATT_HEAD_DIM, (h + 1) * ATT_HEAD_DIM)
        s = _dot_hi(q[:, rows], km_ref[0, rows, :])
        tile = jnp.zeros((t8, LANES), F32)
        for r, (idx, _) in enumerate(_topk_hits(s, blk, 1)):
            tile = jnp.where(lane == r, idx, tile)
        o_ref[0, h] = tile.astype(jnp.int32)


def _select_sample(q8, kmeanT):
    b, t8, _ = q8.shape
    nb = kmeanT.shape[2]
    assert nb >= MOBA_TOPK
    return pl.pallas_call(
        _select_sample_kernel,
        grid=(b,),
        in_specs=[pl.BlockSpec((1, t8, ATT_WIDTH), lambda i: (i, 0, 0)),
                  pl.BlockSpec((1, ATT_WIDTH, nb), lambda i: (i, 0, 0))],
        out_specs=pl.BlockSpec((1, ATT_HEADS, t8, LANES), lambda i: (i, 0, 0, 0)),
        out_shape=jax.ShapeDtypeStruct((b, ATT_HEADS, t8, LANES), jnp.int32),
        compiler_params=_cparams(("parallel",)),
        name="moba_select_sample",
    )(q8, kmeanT)


_N_SLAB = MOBA_TOPK * PAGES_PER_BLOCK


def _attend_sample_kernel(sel_ref, pt_ref, rb_ref, q_ref, kn_ref, vn_ref, near_ref, pk_ref, pv_ref,
                          o_ref, kbuf, vbuf, sem, *, T, n_pages):
    bi = pl.program_id(0)
    t = pl.program_id(1)
    step = bi * T + t
    nsteps = pl.num_programs(0) * T
    nb = n_pages // PAGES_PER_BLOCK
    D = ATT_HEAD_DIM

    def copies(s, slot):
        b_s = s // T
        out = []
        for h in range(ATT_HEADS):
            for r in range(MOBA_TOPK):
                blk = sel_ref[(s * ATT_HEADS + h) * MOBA_TOPK + r]
                for pg in range(PAGES_PER_BLOCK):
                    phys = pt_ref[b_s * n_pages + blk * PAGES_PER_BLOCK + pg]
                    i = r * PAGES_PER_BLOCK + pg
                    out.append(pltpu.make_async_copy(pk_ref.at[phys, h], kbuf.at[slot, h, i], sem.at[0, slot]))
                    out.append(pltpu.make_async_copy(pv_ref.at[phys, h], vbuf.at[slot, h, i], sem.at[1, slot]))
        return out

    @pl.when(step == 0)
    def _():
        for cp in copies(step, 0):
            cp.start()

    @pl.when(step + 1 < nsteps)
    def _():
        for cp in copies(step + 1, (step + 1) % 2):
            cp.start()

    slot = step % 2
    for cp in copies(step, slot):
        cp.wait()

    lane_t = lax.broadcasted_iota(jnp.int32, (1, T), 1)
    for h in range(ATT_HEADS):
        rows = slice(h * D, (h + 1) * D)
        qc = q_ref[0, rows, :]
        cfar = rb_ref[(RPE_BUCKETS - 1) * ATT_HEADS + h]
        s_rows = []
        for r in range(MOBA_TOPK):
            blk = sel_ref[(step * ATT_HEADS + h) * MOBA_TOPK + r]
            for pg in range(PAGES_PER_BLOCK):
                s = jnp.sum(kbuf[slot, h, r * PAGES_PER_BLOCK + pg] * qc, axis=0, keepdims=True)
                s_rows.append(s + jnp.where(blk == nb - 1, near_ref[h, 0, pg:pg + 1, :], cfar))
        s_own = jnp.sum(kn_ref[0, rows, :] * qc, axis=0, keepdims=True)
        b_own = jnp.zeros((1, T), F32)
        for tp in range(T):
            b_own = jnp.where(lane_t == tp, rb_ref[jnp.maximum(t - tp, 0) * ATT_HEADS + h], b_own)
        s_own = jnp.where(lane_t <= t, s_own + b_own, NEG)
        m_row = s_rows[0]
        for s in s_rows[1:]:
            m_row = jnp.maximum(m_row, s)
        m = jnp.maximum(jnp.max(m_row, axis=1, keepdims=True), jnp.max(s_own, axis=1, keepdims=True))
        p_own = jnp.exp(s_own - m)
        l_row = jnp.zeros((1, PAGE_SIZE), F32)
        acc = jnp.zeros((D, PAGE_SIZE), F32)
        for n, s in enumerate(s_rows):
            p = jnp.exp(s - m)
            l_row = l_row + p
            acc = acc + vbuf[slot, h, n] * p
        l = jnp.sum(l_row, axis=1, keepdims=True) + jnp.sum(p_own, axis=1, keepdims=True)
        o = jnp.sum(acc, axis=1, keepdims=True)
        vn = vn_ref[0, rows, :]
        for tp in range(T):
            o = o + vn[:, tp:tp + 1] * p_own[:, tp:tp + 1]
        o_ref[0, rows, :] = o / l


def _attend_sample(sel, page_table, rel_bias, q_col, k_newT, v_newT, near, pool_kt, pool_vt):
    b, _, T = k_newT.shape
    n_pages = page_table.shape[1]
    assert T <= RPE_BUCKETS // 2
    slab = (2, ATT_HEADS, _N_SLAB, ATT_HEAD_DIM, PAGE_SIZE)
    grid_spec = pltpu.PrefetchScalarGridSpec(
        num_scalar_prefetch=3,
        grid=(b, T),
        in_specs=[pl.BlockSpec((1, ATT_WIDTH, 1), lambda bi, t, *_: (bi * T + t, 0, 0)),
                  pl.BlockSpec((1, ATT_WIDTH, T), lambda bi, t, *_: (bi, 0, 0)),
                  pl.BlockSpec((1, ATT_WIDTH, T), lambda bi, t, *_: (bi, 0, 0)),
                  pl.BlockSpec((ATT_HEADS, 1, PAGES_PER_BLOCK, PAGE_SIZE), lambda bi, t, *_: (0, t, 0, 0)),
                  pl.BlockSpec(memory_space=pl.ANY),
                  pl.BlockSpec(memory_space=pl.ANY)],
        out_specs=pl.BlockSpec((1, ATT_WIDTH, 1), lambda bi, t, *_: (bi * T + t, 0, 0)),
        scratch_shapes=[pltpu.VMEM(slab, F32), pltpu.VMEM(slab, F32), pltpu.SemaphoreType.DMA((2, 2))],
    )
    return pl.pallas_call(
        functools.partial(_attend_sample_kernel, T=T, n_pages=n_pages),
        grid_spec=grid_spec,
        out_shape=jax.ShapeDtypeStruct((b * T, ATT_WIDTH, 1), F32),
        compiler_params=_cparams(("arbitrary", "arbitrary")),
        name="moba_attend_sample",
    )(sel, page_table.reshape(-1), rel_bias.reshape(-1), q_col, k_newT, v_newT, near, pool_kt, pool_vt)


def _prompt_layer(x, mem, norm_w, w_in, conv_w, conv_b, dt_bias, a_log, d_skip, ssm_norm_w,
                  mem_norm_w, w_mem_kv, w_out, rel_bias, final_norm_w):
    s = x.shape[0]
    W = MOBA_BLOCK
    z, xbc, gaT, qx, gx, kb3, km3, qT3, kT, vT, vT3, dtT = _in_proj_prompt(x, norm_w, w_in)
    y_ssd, state = _ssd_prompt(xbc, dtT, z, conv_w, conv_b, dt_bias, a_log, d_skip, ssm_norm_w)
    ki = np.arange(W)[:, None]
    qi = np.arange(W)[None, :]
    own_map = np.where(ki <= qi, _bucket_np(qi - ki), -1)
    d01 = _bias_table(rel_bias, np.stack([own_map, _bucket_np(W + qi - ki)]), LOG2E)
    y_attT = _moba_prompt(qT3, kb3, vT3, km3.reshape(-1, ATT_WIDTH), d01[:, 0], d01[:, 1], gaT, rel_bias)
    mk, mv = _memory_kv(mem, mem_norm_w, w_mem_kv)
    y_x = _xattn_prompt(qx, gx, mk, mv, tq=512)
    y = _out_proj(x, y_ssd, y_attT, None, y_x, w_out, final_norm_w, tm=512)
    conv_state = xbc[s - (CONV_W - 1):]
    heads = lambda aT: aT.reshape(ATT_HEADS, ATT_HEAD_DIM, s).transpose(2, 0, 1)
    return y, heads(kT), heads(vT), mk, mv, state, conv_state


def _sample_layer(x, cache_k, cache_v, page_table, mem_k, mem_v, state_ssm, state_conv, norm_w, w_in,
                  conv_w, conv_b, dt_bias, a_log, d_skip, ssm_norm_w, w_out, rel_bias, final_norm_w):
    b, T, d = x.shape
    n_pages = page_table.shape[1]
    assert (n_pages * PAGE_SIZE) % MOBA_BLOCK == 0 and n_pages % _KM_PAGES == 0
    assert CONV_W - 1 <= T <= SUBLANES
    xf = x.reshape(b * T, d)
    z, xbc, q, k, v, ga, qx, gx, dtT = _in_proj_sample(xf, norm_w, w_in)
    per = lambda a: a.reshape(b, T, a.shape[-1])
    y_ssd, h_new = _ssd_sample(per(xbc), state_conv, per(dtT.T), per(z), state_ssm,
                               conv_w, conv_b, dt_bias, a_log, d_skip, ssm_norm_w)
    pad8 = lambda a: jnp.pad(per(a), ((0, 0), (0, SUBLANES - T), (0, 0)))
    pool_kt = cache_k.transpose(0, 2, 3, 1)
    pool_vt = cache_v.transpose(0, 2, 3, 1)
    kmeanT = _kmean_sample(pool_kt, page_table)
    sel = _select_sample(pad8(q), kmeanT)[:, :, :T, :MOBA_TOPK]
    sel = sel.transpose(0, 2, 1, 3).reshape(-1)
    kk = (np.arange(PAGES_PER_BLOCK)[:, None] * PAGE_SIZE + np.arange(PAGE_SIZE)[None, :])[None]
    tt = np.arange(T)[:, None, None]
    near = _bias_table(rel_bias, _bucket_np(MOBA_BLOCK + tt - kk))
    q_col = (q * (ATT_HEAD_DIM ** -0.5)).reshape(b * T, ATT_WIDTH, 1)
    y_att = _attend_sample(sel, page_table, rel_bias, q_col, per(k).transpose(0, 2, 1),
                           per(v).transpose(0, 2, 1), near, pool_kt, pool_vt)
    y_att = y_att.reshape(b * T, ATT_WIDTH)
    y_x = _xattn_sample(pad8(qx), pad8(gx), mem_k.reshape(b, -1, X_WIDTH), mem_v.reshape(b, -1, X_WIDTH))
    y_x = y_x[:, :T].reshape(b * T, X_WIDTH)
    y = _out_proj(xf, y_ssd.reshape(b * T, SSM_WIDTH), y_att, ga, y_x, w_out, final_norm_w, tm=b * T)
    conv_state = per(xbc)[:, T - (CONV_W - 1):]
    return y.reshape(b, T, d), per(k), per(v), h_new, conv_state


def kernel(x_prompt, x_sample, mem_prompt, cache_k, cache_v, page_table, cache_mem_k, cache_mem_v,
           state_ssm, state_conv, norm_w, w_in, conv_w, conv_b, dt_bias, a_log, d_skip, ssm_norm_w,
           mem_norm_w, w_mem_kv, w_out, rel_bias, final_norm_w):
    bp, s, d = x_prompt.shape
    bs, T, _ = x_sample.shape
    depth = w_in.shape[0]
    assert bp == 1 and depth == 1
    l = 0
    y_p, k_p, v_p, mk, mv, ssm_p, conv_p = _prompt_layer(
        x_prompt[0], mem_prompt[0], norm_w[l], w_in[l], conv_w[l], conv_b[l], dt_bias[l], a_log[l],
        d_skip[l], ssm_norm_w[l], mem_norm_w[l], w_mem_kv[l], w_out[l], rel_bias, final_norm_w)
    y_s, k_s, v_s, ssm_s, conv_s = _sample_layer(
        x_sample, cache_k[l], cache_v[l], page_table, cache_mem_k[l], cache_mem_v[l], state_ssm[l],
        state_conv[l], norm_w[l], w_in[l], conv_w[l], conv_b[l], dt_bias[l], a_log[l], d_skip[l],
        ssm_norm_w[l], w_out[l], rel_bias, final_norm_w)
    nm = mem_prompt.shape[1]
    return (y_p[None],
            y_s,
            k_p[None, None],
            v_p[None, None],
            mk.reshape(1, 1, nm, X_HEADS, X_HEAD_DIM),
            mv.reshape(1, 1, nm, X_HEADS, X_HEAD_DIM),
            ssm_p[None, None],
            conv_p[None, None],
            k_s.reshape(1, bs, T, ATT_HEADS, ATT_HEAD_DIM),
            v_s.reshape(1, bs, T, ATT_HEADS, ATT_HEAD_DIM),
            ssm_s[None],
            conv_s[None])
```

```python
import functools
import math

import numpy as np
import jax
import jax.numpy as jnp
from jax import lax
from jax.experimental import pallas as pl
from jax.experimental.pallas import tpu as pltpu

F32 = jnp.float32
BF16 = jnp.bfloat16
EPS = 1e-6

SSM_HEADS = 16
SSM_HEAD_DIM = 64
SSM_GROUPS = 2
SSM_HG = SSM_HEADS // SSM_GROUPS
SSM_STATE = 128
SSM_WIDTH = SSM_HEADS * SSM_HEAD_DIM
SSM_GW = SSM_WIDTH // SSM_GROUPS
SSM_CHUNK = 128
CONV_W = 4
CONV_DIM = SSM_WIDTH + 2 * SSM_GROUPS * SSM_STATE
ATT_HEADS = 8
ATT_HEAD_DIM = 64
ATT_WIDTH = ATT_HEADS * ATT_HEAD_DIM
MOBA_BLOCK = 256
MOBA_TOPK = 3
PAGE_SIZE = 128
PAGES_PER_BLOCK = MOBA_BLOCK // PAGE_SIZE
RPE_BUCKETS = 32
RPE_MAX_DIST = 128
X_HEADS = 4
X_HEAD_DIM = 128
X_WIDTH = X_HEADS * X_HEAD_DIM

LANES = 128
SUBLANES = 8
VMEM_LIMIT = 56 * 1024 * 1024
NEG = -1e30
BIG = 2.0 ** 100
LOG2E = math.log2(math.e)


def _cparams(sem, vmem=VMEM_LIMIT):
    return pltpu.CompilerParams(dimension_semantics=sem, vmem_limit_bytes=vmem)


def _split2(x):
    hi = x.astype(BF16)
    lo = (x - hi.astype(F32)).astype(BF16)
    return hi, lo


def _split3(x):
    hi = x.astype(BF16)
    r = x - hi.astype(F32)
    mid = r.astype(BF16)
    lo = (r - mid.astype(F32)).astype(BF16)
    return hi, mid, lo


def _dot(a, b):
    return jnp.dot(a, b, preferred_element_type=F32)


def _dot_nt(a, b):
    return lax.dot_general(a, b, (((1,), (1,)), ((), ())), preferred_element_type=F32)


def _dot_tn(a, b):
    return lax.dot_general(a, b, (((0,), (0,)), ((), ())), preferred_element_type=F32)


def _dot_f32_exactrhs(a, b_bf16, passes=3):
    parts = _split3(a) if passes == 3 else _split2(a)
    out = _dot(parts[0], b_bf16)
    for p in parts[1:]:
        out = out + _dot(p, b_bf16)
    return out


def _dot_hi(a, b):
    ah, al = _split2(a)
    bh, bl = _split2(b)
    return _dot(ah, bh) + _dot(al, bh) + _dot(ah, bl)


def _silu(x):
    return x / (1.0 + jnp.exp(-x))


def _softplus(x):
    return jnp.maximum(x, 0.0) + jnp.log1p(jnp.exp(-jnp.abs(x)))


def _rms(x, g):
    ms = jnp.mean(x * x, axis=-1, keepdims=True)
    return (x * lax.rsqrt(ms + EPS)) * g


_MAIN_SPLITS = (SSM_WIDTH, CONV_DIM, ATT_WIDTH, ATT_WIDTH, ATT_WIDTH, ATT_WIDTH, X_WIDTH, X_WIDTH)


_W_ROW_CHUNK = 512


def _inproj_kernel(x_ref, nw_ref, wT_ref, z_ref, xbc_ref, q_ref, k_ref, v_ref, ga_ref, qx_ref, gx_ref, dtT_ref):
    xn = _rms(x_ref[...], nw_ref[...])
    xh, xl = _split2(xn)

    def proj(lo, width):
        wh, wl = _split2(wT_ref[lo:lo + width, :])
        return _dot_nt(xh, wh) + _dot_nt(xl, wh) + _dot_nt(xh, wl)

    outs = (z_ref, xbc_ref, None, q_ref, k_ref, v_ref, ga_ref, qx_ref, gx_ref)
    lo = 0
    for o_ref, width in zip(outs, _IN_SPLITS):
        if o_ref is None:
            wh, wl = _split2(wT_ref[lo:lo + width, :])
            dtT_ref[...] = _dot_nt(wh, xh) + _dot_nt(wh, xl) + _dot_nt(wl, xh)
        else:
            for c in range(0, width, _W_ROW_CHUNK):
                o_ref[:, c:c + _W_ROW_CHUNK] = proj(lo + c, _W_ROW_CHUNK)
        lo += width


_IN_SPLITS = (SSM_WIDTH, CONV_DIM, SSM_HEADS, ATT_WIDTH, ATT_WIDTH, ATT_WIDTH, ATT_WIDTH, X_WIDTH, X_WIDTH)


def _in_proj_sample(x, norm_w, w_in):
    m, d = x.shape
    wT = w_in.T
    full = lambda a: pl.BlockSpec(a.shape, lambda: (0,) * a.ndim)
    out_shapes = [jax.ShapeDtypeStruct((m, w), F32) for w in _MAIN_SPLITS]
    out_shapes += [jax.ShapeDtypeStruct((SSM_HEADS, m), F32)]
    nw = norm_w.reshape(1, d)
    return pl.pallas_call(
        _inproj_kernel,
        in_specs=[full(x), full(nw), full(wT)],
        out_specs=[pl.BlockSpec(s.shape, lambda: (0, 0)) for s in out_shapes],
        out_shape=out_shapes,
        compiler_params=pltpu.CompilerParams(vmem_limit_bytes=VMEM_LIMIT),
        name="in_proj_sample",
    )(x, nw, wT)


_PROMPT_NN = (SSM_WIDTH, CONV_DIM, X_WIDTH, X_WIDTH, ATT_WIDTH)


def _inproj_prompt_kernel(x_ref, nw_ref, wn_ref, wt_ref, z_ref, xbc_ref, gaT_ref, qx_ref, gx_ref, kb_ref,
                          km_ref, qT_ref, kT_ref, vT_ref, vTb_ref, dtT_ref):
    xh = _rms(x_ref[...], nw_ref[...]).astype(BF16)
    lo = 0
    for o_ref, width in zip((z_ref, xbc_ref, qx_ref, gx_ref), _PROMPT_NN[:-1]):
        o_ref[...] = _dot(xh, wn_ref[:, lo:lo + width])
        lo += width
    k = _dot(xh, wn_ref[:, lo:lo + ATT_WIDTH])
    kb_ref[0] = k.astype(BF16)
    km_ref[0] = jnp.mean(k, axis=0, keepdims=True)
    qT_ref[0] = _dot_nt(wt_ref[0:ATT_WIDTH, :], xh)
    kT_ref[...] = _dot_nt(wt_ref[ATT_WIDTH:2 * ATT_WIDTH, :], xh)
    vT = _dot_nt(wt_ref[2 * ATT_WIDTH:3 * ATT_WIDTH, :], xh)
    vT_ref[...] = vT
    vTb_ref[0] = vT.astype(BF16)
    gaT_ref[...] = _dot_nt(wt_ref[3 * ATT_WIDTH:4 * ATT_WIDTH, :], xh)
    dtT_ref[...] = _dot_nt(wt_ref[4 * ATT_WIDTH:, :], xh)


def _in_proj_prompt(x, norm_w, w_in):
    m, d = x.shape
    tm = MOBA_BLOCK
    nb = m // tm
    c_z, c_dt = 0, SSM_WIDTH + CONV_DIM
    c_q = c_dt + SSM_HEADS
    c_k, c_qx = c_q + ATT_WIDTH, c_q + 4 * ATT_WIDTH
    cols = lambda lo, w: w_in[:, lo:lo + w]
    wn = jnp.concatenate([cols(c_z, SSM_WIDTH + CONV_DIM), cols(c_qx, 2 * X_WIDTH),
                          cols(c_k, ATT_WIDTH)], axis=1).astype(BF16)
    wt = jnp.concatenate([cols(c_q, 4 * ATT_WIDTH), cols(c_dt, SSM_HEADS)], axis=1).T.astype(BF16)
    nw = norm_w.reshape(1, d)
    row = lambda w: pl.BlockSpec((tm, w), lambda i: (i, 0))
    col = lambda h: pl.BlockSpec((h, tm), lambda i: (0, i))
    full = lambda a: pl.BlockSpec(a.shape, lambda i: (0,) * a.ndim)
    tile3 = lambda a, b: pl.BlockSpec((1, a, b), lambda i: (i, 0, 0))
    out_specs = [row(SSM_WIDTH), row(CONV_DIM), col(ATT_WIDTH), row(X_WIDTH), row(X_WIDTH),
                 tile3(tm, ATT_WIDTH), tile3(1, ATT_WIDTH), tile3(ATT_WIDTH, tm),
                 col(ATT_WIDTH), col(ATT_WIDTH), tile3(ATT_WIDTH, tm), col(SSM_HEADS)]
    sds = jax.ShapeDtypeStruct
    out_shape = [sds((m, SSM_WIDTH), F32), sds((m, CONV_DIM), F32), sds((ATT_WIDTH, m), F32),
                 sds((m, X_WIDTH), F32), sds((m, X_WIDTH), F32),
                 sds((nb, tm, ATT_WIDTH), BF16), sds((nb, 1, ATT_WIDTH), F32), sds((nb, ATT_WIDTH, tm), F32),
                 sds((ATT_WIDTH, m), F32), sds((ATT_WIDTH, m), F32), sds((nb, ATT_WIDTH, tm), BF16),
                 sds((SSM_HEADS, m), F32)]
    return pl.pallas_call(
        _inproj_prompt_kernel,
        grid=(nb,),
        in_specs=[row(d), full(nw), full(wn), full(wt)],
        out_specs=out_specs,
        out_shape=out_shape,
        compiler_params=_cparams(("parallel",)),
        name="in_proj_prompt",
    )(x, nw, wn, wt)


_SSD_CHUNKS = 2
def _ssd_kernel(xbc_ref, dtT_ref, z_ref, convw_ref, convb_ref, dtb_ref, alog_ref, dskip_ref, nw_ref,
                e_ref, y_ref, hT_ref, ext_sc, h_sc):
    c = pl.program_id(0)
    L = SSM_CHUNK

    @pl.when(c == 0)
    def _():
        ext_sc[0:SUBLANES, :] = jnp.zeros((SUBLANES, CONV_DIM), F32)
        h_sc[...] = jnp.zeros_like(h_sc)

    for sub in range(_SSD_CHUNKS):
        rows = slice(sub * L, (sub + 1) * L)
        x = xbc_ref[rows, :]
        ext_sc[SUBLANES:SUBLANES + L, :] = x
        w = convw_ref[...]
        acc = x * w[CONV_W - 1:CONV_W, :] + convb_ref[...]
        for k in range(1, CONV_W):
            acc = acc + ext_sc[SUBLANES - k:SUBLANES - k + L, :] * w[CONV_W - 1 - k:CONV_W - k, :]
        ext_sc[0:SUBLANES, :] = x[L - SUBLANES:L]
        u = _silu(acc)
        xs = u[:, :SSM_WIDTH]
        bm = u[:, SSM_WIDTH:SSM_WIDTH + SSM_GROUPS * SSM_STATE]
        cm = u[:, SSM_WIDTH + SSM_GROUPS * SSM_STATE:]

        dtT = _softplus(dtT_ref[:, rows] + dtb_ref[...])
        aT = dtT * (-jnp.exp(alog_ref[...]))
        r_i = lax.broadcasted_iota(jnp.int32, (L, L), 0)
        c_i = lax.broadcasted_iota(jnp.int32, (L, L), 1)
        upper = (r_i <= c_i).astype(BF16)
        acumT = _dot_f32_exactrhs(aT, upper)
        a_last = acumT[:, L - 1:L]
        to_endT = jnp.exp(a_last - acumT) * dtT
        eacT = jnp.exp(acumT)
        stack = jnp.concatenate(
            [acumT, dtT, to_endT, eacT, jnp.zeros((L - 4 * SSM_HEADS, L), F32)], axis=0)
        st = stack.T
        e = e_ref[...]
        dt_e = _dot_f32_exactrhs(st[:, SSM_HEADS:2 * SSM_HEADS], e, 2)
        toend_e = _dot_f32_exactrhs(st[:, 2 * SSM_HEADS:3 * SSM_HEADS], e, 2)
        eac_e = _dot_f32_exactrhs(st[:, 3 * SSM_HEADS:4 * SSM_HEADS], e, 2)
        cd_e = eac_e[L - 1:L, :]

        xd = (xs * dt_e).astype(BF16)
        xw = (xs * toend_e).astype(BF16)
        causal = r_i >= c_i
        ys = []
        yoffs = []
        for g in range(SSM_GROUPS):
            bg = bm[:, g * SSM_STATE:(g + 1) * SSM_STATE].astype(BF16)
            cg = cm[:, g * SSM_STATE:(g + 1) * SSM_STATE].astype(BF16)
            cb = jnp.where(causal, _dot_nt(cg, bg), 0.0)
            for hh in range(SSM_HG):
                h = g * SSM_HG + hh
                seg = st[:, h:h + 1] - acumT[h:h + 1, :]
                mix = (cb * jnp.exp(jnp.minimum(seg, 0.0))).astype(BF16)
                ys.append(_dot(mix, xd[:, h * SSM_HEAD_DIM:(h + 1) * SSM_HEAD_DIM]))
            hprev = h_sc[g]
            yoffs.append(_dot(cg, hprev.astype(BF16)))
            s_new = _dot_tn(bg, xw[:, g * SSM_GW:(g + 1) * SSM_GW])
            h_sc[g] = hprev * cd_e[:, g * SSM_GW:(g + 1) * SSM_GW] + s_new

        y = jnp.concatenate(ys, axis=1) + jnp.concatenate(yoffs, axis=1) * eac_e + dskip_ref[...] * xs
        gt = y * _silu(z_ref[rows, :])
        outs = []
        for g in range(SSM_GROUPS):
            gg = gt[:, g * SSM_GW:(g + 1) * SSM_GW]
            outs.append(gg * lax.rsqrt(jnp.mean(gg * gg, axis=-1, keepdims=True) + EPS))
        y_ref[rows, :] = jnp.concatenate(outs, axis=1) * nw_ref[...]

    @pl.when(c == pl.num_programs(0) - 1)
    def _():
        hT_ref[...] = h_sc[...]


def _head_expand():
    return jnp.asarray(np.repeat(np.eye(SSM_HEADS, dtype=np.float32), SSM_HEAD_DIM, axis=1), BF16)


def _ssd_prompt(xbc, dtT, z, conv_w, conv_b, dt_bias, a_log, d_skip, ssm_norm_w):
    m = xbc.shape[0]
    L = SSM_CHUNK
    step = L * _SSD_CHUNKS
    row = lambda w: pl.BlockSpec((step, w), lambda i: (i, 0))
    full = lambda a: pl.BlockSpec(a.shape, lambda i: (0,) * a.ndim)
    convb = conv_b.reshape(1, CONV_DIM)
    dtb = dt_bias.reshape(SSM_HEADS, 1)
    alog = a_log.reshape(SSM_HEADS, 1)
    dskip = jnp.repeat(d_skip, SSM_HEAD_DIM).reshape(1, SSM_WIDTH)
    nw = ssm_norm_w.reshape(1, SSM_WIDTH)
    e = _head_expand()
    y, hT = pl.pallas_call(
        _ssd_kernel,
        grid=(m // step,),
        in_specs=[row(CONV_DIM), pl.BlockSpec((SSM_HEADS, step), lambda i: (0, i)), row(SSM_WIDTH),
                  full(conv_w), full(convb), full(dtb), full(alog), full(dskip), full(nw), full(e)],
        out_specs=[row(SSM_WIDTH), pl.BlockSpec((SSM_GROUPS, SSM_STATE, SSM_GW), lambda i: (0, 0, 0))],
        out_shape=[jax.ShapeDtypeStruct((m, SSM_WIDTH), F32),
                   jax.ShapeDtypeStruct((SSM_GROUPS, SSM_STATE, SSM_GW), F32)],
        scratch_shapes=[pltpu.VMEM((SUBLANES + L, CONV_DIM), F32),
                        pltpu.VMEM((SSM_GROUPS, SSM_STATE, SSM_GW), F32)],
        compiler_params=_cparams(("arbitrary",)),
        name="ssd_prompt",
    )(xbc, dtT, z, conv_w, convb, dtb, alog, dskip, nw, e)
    state = hT.reshape(SSM_GROUPS, SSM_STATE, SSM_HG, SSM_HEAD_DIM).transpose(0, 2, 3, 1)
    return y, state.reshape(SSM_HEADS, SSM_HEAD_DIM, SSM_STATE)


def _bucket_np(dist):
    n = np.maximum(dist, 0)
    exact = RPE_BUCKETS // 2
    nf = np.maximum(n, 1).astype(np.float32)
    scaled = (np.log(nf / np.float32(exact)) / np.float32(math.log(RPE_MAX_DIST / exact))
              * np.float32(RPE_BUCKETS - exact))
    large = exact + scaled.astype(np.int32)
    return np.where(n < exact, n, np.minimum(large, RPE_BUCKETS - 1)).astype(np.int32)


def _bias_kernel(rb_ref, map_ref, out_ref, *, scale):
    h = pl.program_id(0)
    bmap = map_ref[...]
    acc = jnp.full(bmap.shape, -2.0 * BIG, F32)
    for b in range(RPE_BUCKETS):
        acc = jnp.where(bmap == b, rb_ref[b * ATT_HEADS + h] * scale, acc)
    out_ref[0] = acc


def _bias_table(rel_bias, bucket_map, scale=1.0):
    bmap = jnp.asarray(bucket_map, jnp.int32)
    nd = bmap.ndim
    return pl.pallas_call(
        functools.partial(_bias_kernel, scale=scale),
        grid=(ATT_HEADS,),
        in_specs=[pl.BlockSpec(memory_space=pltpu.SMEM),
                  pl.BlockSpec(bmap.shape, lambda h: (0,) * nd)],
        out_specs=pl.BlockSpec((1,) + bmap.shape, lambda h: (h,) + (0,) * nd),
        out_shape=jax.ShapeDtypeStruct((ATT_HEADS,) + bmap.shape, F32),
        compiler_params=_cparams(("parallel",)),
        name="rel_bias_table",
    )(rel_bias.reshape(-1), bmap)


def _topk_hits(s, blk, axis):
    picks = []
    for _ in range(MOBA_TOPK):
        mx = jnp.max(s, axis=axis, keepdims=True)
        idx = jnp.min(jnp.where(s == mx, blk, float(2 * LANES)), axis=axis, keepdims=True)
        picks.append((idx, jnp.where(mx > -jnp.inf, 1.0, 0.0)))
        s = jnp.where(blk == idx, -jnp.inf, s)
    return picks


_HPS = 4
_SPLIT = 4
_ONES_ROWS = 16


def _moba_kernel(cfar_ref, qT_ref, kb_ref, vT_ref, km_ref, d0_ref, d1_ref, gaT_ref, o_ref,
                 pen_sc, acc_sc, m_sc):
    hp = pl.program_id(0)
    i = pl.program_id(1)
    W = MOBA_BLOCK
    D = ATT_HEAD_DIM
    nb = km_ref.shape[0]
    blk = lax.broadcasted_iota(jnp.int32, (nb, W), 0).astype(F32)
    i_f = i.astype(F32)
    near = jnp.maximum(i - 1, 0)
    q_augs, cfars = [], []
    acc_sc[...] = jnp.zeros_like(acc_sc)
    for hh in range(_HPS):
        rows = slice(hh * D, (hh + 1) * D)
        cfar2 = cfar_ref[hp * _HPS + hh] * LOG2E
        qT = qT_ref[0, rows, :]
        sT = _dot_hi(km_ref[:, rows], qT)
        sT = jnp.where(blk < i_f, sT, -jnp.inf)
        hit = jnp.zeros((nb, W), F32)
        for idx, valid in _topk_hits(sT, blk, 0):
            hit = jnp.maximum(hit, jnp.where(blk == idx, valid, 0.0))
        pen_sc[hh] = jnp.where(hit > 0.0, cfar2, -2.0 * BIG)

        qs = (qT * (D ** -0.5 * LOG2E)).astype(BF16)
        parts = [jnp.zeros((D, W), BF16)] * _HPS
        parts[hh] = qs
        q_augs.append(jnp.concatenate(parts, axis=0))
        cfars.append(cfar2)

    m_sc[...] = jnp.full(m_sc.shape, -BIG, F32)
    ones = jnp.ones((_ONES_ROWS, W), BF16)

    def scores(hh, j, bias_ref):
        s = _dot(kb_ref[j], q_augs[hh])
        return s if bias_ref is None else s + bias_ref[hh]

    def absorb(chains, ss):
        ps = []
        for (hh, a, j, _, pen), s in zip(chains, ss):
            m = m_sc[hh, a, 0:1, :]
            m_new = jnp.maximum(m, jnp.max(s, axis=0, keepdims=True) + pen)
            m_sc[hh, a, 0:1, :] = m_new
            ps.append((jnp.exp2(m - m_new), jnp.exp2(s - (m_new - pen)).astype(BF16)))
        for (hh, a, j, _, _), (alpha, p) in zip(chains, ps):
            v1 = jnp.concatenate([vT_ref[j, hh * D:(hh + 1) * D, :], ones], axis=0)
            acc_sc[hh, a] = alpha * acc_sc[hh, a] + _dot(v1, p)

    def group(chains):
        absorb(chains, [scores(hh, j, bias_ref) for hh, _, j, bias_ref, _ in chains])

    zero_row = jnp.zeros((1, W), F32)
    group([(hh, 0, i, d0_ref, zero_row) for hh in range(_HPS)]
          + [(hh, 1, near, d1_ref, pen_sc[hh, pl.ds(near, 1), :] - cfars[hh]) for hh in range(_HPS)])

    def far_chains(first, count):
        return [(hh, a, first + a, None, pen_sc[hh, pl.ds(first + a, 1), :])
                for a in range(count) for hh in range(_HPS)]

    def far_group(first, count):
        group(far_chains(first, count))

    def far_body(trip, carry):
        far_group(_SPLIT * trip, _SPLIT)
        return carry

    lax.fori_loop(0, near // _SPLIT, far_body, 0)
    done = (near // _SPLIT) * _SPLIT
    count = _SPLIT // 2
    while count >= 1:
        take = (near & count) != 0

        @pl.when(take)
        def _(done=done, count=count):
            far_group(done, count)

        done = done + jnp.where(take, count, 0)
        count //= 2

    outs = []
    for hh in range(_HPS):
        m = m_sc[hh, 0, 0:1, :]
        for a in range(1, _SPLIT):
            m = jnp.maximum(m, m_sc[hh, a, 0:1, :])
        acc = jnp.zeros((D + _ONES_ROWS, W), F32)
        for a in range(_SPLIT):
            acc = acc + jnp.exp2(m_sc[hh, a, 0:1, :] - m) * acc_sc[hh, a]
        outs.append(acc[:D] / acc[D:D + 1])
    o_ref[...] = jnp.concatenate(outs, axis=0) * _silu(gaT_ref[...])


def _moba_prompt(qT3, kb3, vT3, kmean, d0T, d1T, gaT, rel_bias):
    nb = qT3.shape[0]
    W = MOBA_BLOCK
    HW = _HPS * ATT_HEAD_DIM
    assert nb - 1 >= MOBA_TOPK and nb <= LANES and ATT_HEADS % _HPS == 0 and HW % LANES == 0
    cfar = rel_bias[RPE_BUCKETS - 1]
    return pl.pallas_call(
        _moba_kernel,
        grid=(ATT_HEADS // _HPS, nb),
        in_specs=[pl.BlockSpec(memory_space=pltpu.SMEM),
                  pl.BlockSpec((1, HW, W), lambda hp, i: (i, hp, 0)),
                  pl.BlockSpec((nb, W, HW), lambda hp, i: (0, 0, hp)),
                  pl.BlockSpec((nb, HW, W), lambda hp, i: (0, hp, 0)),
                  pl.BlockSpec((nb, HW), lambda hp, i: (0, hp)),
                  pl.BlockSpec((_HPS, W, W), lambda hp, i: (hp, 0, 0)),
                  pl.BlockSpec((_HPS, W, W), lambda hp, i: (hp, 0, 0)),
                  pl.BlockSpec((HW, W), lambda hp, i: (hp, i))],
        out_specs=pl.BlockSpec((HW, W), lambda hp, i: (hp, i)),
        out_shape=jax.ShapeDtypeStruct((ATT_WIDTH, nb * W), F32),
        scratch_shapes=[pltpu.VMEM((_HPS, nb, W), F32),
                        pltpu.VMEM((_HPS, _SPLIT, ATT_HEAD_DIM + _ONES_ROWS, W), F32),
                        pltpu.VMEM((_HPS, _SPLIT, SUBLANES, W), F32)],
        compiler_params=_cparams(("parallel", "arbitrary")),
        name="moba_prompt",
    )(cfar, qT3, kb3, vT3, kmean, d0T, d1T, gaT)


def _memkv_kernel(mem_ref, g_ref, w_ref, k_ref, v_ref):
    xn = _rms(mem_ref[...], g_ref[...]).astype(BF16)
    kv = _dot(xn, w_ref[...])
    k_ref[...] = kv[:, :X_WIDTH]
    v_ref[...] = kv[:, X_WIDTH:]


def _memory_kv(mem, g, w):
    nm, d = mem.shape
    full = lambda a: pl.BlockSpec(a.shape, lambda: (0,) * a.ndim)
    g2 = g.reshape(1, d)
    wb = w.astype(BF16)
    return pl.pallas_call(
        _memkv_kernel,
        in_specs=[full(mem), full(g2), full(wb)],
        out_specs=[pl.BlockSpec((nm, X_WIDTH), lambda: (0, 0))] * 2,
        out_shape=[jax.ShapeDtypeStruct((nm, X_WIDTH), F32)] * 2,
        compiler_params=pltpu.CompilerParams(vmem_limit_bytes=VMEM_LIMIT),
        name="memory_kv",
    )(mem, g2, wb)


def _xattn_kernel(q_ref, gx_ref, mk_ref, mv_ref, o_ref):
    q = q_ref[...].reshape(-1, X_WIDTH)
    gx = gx_ref[...].reshape(-1, X_WIDTH)
    mk = mk_ref[...].reshape(-1, X_WIDTH).astype(BF16)
    mv = mv_ref[...].reshape(-1, X_WIDTH).astype(BF16)
    outs = []
    for h in range(X_HEADS):
        sl = slice(h * X_HEAD_DIM, (h + 1) * X_HEAD_DIM)
        s = _dot_nt((q[:, sl] * (X_HEAD_DIM ** -0.5)).astype(BF16), mk[:, sl])
        m = jnp.max(s, axis=1, keepdims=True)
        p = jnp.exp(s - m)
        l = jnp.sum(p, axis=1, keepdims=True)
        outs.append(_dot(p.astype(BF16), mv[:, sl]) / l)
    o_ref[...] = (jnp.concatenate(outs, axis=1) * _silu(gx)).reshape(o_ref.shape)


def _xattn_prompt(qx, gx, mk, mv, *, tq):
    m = qx.shape[0]
    row = pl.BlockSpec((tq, X_WIDTH), lambda i: (i, 0))
    full = pl.BlockSpec(mk.shape, lambda i: (0, 0))
    return pl.pallas_call(
        _xattn_kernel,
        grid=(m // tq,),
        in_specs=[row, row, full, full],
        out_specs=row,
        out_shape=jax.ShapeDtypeStruct((m, X_WIDTH), F32),
        compiler_params=_cparams(("parallel",)),
        name="xattn_prompt",
    )(qx, gx, mk, mv)


def _xattn_sample(qx, gx, mk, mv):
    b, t8, _ = qx.shape
    nm = mk.shape[1]
    qs = pl.BlockSpec((1, t8, X_WIDTH), lambda i: (i, 0, 0))
    ms = pl.BlockSpec((1, nm, X_WIDTH), lambda i: (i, 0, 0))
    return pl.pallas_call(
        _xattn_kernel,
        grid=(b,),
        in_specs=[qs, qs, ms, ms],
        out_specs=qs,
        out_shape=jax.ShapeDtypeStruct((b, t8, X_WIDTH), F32),
        compiler_params=_cparams(("parallel",)),
        name="xattn_sample",
    )(qx, gx, mk, mv)


def _outproj_tail(x_ref, ys_ref, att_proj, yx_ref, w_ref, fw_ref, o_ref):
    h = x_ref[...] + _dot(ys_ref[...].astype(BF16), w_ref[0:SSM_WIDTH, :])
    h = h + att_proj
    h = h + _dot(yx_ref[...].astype(BF16), w_ref[SSM_WIDTH + ATT_WIDTH:, :])
    o_ref[...] = _rms(h, fw_ref[...])


def _outproj_kernel(x_ref, ys_ref, ya_ref, ga_ref, yx_ref, w_ref, fw_ref, o_ref):
    att = (ya_ref[...] * _silu(ga_ref[...])).astype(BF16)
    att_proj = _dot(att, w_ref[SSM_WIDTH:SSM_WIDTH + ATT_WIDTH, :])
    _outproj_tail(x_ref, ys_ref, att_proj, yx_ref, w_ref, fw_ref, o_ref)


def _outproj_t_kernel(x_ref, ys_ref, yaT_ref, yx_ref, w_ref, fw_ref, o_ref):
    att_proj = _dot_tn(yaT_ref[...].astype(BF16), w_ref[SSM_WIDTH:SSM_WIDTH + ATT_WIDTH, :])
    _outproj_tail(x_ref, ys_ref, att_proj, yx_ref, w_ref, fw_ref, o_ref)


def _out_proj(x, y_ssd, y_att, ga, y_x, w_out, final_norm_w, *, tm):
    m, d = x.shape
    row = lambda w: pl.BlockSpec((tm, w), lambda i: (i, 0))
    full = lambda a: pl.BlockSpec(a.shape, lambda i: (0,) * a.ndim)
    wb = w_out.astype(BF16)
    fw = final_norm_w.reshape(1, d)
    if ga is None:
        body, att_args = _outproj_t_kernel, (y_att,)
        att_specs = [pl.BlockSpec((ATT_WIDTH, tm), lambda i: (0, i))]
    else:
        body, att_args = _outproj_kernel, (y_att, ga)
        att_specs = [row(ATT_WIDTH), row(ATT_WIDTH)]
    return pl.pallas_call(
        body,
        grid=(m // tm,),
        in_specs=[row(d), row(SSM_WIDTH)] + att_specs + [row(X_WIDTH), full(wb), full(fw)],
        out_specs=row(d),
        out_shape=jax.ShapeDtypeStruct((m, d), F32),
        compiler_params=_cparams(("parallel",)),
        name="out_proj",
    )(x, y_ssd, *att_args, y_x, wb, fw)


def _ssd_sample_kernel(x_ref, c_ref, dt_ref, z_ref, h0_ref, w_ref, b_ref, dtb_ref, alog_ref, dsk_ref,
                       nw_ref, sel_ref, y_ref, h_ref, y_sc, *, T):
    P = SSM_HEAD_DIM
    N = SSM_STATE
    cr = c_ref[0]
    xr = x_ref[0]
    ext = lambda i: cr[i:i + 1, :] if i < CONV_W - 1 else xr[i - (CONV_W - 1):i - (CONV_W - 2), :]
    w = w_ref[...]
    rows = []
    for t in range(T):
        u = b_ref[...]
        for j in range(CONV_W):
            u = u + w[j:j + 1, :] * ext(t + j)
        rows.append(u)
    u = _silu(jnp.concatenate(rows + [jnp.zeros((SUBLANES - T, CONV_DIM), F32)], axis=0))
    xs = u[:, :SSM_WIDTH]
    dt = _softplus(dt_ref[0] + dtb_ref[...])
    dec = jnp.exp(dt * (-jnp.exp(alog_ref[...])))
    xdt = xs[0:T] * dt
    sel = sel_ref[...]
    bc = [_dot_tn_exactrhs(u[:, SSM_WIDTH + k * N:SSM_WIDTH + (k + 1) * N], sel)
          for k in range(2 * SSM_GROUPS)]
    for pair in range(SSM_HEADS // 2):
        g = (2 * pair) // SSM_HG
        lanes = slice(pair * 2 * P, (pair + 1) * 2 * P)
        hT = jnp.concatenate([h0_ref[0, 2 * pair], h0_ref[0, 2 * pair + 1]], axis=0).T
        for t in range(T):
            tl = slice(t * N, (t + 1) * N)
            hT = hT * dec[t:t + 1, lanes] + bc[g][:, tl] * xdt[t:t + 1, lanes]
            y_sc[t:t + 1, lanes] = jnp.sum(hT * bc[SSM_GROUPS + g][:, tl], axis=0, keepdims=True)
        h2 = hT.T
        h_ref[0, 2 * pair] = h2[:P]
        h_ref[0, 2 * pair + 1] = h2[P:]
    y = y_sc[0:T, :] + dsk_ref[...] * xs[0:T]
    gt = y * _silu(z_ref[0])
    outs = []
    for g in range(SSM_GROUPS):
        gg = gt[:, g * SSM_GW:(g + 1) * SSM_GW]
        outs.append(gg * lax.rsqrt(jnp.mean(gg * gg, axis=-1, keepdims=True) + EPS))
    y_ref[0] = jnp.concatenate(outs, axis=1) * nw_ref[...]


def _dot_tn_exactrhs(a, b_bf16):
    out = None
    for part in _split3(a):
        term = _dot_tn(part, b_bf16)
        out = term if out is None else out + term
    return out


def _ssd_sample(xbc, state_conv, dt_raw, z, state_ssm, conv_w, conv_b, dt_bias, a_log, d_skip, ssm_norm_w):
    b, T, _ = xbc.shape
    per = lambda a: pl.BlockSpec((1,) + a.shape[1:], lambda i: (i,) + (0,) * (a.ndim - 1))
    full = lambda a: pl.BlockSpec(a.shape, lambda i: (0,) * a.ndim)
    per_head = lambda v: jnp.repeat(v, SSM_HEAD_DIM, axis=-1)
    dt_e = per_head(dt_raw)
    br = conv_b.reshape(1, CONV_DIM)
    dtb = per_head(dt_bias).reshape(1, SSM_WIDTH)
    alog = per_head(a_log).reshape(1, SSM_WIDTH)
    dsk = per_head(d_skip).reshape(1, SSM_WIDTH)
    nw = ssm_norm_w.reshape(1, SSM_WIDTH)
    sel = np.zeros((SUBLANES, T * SSM_STATE), np.float32)
    for t in range(T):
        sel[t, t * SSM_STATE:(t + 1) * SSM_STATE] = 1.0
    sel = jnp.asarray(sel, BF16)
    args = (xbc, state_conv, dt_e, z, state_ssm, conv_w, br, dtb, alog, dsk, nw, sel)
    in_specs = [per(a) for a in args[:5]] + [full(a) for a in args[5:]]
    return pl.pallas_call(
        functools.partial(_ssd_sample_kernel, T=T),
        grid=(b,),
        in_specs=in_specs,
        out_specs=[pl.BlockSpec((1, T, SSM_WIDTH), lambda i: (i, 0, 0)), per(state_ssm)],
        out_shape=[jax.ShapeDtypeStruct((b, T, SSM_WIDTH), F32),
                   jax.ShapeDtypeStruct(state_ssm.shape, F32)],
        scratch_shapes=[pltpu.VMEM((SUBLANES, SSM_WIDTH), F32)],
        compiler_params=_cparams(("parallel",)),
        name="ssd_sample",
    )(*args)


_KM_PAGES = 32


def _kmean_pages_kernel(pt_ref, *refs):
    o_ref = refs[-1]
    si = pl.program_id(1)
    nb = o_ref.shape[2]

    @pl.when(si == 0)
    def _():
        o_ref[...] = jnp.zeros_like(o_ref)

    lane = lax.broadcasted_iota(jnp.int32, (ATT_WIDTH, nb), 1)
    acc = o_ref[0]
    for b in range(_KM_PAGES // PAGES_PER_BLOCK):
        tok = refs[b * PAGES_PER_BLOCK][0].reshape(ATT_WIDTH, PAGE_SIZE)
        for pg in range(1, PAGES_PER_BLOCK):
            tok = tok + refs[b * PAGES_PER_BLOCK + pg][0].reshape(ATT_WIDTH, PAGE_SIZE)
        mean = jnp.sum(tok, axis=1, keepdims=True) * (1.0 / MOBA_BLOCK)
        acc = jnp.where(lane == si * (_KM_PAGES // PAGES_PER_BLOCK) + b, mean, acc)
    o_ref[0] = acc


def _kmean_sample(pool_kt, page_table):
    b, n_pages = page_table.shape
    nb = n_pages // PAGES_PER_BLOCK
    steps = n_pages // _KM_PAGES
    page_spec = lambda r: pl.BlockSpec(
        (1, ATT_HEADS, ATT_HEAD_DIM, PAGE_SIZE),
        lambda bi, si, pt: (pt[bi * n_pages + si * _KM_PAGES + r], 0, 0, 0))
    grid_spec = pltpu.PrefetchScalarGridSpec(
        num_scalar_prefetch=1,
        grid=(b, steps),
        in_specs=[page_spec(r) for r in range(_KM_PAGES)],
        out_specs=pl.BlockSpec((1, ATT_WIDTH, nb), lambda bi, si, pt: (bi, 0, 0)),
    )
    return pl.pallas_call(
        _kmean_pages_kernel,
        grid_spec=grid_spec,
        out_shape=jax.ShapeDtypeStruct((b, ATT_WIDTH, nb), F32),
        compiler_params=_cparams(("parallel", "arbitrary")),
        name="moba_kmean_pages",
    )(page_table.reshape(-1), *([pool_kt] * _KM_PAGES))


def _select_sample_kernel(q_ref, km_ref, o_ref):
    q = q_ref[0]
    nb = km_ref.shape[2]
    t8 = q.shape[0]
    blk = lax.broadcasted_iota(jnp.int32, (t8, nb), 1).astype(F32)
    lane = lax.broadcasted_iota(jnp.int32, (t8, LANES), 1)
    for h in range(ATT_HEADS):
        rows = slice(h * ATT_HEAD_DIM, (h + 1) * ATT_HEAD_DIM)
        s = _dot_hi(q[:, rows], km_ref[0, rows, :])
        tile = jnp.zeros((t8, LANES), F32)
        for r, (idx, _) in enumerate(_topk_hits(s, blk, 1)):
            tile = jnp.where(lane == r, idx, tile)
        o_ref[0, h] = tile.astype(jnp.int32)


def _select_sample(q8, kmeanT):
    b, t8, _ = q8.shape
    nb = kmeanT.shape[2]
    assert nb >= MOBA_TOPK
    return pl.pallas_call(
        _select_sample_kernel,
        grid=(b,),
        in_specs=[pl.BlockSpec((1, t8, ATT_WIDTH), lambda i: (i, 0, 0)),
                  pl.BlockSpec((1, ATT_WIDTH, nb), lambda i: (i, 0, 0))],
        out_specs=pl.BlockSpec((1, ATT_HEADS, t8, LANES), lambda i: (i, 0, 0, 0)),
        out_shape=jax.ShapeDtypeStruct((b, ATT_HEADS, t8, LANES), jnp.int32),
        compiler_params=_cparams(("parallel",)),
        name="moba_select_sample",
    )(q8, kmeanT)


_N_SLAB = MOBA_TOPK * PAGES_PER_BLOCK


def _attend_sample_kernel(sel_ref, pt_ref, rb_ref, q_ref, kn_ref, vn_ref, near_ref, pk_ref, pv_ref,
                          o_ref, kbuf, vbuf, sem, *, T, n_pages):
    bi = pl.program_id(0)
    t = pl.program_id(1)
    step = bi * T + t
    nsteps = pl.num_programs(0) * T
    nb = n_pages // PAGES_PER_BLOCK
    D = ATT_HEAD_DIM

    def copies(s, slot):
        b_s = s // T
        out = []
        for h in range(ATT_HEADS):
            for r in range(MOBA_TOPK):
                blk = sel_ref[(s * ATT_HEADS + h) * MOBA_TOPK + r]
                for pg in range(PAGES_PER_BLOCK):
                    phys = pt_ref[b_s * n_pages + blk * PAGES_PER_BLOCK + pg]
                    i = r * PAGES_PER_BLOCK + pg
                    out.append(pltpu.make_async_copy(pk_ref.at[phys, h], kbuf.at[slot, h, i], sem.at[0, slot]))
                    out.append(pltpu.make_async_copy(pv_ref.at[phys, h], vbuf.at[slot, h, i], sem.at[1, slot]))
        return out

    @pl.when(step == 0)
    def _():
        for cp in copies(step, 0):
            cp.start()

    @pl.when(step + 1 < nsteps)
    def _():
        for cp in copies(step + 1, (step + 1) % 2):
            cp.start()

    slot = step % 2
    for cp in copies(step, slot):
        cp.wait()

    lane_t = lax.broadcasted_iota(jnp.int32, (1, T), 1)
    for h in range(ATT_HEADS):
        rows = slice(h * D, (h + 1) * D)
        qc = q_ref[0, rows, :]
        cfar = rb_ref[(RPE_BUCKETS - 1) * ATT_HEADS + h]
        s_rows = []
        for r in range(MOBA_TOPK):
            blk = sel_ref[(step * ATT_HEADS + h) * MOBA_TOPK + r]
            for pg in range(PAGES_PER_BLOCK):
                s = jnp.sum(kbuf[slot, h, r * PAGES_PER_BLOCK + pg] * qc, axis=0, keepdims=True)
                s_rows.append(s + jnp.where(blk == nb - 1, near_ref[h, 0, pg:pg + 1, :], cfar))
        s_own = jnp.sum(kn_ref[0, rows, :] * qc, axis=0, keepdims=True)
        b_own = jnp.zeros((1, T), F32)
        for tp in range(T):
            b_own = jnp.where(lane_t == tp, rb_ref[jnp.maximum(t - tp, 0) * ATT_HEADS + h], b_own)
        s_own = jnp.where(lane_t <= t, s_own + b_own, NEG)
        m_row = s_rows[0]
        for s in s_rows[1:]:
            m_row = jnp.maximum(m_row, s)
        m = jnp.maximum(jnp.max(m_row, axis=1, keepdims=True), jnp.max(s_own, axis=1, keepdims=True))
        p_own = jnp.exp(s_own - m)
        l_row = jnp.zeros((1, PAGE_SIZE), F32)
        acc = jnp.zeros((D, PAGE_SIZE), F32)
        for n, s in enumerate(s_rows):
            p = jnp.exp(s - m)
            l_row = l_row + p
            acc = acc + vbuf[slot, h, n] * p
        l = jnp.sum(l_row, axis=1, keepdims=True) + jnp.sum(p_own, axis=1, keepdims=True)
        o = jnp.sum(acc, axis=1, keepdims=True)
        vn = vn_ref[0, rows, :]
        for tp in range(T):
            o = o + vn[:, tp:tp + 1] * p_own[:, tp:tp + 1]
        o_ref[0, rows, :] = o / l


def _attend_sample(sel, page_table, rel_bias, q_col, k_newT, v_newT, near, pool_kt, pool_vt):
    b, _, T = k_newT.shape
    n_pages = page_table.shape[1]
    assert T <= RPE_BUCKETS // 2
    slab = (2, ATT_HEADS, _N_SLAB, ATT_HEAD_DIM, PAGE_SIZE)
    grid_spec = pltpu.PrefetchScalarGridSpec(
        num_scalar_prefetch=3,
        grid=(b, T),
        in_specs=[pl.BlockSpec((1, ATT_WIDTH, 1), lambda bi, t, *_: (bi * T + t, 0, 0)),
                  pl.BlockSpec((1, ATT_WIDTH, T), lambda bi, t, *_: (bi, 0, 0)),
                  pl.BlockSpec((1, ATT_WIDTH, T), lambda bi, t, *_: (bi, 0, 0)),
                  pl.BlockSpec((ATT_HEADS, 1, PAGES_PER_BLOCK, PAGE_SIZE), lambda bi, t, *_: (0, t, 0, 0)),
                  pl.BlockSpec(memory_space=pl.ANY),
                  pl.BlockSpec(memory_space=pl.ANY)],
        out_specs=pl.BlockSpec((1, ATT_WIDTH, 1), lambda bi, t, *_: (bi * T + t, 0, 0)),
        scratch_shapes=[pltpu.VMEM(slab, F32), pltpu.VMEM(slab, F32), pltpu.SemaphoreType.DMA((2, 2))],
    )
    return pl.pallas_call(
        functools.partial(_attend_sample_kernel, T=T, n_pages=n_pages),
        grid_spec=grid_spec,
        out_shape=jax.ShapeDtypeStruct((b * T, ATT_WIDTH, 1), F32),
        compiler_params=_cparams(("arbitrary", "arbitrary")),
        name="moba_attend_sample",
    )(sel, page_table.reshape(-1), rel_bias.reshape(-1), q_col, k_newT, v_newT, near, pool_kt, pool_vt)


def _prompt_layer(x, mem, norm_w, w_in, conv_w, conv_b, dt_bias, a_log, d_skip, ssm_norm_w,
                  mem_norm_w, w_mem_kv, w_out, rel_bias, final_norm_w):
    s = x.shape[0]
    W = MOBA_BLOCK
    z, xbc, gaT, qx, gx, kb3, km3, qT3, kT, vT, vT3, dtT = _in_proj_prompt(x, norm_w, w_in)
    y_ssd, state = _ssd_prompt(xbc, dtT, z, conv_w, conv_b, dt_bias, a_log, d_skip, ssm_norm_w)
    ki = np.arange(W)[:, None]
    qi = np.arange(W)[None, :]
    own_map = np.where(ki <= qi, _bucket_np(qi - ki), -1)
    d01 = _bias_table(rel_bias, np.stack([own_map, _bucket_np(W + qi - ki)]), LOG2E)
    y_attT = _moba_prompt(qT3, kb3, vT3, km3.reshape(-1, ATT_WIDTH), d01[:, 0], d01[:, 1], gaT, rel_bias)
    mk, mv = _memory_kv(mem, mem_norm_w, w_mem_kv)
    y_x = _xattn_prompt(qx, gx, mk, mv, tq=512)
    y = _out_proj(x, y_ssd, y_attT, None, y_x, w_out, final_norm_w, tm=512)
    conv_state = xbc[s - (CONV_W - 1):]
    heads = lambda aT: aT.reshape(ATT_HEADS, ATT_HEAD_DIM, s).transpose(2, 0, 1)
    return y, heads(kT), heads(vT), mk, mv, state, conv_state


def _sample_layer(x, cache_k, cache_v, page_table, mem_k, mem_v, state_ssm, state_conv, norm_w, w_in,
                  conv_w, conv_b, dt_bias, a_log, d_skip, ssm_norm_w, w_out, rel_bias, final_norm_w):
    b, T, d = x.shape
    n_pages = page_table.shape[1]
    assert (n_pages * PAGE_SIZE) % MOBA_BLOCK == 0 and n_pages % _KM_PAGES == 0
    assert CONV_W - 1 <= T <= SUBLANES
    xf = x.reshape(b * T, d)
    z, xbc, q, k, v, ga, qx, gx, dtT = _in_proj_sample(xf, norm_w, w_in)
    per = lambda a: a.reshape(b, T, a.shape[-1])
    y_ssd, h_new = _ssd_sample(per(xbc), state_conv, per(dtT.T), per(z), state_ssm,
                               conv_w, conv_b, dt_bias, a_log, d_skip, ssm_norm_w)
    pad8 = lambda a: jnp.pad(per(a), ((0, 0), (0, SUBLANES - T), (0, 0)))
    pool_kt = cache_k.transpose(0, 2, 3, 1)
    pool_vt = cache_v.transpose(0, 2, 3, 1)
    kmeanT = _kmean_sample(pool_kt, page_table)
    sel = _select_sample(pad8(q), kmeanT)[:, :, :T, :MOBA_TOPK]
    sel = sel.transpose(0, 2, 1, 3).reshape(-1)
    kk = (np.arange(PAGES_PER_BLOCK)[:, None] * PAGE_SIZE + np.arange(PAGE_SIZE)[None, :])[None]
    tt = np.arange(T)[:, None, None]
    near = _bias_table(rel_bias, _bucket_np(MOBA_BLOCK + tt - kk))
    q_col = (q * (ATT_HEAD_DIM ** -0.5)).reshape(b * T, ATT_WIDTH, 1)
    y_att = _attend_sample(sel, page_table, rel_bias, q_col, per(k).transpose(0, 2, 1),
                           per(v).transpose(0, 2, 1), near, pool_kt, pool_vt)
    y_att = y_att.reshape(b * T, ATT_WIDTH)
    y_x = _xattn_sample(pad8(qx), pad8(gx), mem_k.reshape(b, -1, X_WIDTH), mem_v.reshape(b, -1, X_WIDTH))
    y_x = y_x[:, :T].reshape(b * T, X_WIDTH)
    y = _out_proj(xf, y_ssd.reshape(b * T, SSM_WIDTH), y_att, ga, y_x, w_out, final_norm_w, tm=b * T)
    conv_state = per(xbc)[:, T - (CONV_W - 1):]
    return y.reshape(b, T, d), per(k), per(v), h_new, conv_state


def kernel(x_prompt, x_sample, mem_prompt, cache_k, cache_v, page_table, cache_mem_k, cache_mem_v,
           state_ssm, state_conv, norm_w, w_in, conv_w, conv_b, dt_bias, a_log, d_skip, ssm_norm_w,
           mem_norm_w, w_mem_kv, w_out, rel_bias, final_norm_w):
    bp, s, d = x_prompt.shape
    bs, T, _ = x_sample.shape
    depth = w_in.shape[0]
    assert bp == 1 and depth == 1
    l = 0
    y_p, k_p, v_p, mk, mv, ssm_p, conv_p = _prompt_layer(
        x_prompt[0], mem_prompt[0], norm_w[l], w_in[l], conv_w[l], conv_b[l], dt_bias[l], a_log[l],
        d_skip[l], ssm_norm_w[l], mem_norm_w[l], w_mem_kv[l], w_out[l], rel_bias, final_norm_w)
    y_s, k_s, v_s, ssm_s, conv_s = _sample_layer(
        x_sample, cache_k[l], cache_v[l], page_table, cache_mem_k[l], cache_mem_v[l], state_ssm[l],
        state_conv[l], norm_w[l], w_in[l], conv_w[l], conv_b[l], dt_bias[l], a_log[l], d_skip[l],
        ssm_norm_w[l], w_out[l], rel_bias, final_norm_w)
    nm = mem_prompt.shape[1]
    return (y_p[None],
            y_s,
            k_p[None, None],
            v_p[None, None],
            mk.reshape(1, 1, nm, X_HEADS, X_HEAD_DIM),
            mv.reshape(1, 1, nm, X_HEADS, X_HEAD_DIM),
            ssm_p[None, None],
            conv_p[None, None],
            k_s.reshape(1, bs, T, ATT_HEADS, ATT_HEAD_DIM),
            v_s.reshape(1, bs, T, ATT_HEADS, ATT_HEAD_DIM),
            ssm_s[None],
            conv_s[None])
```

```python
import functools
import math

import numpy as np
import jax
import jax.numpy as jnp
from jax import lax
from jax.experimental import pallas as pl
from jax.experimental.pallas import tpu as pltpu

F32 = jnp.float32
BF16 = jnp.bfloat16
EPS = 1e-6

SSM_HEADS = 16
SSM_HEAD_DIM = 64
SSM_GROUPS = 2
SSM_HG = SSM_HEADS // SSM_GROUPS
SSM_STATE = 128
SSM_WIDTH = SSM_HEADS * SSM_HEAD_DIM
SSM_GW = SSM_WIDTH // SSM_GROUPS
SSM_CHUNK = 128
CONV_W = 4
CONV_DIM = SSM_WIDTH + 2 * SSM_GROUPS * SSM_STATE
ATT_HEADS = 8
ATT_HEAD_DIM = 64
ATT_WIDTH = ATT_HEADS * ATT_HEAD_DIM
MOBA_BLOCK = 256
MOBA_TOPK = 3
PAGE_SIZE = 128
PAGES_PER_BLOCK = MOBA_BLOCK // PAGE_SIZE
RPE_BUCKETS = 32
RPE_MAX_DIST = 128
X_HEADS = 4
X_HEAD_DIM = 128
X_WIDTH = X_HEADS * X_HEAD_DIM

LANES = 128
SUBLANES = 8
VMEM_LIMIT = 56 * 1024 * 1024
NEG = -1e30
BIG = 2.0 ** 100
LOG2E = math.log2(math.e)


def _cparams(sem, vmem=VMEM_LIMIT):
    return pltpu.CompilerParams(dimension_semantics=sem, vmem_limit_bytes=vmem)


def _split2(x):
    hi = x.astype(BF16)
    lo = (x - hi.astype(F32)).astype(BF16)
    return hi, lo


def _split3(x):
    hi = x.astype(BF16)
    r = x - hi.astype(F32)
    mid = r.astype(BF16)
    lo = (r - mid.astype(F32)).astype(BF16)
    return hi, mid, lo


def _dot(a, b):
    return jnp.dot(a, b, preferred_element_type=F32)


def _dot_nt(a, b):
    return lax.dot_general(a, b, (((1,), (1,)), ((), ())), preferred_element_type=F32)


def _dot_tn(a, b):
    return lax.dot_general(a, b, (((0,), (0,)), ((), ())), preferred_element_type=F32)


def _dot_f32_exactrhs(a, b_bf16, passes=3):
    parts = _split3(a) if passes == 3 else _split2(a)
    out = _dot(parts[0], b_bf16)
    for p in parts[1:]:
        out = out + _dot(p, b_bf16)
    return out


def _dot_hi(a, b):
    ah, al = _split2(a)
    bh, bl = _split2(b)
    return _dot(ah, bh) + _dot(al, bh) + _dot(ah, bl)


def _silu(x):
    return x / (1.0 + jnp.exp(-x))


def _softplus(x):
    return jnp.maximum(x, 0.0) + jnp.log1p(jnp.exp(-jnp.abs(x)))


def _rms(x, g):
    ms = jnp.mean(x * x, axis=-1, keepdims=True)
    return (x * lax.rsqrt(ms + EPS)) * g


_MAIN_SPLITS = (SSM_WIDTH, CONV_DIM, ATT_WIDTH, ATT_WIDTH, ATT_WIDTH, ATT_WIDTH, X_WIDTH, X_WIDTH)


_W_ROW_CHUNK = 512


def _inproj_kernel(x_ref, nw_ref, wT_ref, z_ref, xbc_ref, q_ref, k_ref, v_ref, ga_ref, qx_ref, gx_ref, dtT_ref):
    xn = _rms(x_ref[...], nw_ref[...])
    xh, xl = _split2(xn)

    def proj(lo, width):
        wh, wl = _split2(wT_ref[lo:lo + width, :])
        return _dot_nt(xh, wh) + _dot_nt(xl, wh) + _dot_nt(xh, wl)

    outs = (z_ref, xbc_ref, None, q_ref, k_ref, v_ref, ga_ref, qx_ref, gx_ref)
    lo = 0
    for o_ref, width in zip(outs, _IN_SPLITS):
        if o_ref is None:
            wh, wl = _split2(wT_ref[lo:lo + width, :])
            dtT_ref[...] = _dot_nt(wh, xh) + _dot_nt(wh, xl) + _dot_nt(wl, xh)
        else:
            for c in range(0, width, _W_ROW_CHUNK):
                o_ref[:, c:c + _W_ROW_CHUNK] = proj(lo + c, _W_ROW_CHUNK)
        lo += width


_IN_SPLITS = (SSM_WIDTH, CONV_DIM, SSM_HEADS, ATT_WIDTH, ATT_WIDTH, ATT_WIDTH, ATT_WIDTH, X_WIDTH, X_WIDTH)


def _in_proj_sample(x, norm_w, w_in):
    m, d = x.shape
    wT = w_in.T
    full = lambda a: pl.BlockSpec(a.shape, lambda: (0,) * a.ndim)
    out_shapes = [jax.ShapeDtypeStruct((m, w), F32) for w in _MAIN_SPLITS]
    out_shapes += [jax.ShapeDtypeStruct((SSM_HEADS, m), F32)]
    nw = norm_w.reshape(1, d)
    return pl.pallas_call(
        _inproj_kernel,
        in_specs=[full(x), full(nw), full(wT)],
        out_specs=[pl.BlockSpec(s.shape, lambda: (0, 0)) for s in out_shapes],
        out_shape=out_shapes,
        compiler_params=pltpu.CompilerParams(vmem_limit_bytes=VMEM_LIMIT),
        name="in_proj_sample",
    )(x, nw, wT)


_PROMPT_NN = (SSM_WIDTH, CONV_DIM, X_WIDTH, X_WIDTH, ATT_WIDTH)


def _inproj_prompt_kernel(x_ref, nw_ref, wn_ref, wt_ref, z_ref, xbc_ref, gaT_ref, qx_ref, gx_ref, kb_ref,
                          km_ref, qT_ref, kT_ref, vT_ref, vTb_ref, dtT_ref):
    xh = _rms(x_ref[...], nw_ref[...]).astype(BF16)
    lo = 0
    for o_ref, width in zip((z_ref, xbc_ref, qx_ref, gx_ref), _PROMPT_NN[:-1]):
        o_ref[...] = _dot(xh, wn_ref[:, lo:lo + width])
        lo += width
    k = _dot(xh, wn_ref[:, lo:lo + ATT_WIDTH])
    kb_ref[0] = k.astype(BF16)
    km_ref[0] = jnp.mean(k, axis=0, keepdims=True)
    qT_ref[0] = _dot_nt(wt_ref[0:ATT_WIDTH, :], xh)
    kT_ref[...] = _dot_nt(wt_ref[ATT_WIDTH:2 * ATT_WIDTH, :], xh)
    vT = _dot_nt(wt_ref[2 * ATT_WIDTH:3 * ATT_WIDTH, :], xh)
    vT_ref[...] = vT
    vTb_ref[0] = vT.astype(BF16)
    gaT_ref[...] = _dot_nt(wt_ref[3 * ATT_WIDTH:4 * ATT_WIDTH, :], xh)
    dtT_ref[...] = _dot_nt(wt_ref[4 * ATT_WIDTH:, :], xh)


def _in_proj_prompt(x, norm_w, w_in):
    m, d = x.shape
    tm = MOBA_BLOCK
    nb = m // tm
    c_z, c_dt = 0, SSM_WIDTH + CONV_DIM
    c_q = c_dt + SSM_HEADS
    c_k, c_qx = c_q + ATT_WIDTH, c_q + 4 * ATT_WIDTH
    cols = lambda lo, w: w_in[:, lo:lo + w]
    wn = jnp.concatenate([cols(c_z, SSM_WIDTH + CONV_DIM), cols(c_qx, 2 * X_WIDTH),
                          cols(c_k, ATT_WIDTH)], axis=1).astype(BF16)
    wt = jnp.concatenate([cols(c_q, 4 * ATT_WIDTH), cols(c_dt, SSM_HEADS)], axis=1).T.astype(BF16)
    nw = norm_w.reshape(1, d)
    row = lambda w: pl.BlockSpec((tm, w), lambda i: (i, 0))
    col = lambda h: pl.BlockSpec((h, tm), lambda i: (0, i))
    full = lambda a: pl.BlockSpec(a.shape, lambda i: (0,) * a.ndim)
    tile3 = lambda a, b: pl.BlockSpec((1, a, b), lambda i: (i, 0, 0))
    out_specs = [row(SSM_WIDTH), row(CONV_DIM), col(ATT_WIDTH), row(X_WIDTH), row(X_WIDTH),
                 tile3(tm, ATT_WIDTH), tile3(1, ATT_WIDTH), tile3(ATT_WIDTH, tm),
                 col(ATT_WIDTH), col(ATT_WIDTH), tile3(ATT_WIDTH, tm), col(SSM_HEADS)]
    sds = jax.ShapeDtypeStruct
    out_shape = [sds((m, SSM_WIDTH), F32), sds((m, CONV_DIM), F32), sds((ATT_WIDTH, m), F32),
                 sds((m, X_WIDTH), F32), sds((m, X_WIDTH), F32),
                 sds((nb, tm, ATT_WIDTH), BF16), sds((nb, 1, ATT_WIDTH), F32), sds((nb, ATT_WIDTH, tm), F32),
                 sds((ATT_WIDTH, m), F32), sds((ATT_WIDTH, m), F32), sds((nb, ATT_WIDTH, tm), BF16),
                 sds((SSM_HEADS, m), F32)]
    return pl.pallas_call(
        _inproj_prompt_kernel,
        grid=(nb,),
        in_specs=[row(d), full(nw), full(wn), full(wt)],
        out_specs=out_specs,
        out_shape=out_shape,
        compiler_params=_cparams(("parallel",)),
        name="in_proj_prompt",
    )(x, nw, wn, wt)


_SSD_CHUNKS = 2
def _ssd_kernel(xbc_ref, dtT_ref, z_ref, convw_ref, convb_ref, dtb_ref, alog_ref, dskip_ref, nw_ref,
                e_ref, y_ref, hT_ref, ext_sc, h_sc):
    c = pl.program_id(0)
    L = SSM_CHUNK

    @pl.when(c == 0)
    def _():
        ext_sc[0:SUBLANES, :] = jnp.zeros((SUBLANES, CONV_DIM), F32)
        h_sc[...] = jnp.zeros_like(h_sc)

    for sub in range(_SSD_CHUNKS):
        rows = slice(sub * L, (sub + 1) * L)
        x = xbc_ref[rows, :]
        ext_sc[SUBLANES:SUBLANES + L, :] = x
        w = convw_ref[...]
        acc = x * w[CONV_W - 1:CONV_W, :] + convb_ref[...]
        for k in range(1, CONV_W):
            acc = acc + ext_sc[SUBLANES - k:SUBLANES - k + L, :] * w[CONV_W - 1 - k:CONV_W - k, :]
        ext_sc[0:SUBLANES, :] = x[L - SUBLANES:L]
        u = _silu(acc)
        xs = u[:, :SSM_WIDTH]
        bm = u[:, SSM_WIDTH:SSM_WIDTH + SSM_GROUPS * SSM_STATE]
        cm = u[:, SSM_WIDTH + SSM_GROUPS * SSM_STATE:]

        dtT = _softplus(dtT_ref[:, rows] + dtb_ref[...])
        aT = dtT * (-jnp.exp(alog_ref[...]))
        r_i = lax.broadcasted_iota(jnp.int32, (L, L), 0)
        c_i = lax.broadcasted_iota(jnp.int32, (L, L), 1)
        upper = (r_i <= c_i).astype(BF16)
        acumT = _dot_f32_exactrhs(aT, upper)
        a_last = acumT[:, L - 1:L]
        to_endT = jnp.exp(a_last - acumT) * dtT
        eacT = jnp.exp(acumT)
        stack = jnp.concatenate(
            [acumT, dtT, to_endT, eacT, jnp.zeros((L - 4 * SSM_HEADS, L), F32)], axis=0)
        st = stack.T
        e = e_ref[...]
        dt_e = _dot_f32_exactrhs(st[:, SSM_HEADS:2 * SSM_HEADS], e, 2)
        toend_e = _dot_f32_exactrhs(st[:, 2 * SSM_HEADS:3 * SSM_HEADS], e, 2)
        eac_e = _dot_f32_exactrhs(st[:, 3 * SSM_HEADS:4 * SSM_HEADS], e, 2)
        cd_e = eac_e[L - 1:L, :]

        xd = (xs * dt_e).astype(BF16)
        xw = (xs * toend_e).astype(BF16)
        causal = r_i >= c_i
        ys = []
        yoffs = []
        for g in range(SSM_GROUPS):
            bg = bm[:, g * SSM_STATE:(g + 1) * SSM_STATE].astype(BF16)
            cg = cm[:, g * SSM_STATE:(g + 1) * SSM_STATE].astype(BF16)
            cb = jnp.where(causal, _dot_nt(cg, bg), 0.0)
            for hh in range(SSM_HG):
                h = g * SSM_HG + hh
                seg = st[:, h:h + 1] - acumT[h:h + 1, :]
                mix = (cb * jnp.exp(jnp.minimum(seg, 0.0))).astype(BF16)
                ys.append(_dot(mix, xd[:, h * SSM_HEAD_DIM:(h + 1) * SSM_HEAD_DIM]))
            hprev = h_sc[g]
            yoffs.append(_dot(cg, hprev.astype(BF16)))
            s_new = _dot_tn(bg, xw[:, g * SSM_GW:(g + 1) * SSM_GW])
            h_sc[g] = hprev * cd_e[:, g * SSM_GW:(g + 1) * SSM_GW] + s_new

        y = jnp.concatenate(ys, axis=1) + jnp.concatenate(yoffs, axis=1) * eac_e + dskip_ref[...] * xs
        gt = y * _silu(z_ref[rows, :])
        outs = []
        for g in range(SSM_GROUPS):
            gg = gt[:, g * SSM_GW:(g + 1) * SSM_GW]
            outs.append(gg * lax.rsqrt(jnp.mean(gg * gg, axis=-1, keepdims=True) + EPS))
        y_ref[rows, :] = jnp.concatenate(outs, axis=1) * nw_ref[...]

    @pl.when(c == pl.num_programs(0) - 1)
    def _():
        hT_ref[...] = h_sc[...]


def _head_expand():
    return jnp.asarray(np.repeat(np.eye(SSM_HEADS, dtype=np.float32), SSM_HEAD_DIM, axis=1), BF16)


def _ssd_prompt(xbc, dtT, z, conv_w, conv_b, dt_bias, a_log, d_skip, ssm_norm_w):
    m = xbc.shape[0]
    L = SSM_CHUNK
    step = L * _SSD_CHUNKS
    row = lambda w: pl.BlockSpec((step, w), lambda i: (i, 0))
    full = lambda a: pl.BlockSpec(a.shape, lambda i: (0,) * a.ndim)
    convb = conv_b.reshape(1, CONV_DIM)
    dtb = dt_bias.reshape(SSM_HEADS, 1)
    alog = a_log.reshape(SSM_HEADS, 1)
    dskip = jnp.repeat(d_skip, SSM_HEAD_DIM).reshape(1, SSM_WIDTH)
    nw = ssm_norm_w.reshape(1, SSM_WIDTH)
    e = _head_expand()
    y, hT = pl.pallas_call(
        _ssd_kernel,
        grid=(m // step,),
        in_specs=[row(CONV_DIM), pl.BlockSpec((SSM_HEADS, step), lambda i: (0, i)), row(SSM_WIDTH),
                  full(conv_w), full(convb), full(dtb), full(alog), full(dskip), full(nw), full(e)],
        out_specs=[row(SSM_WIDTH), pl.BlockSpec((SSM_GROUPS, SSM_STATE, SSM_GW), lambda i: (0, 0, 0))],
        out_shape=[jax.ShapeDtypeStruct((m, SSM_WIDTH), F32),
                   jax.ShapeDtypeStruct((SSM_GROUPS, SSM_STATE, SSM_GW), F32)],
        scratch_shapes=[pltpu.VMEM((SUBLANES + L, CONV_DIM), F32),
                        pltpu.VMEM((SSM_GROUPS, SSM_STATE, SSM_GW), F32)],
        compiler_params=_cparams(("arbitrary",)),
        name="ssd_prompt",
    )(xbc, dtT, z, conv_w, convb, dtb, alog, dskip, nw, e)
    state = hT.reshape(SSM_GROUPS, SSM_STATE, SSM_HG, SSM_HEAD_DIM).transpose(0, 2, 3, 1)
    return y, state.reshape(SSM_HEADS, SSM_HEAD_DIM, SSM_STATE)


def _bucket_np(dist):
    n = np.maximum(dist, 0)
    exact = RPE_BUCKETS // 2
    nf = np.maximum(n, 1).astype(np.float32)
    scaled = (np.log(nf / np.float32(exact)) / np.float32(math.log(RPE_MAX_DIST / exact))
              * np.float32(RPE_BUCKETS - exact))
    large = exact + scaled.astype(np.int32)
    return np.where(n < exact, n, np.minimum(large, RPE_BUCKETS - 1)).astype(np.int32)


def _bias_kernel(rb_ref, map_ref, out_ref, *, scale):
    h = pl.program_id(0)
    bmap = map_ref[...]
    acc = jnp.full(bmap.shape, -2.0 * BIG, F32)
    for b in range(RPE_BUCKETS):
        acc = jnp.where(bmap == b, rb_ref[b * ATT_HEADS + h] * scale, acc)
    out_ref[0] = acc


def _bias_table(rel_bias, bucket_map, scale=1.0):
    bmap = jnp.asarray(bucket_map, jnp.int32)
    nd = bmap.ndim
    return pl.pallas_call(
        functools.partial(_bias_kernel, scale=scale),
        grid=(ATT_HEADS,),
        in_specs=[pl.BlockSpec(memory_space=pltpu.SMEM),
                  pl.BlockSpec(bmap.shape, lambda h: (0,) * nd)],
        out_specs=pl.BlockSpec((1,) + bmap.shape, lambda h: (h,) + (0,) * nd),
        out_shape=jax.ShapeDtypeStruct((ATT_HEADS,) + bmap.shape, F32),
        compiler_params=_cparams(("parallel",)),
        name="rel_bias_table",
    )(rel_bias.reshape(-1), bmap)


def _topk_hits(s, blk, axis):
    picks = []
    for _ in range(MOBA_TOPK):
        mx = jnp.max(s, axis=axis, keepdims=True)
        idx = jnp.min(jnp.where(s == mx, blk, float(2 * LANES)), axis=axis, keepdims=True)
        picks.append((idx, jnp.where(mx > -jnp.inf, 1.0, 0.0)))
        s = jnp.where(blk == idx, -jnp.inf, s)
    return picks


_HPS = 4
_SPLIT = 4
_ONES_ROWS = 16


def _moba_kernel(cfar_ref, qT_ref, kb_ref, vT_ref, km_ref, d0_ref, d1_ref, gaT_ref, o_ref,
                 pen_sc, acc_sc, m_sc):
    hp = pl.program_id(0)
    i = pl.program_id(1)
    W = MOBA_BLOCK
    D = ATT_HEAD_DIM
    nb = km_ref.shape[0]
    blk = lax.broadcasted_iota(jnp.int32, (nb, W), 0).astype(F32)
    i_f = i.astype(F32)
    near = jnp.maximum(i - 1, 0)
    q_augs, cfars = [], []
    acc_sc[...] = jnp.zeros_like(acc_sc)
    for hh in range(_HPS):
        rows = slice(hh * D, (hh + 1) * D)
        cfar2 = cfar_ref[hp * _HPS + hh] * LOG2E
        qT = qT_ref[0, rows, :]
        sT = _dot_hi(km_ref[:, rows], qT)
        sT = jnp.where(blk < i_f, sT, -jnp.inf)
        hit = jnp.zeros((nb, W), F32)
        for idx, valid in _topk_hits(sT, blk, 0):
            hit = jnp.maximum(hit, jnp.where(blk == idx, valid, 0.0))
        pen_sc[hh] = jnp.where(hit > 0.0, cfar2, -2.0 * BIG)

        qs = (qT * (D ** -0.5 * LOG2E)).astype(BF16)
        parts = [jnp.zeros((D, W), BF16)] * _HPS
        parts[hh] = qs
        q_augs.append(jnp.concatenate(parts, axis=0))
        cfars.append(cfar2)

    m_sc[...] = jnp.full(m_sc.shape, -BIG, F32)
    ones = jnp.ones((_ONES_ROWS, W), BF16)

    def scores(hh, j, bias_ref):
        s = _dot(kb_ref[j], q_augs[hh])
        return s if bias_ref is None else s + bias_ref[hh]

    def absorb(chains, ss):
        ps = []
        for (hh, a, j, _, pen), s in zip(chains, ss):
            m = m_sc[hh, a, 0:1, :]
            m_new = jnp.maximum(m, jnp.max(s, axis=0, keepdims=True) + pen)
            m_sc[hh, a, 0:1, :] = m_new
            ps.append((jnp.exp2(m - m_new), jnp.exp2(s - (m_new - pen)).astype(BF16)))
        for (hh, a, j, _, _), (alpha, p) in zip(chains, ps):
            v1 = jnp.concatenate([vT_ref[j, hh * D:(hh + 1) * D, :], ones], axis=0)
            acc_sc[hh, a] = alpha * acc_sc[hh, a] + _dot(v1, p)

    def group(chains):
        absorb(chains, [scores(hh, j, bias_ref) for hh, _, j, bias_ref, _ in chains])

    zero_row = jnp.zeros((1, W), F32)
    group([(hh, 0, i, d0_ref, zero_row) for hh in range(_HPS)]
          + [(hh, 1, near, d1_ref, pen_sc[hh, pl.ds(near, 1), :] - cfars[hh]) for hh in range(_HPS)])

    def far_chains(first, count):
        return [(hh, a, first + a, None, pen_sc[hh, pl.ds(first + a, 1), :])
                for a in range(count) for hh in range(_HPS)]

    def far_group(first, count):
        group(far_chains(first, count))

    def far_body(trip, carry):
        far_group(_SPLIT * trip, _SPLIT)
        return carry

    lax.fori_loop(0, near // _SPLIT, far_body, 0)
    done = (near // _SPLIT) * _SPLIT
    count = _SPLIT // 2
    while count >= 1:
        take = (near & count) != 0

        @pl.when(take)
        def _(done=done, count=count):
            far_group(done, count)

        done = done + jnp.where(take, count, 0)
        count //= 2

    outs = []
    for hh in range(_HPS):
        m = m_sc[hh, 0, 0:1, :]
        for a in range(1, _SPLIT):
            m = jnp.maximum(m, m_sc[hh, a, 0:1, :])
        acc = jnp.zeros((D + _ONES_ROWS, W), F32)
        for a in range(_SPLIT):
            acc = acc + jnp.exp2(m_sc[hh, a, 0:1, :] - m) * acc_sc[hh, a]
        outs.append(acc[:D] / acc[D:D + 1])
    o_ref[...] = jnp.concatenate(outs, axis=0) * _silu(gaT_ref[...])


def _moba_prompt(qT3, kb3, vT3, kmean, d0T, d1T, gaT, rel_bias):
    nb = qT3.shape[0]
    W = MOBA_BLOCK
    HW = _HPS * ATT_HEAD_DIM
    assert nb - 1 >= MOBA_TOPK and nb <= LANES and ATT_HEADS % _HPS == 0 and HW % LANES == 0
    cfar = rel_bias[RPE_BUCKETS - 1]
    return pl.pallas_call(
        _moba_kernel,
        grid=(ATT_HEADS // _HPS, nb),
        in_specs=[pl.BlockSpec(memory_space=pltpu.SMEM),
                  pl.BlockSpec((1, HW, W), lambda hp, i: (i, hp, 0)),
                  pl.BlockSpec((nb, W, HW), lambda hp, i: (0, 0, hp)),
                  pl.BlockSpec((nb, HW, W), lambda hp, i: (0, hp, 0)),
                  pl.BlockSpec((nb, HW), lambda hp, i: (0, hp)),
                  pl.BlockSpec((_HPS, W, W), lambda hp, i: (hp, 0, 0)),
                  pl.BlockSpec((_HPS, W, W), lambda hp, i: (hp, 0, 0)),
                  pl.BlockSpec((HW, W), lambda hp, i: (hp, i))],
        out_specs=pl.BlockSpec((HW, W), lambda hp, i: (hp, i)),
        out_shape=jax.ShapeDtypeStruct((ATT_WIDTH, nb * W), F32),
        scratch_shapes=[pltpu.VMEM((_HPS, nb, W), F32),
                        pltpu.VMEM((_HPS, _SPLIT, ATT_HEAD_DIM + _ONES_ROWS, W), F32),
                        pltpu.VMEM((_HPS, _SPLIT, SUBLANES, W), F32)],
        compiler_params=_cparams(("parallel", "arbitrary")),
        name="moba_prompt",
    )(cfar, qT3, kb3, vT3, kmean, d0T, d1T, gaT)


def _memkv_kernel(mem_ref, g_ref, w_ref, k_ref, v_ref):
    xn = _rms(mem_ref[...], g_ref[...]).astype(BF16)
    kv = _dot(xn, w_ref[...])
    k_ref[...] = kv[:, :X_WIDTH]
    v_ref[...] = kv[:, X_WIDTH:]


def _memory_kv(mem, g, w):
    nm, d = mem.shape
    full = lambda a: pl.BlockSpec(a.shape, lambda: (0,) * a.ndim)
    g2 = g.reshape(1, d)
    wb = w.astype(BF16)
    return pl.pallas_call(
        _memkv_kernel,
        in_specs=[full(mem), full(g2), full(wb)],
        out_specs=[pl.BlockSpec((nm, X_WIDTH), lambda: (0, 0))] * 2,
        out_shape=[jax.ShapeDtypeStruct((nm, X_WIDTH), F32)] * 2,
        compiler_params=pltpu.CompilerParams(vmem_limit_bytes=VMEM_LIMIT),
        name="memory_kv",
    )(mem, g2, wb)


def _xattn_kernel(q_ref, gx_ref, mk_ref, mv_ref, o_ref):
    q = q_ref[...].reshape(-1, X_WIDTH)
    gx = gx_ref[...].reshape(-1, X_WIDTH)
    mk = mk_ref[...].reshape(-1, X_WIDTH).astype(BF16)
    mv = mv_ref[...].reshape(-1, X_WIDTH).astype(BF16)
    outs = []
    for h in range(X_HEADS):
        sl = slice(h * X_HEAD_DIM, (h + 1) * X_HEAD_DIM)
        s = _dot_nt((q[:, sl] * (X_HEAD_DIM ** -0.5)).astype(BF16), mk[:, sl])
        m = jnp.max(s, axis=1, keepdims=True)
        p = jnp.exp(s - m)
        l = jnp.sum(p, axis=1, keepdims=True)
        outs.append(_dot(p.astype(BF16), mv[:, sl]) / l)
    o_ref[...] = (jnp.concatenate(outs, axis=1) * _silu(gx)).reshape(o_ref.shape)


def _xattn_prompt(qx, gx, mk, mv, *, tq):
    m = qx.shape[0]
    row = pl.BlockSpec((tq, X_WIDTH), lambda i: (i, 0))
    full = pl.BlockSpec(mk.shape, lambda i: (0, 0))
    return pl.pallas_call(
        _xattn_kernel,
        grid=(m // tq,),
        in_specs=[row, row, full, full],
        out_specs=row,
        out_shape=jax.ShapeDtypeStruct((m, X_WIDTH), F32),
        compiler_params=_cparams(("parallel",)),
        name="xattn_prompt",
    )(qx, gx, mk, mv)


def _xattn_sample_kernel(q_ref, gx_ref, mk_ref, mv_ref, o_ref):
    q = q_ref[0]
    outs = []
    for h in range(X_HEADS):
        sl = slice(h * X_HEAD_DIM, (h + 1) * X_HEAD_DIM)
        s = _dot_nt((q[:, sl] * (X_HEAD_DIM ** -0.5)).astype(BF16), mk_ref[0, :, h, :].astype(BF16))
        m = jnp.max(s, axis=1, keepdims=True)
        p = jnp.exp(s - m)
        l = jnp.sum(p, axis=1, keepdims=True)
        outs.append(_dot(p.astype(BF16), mv_ref[0, :, h, :].astype(BF16)) / l)
    o_ref[0] = jnp.concatenate(outs, axis=1) * _silu(gx_ref[0])


def _xattn_sample(qx, gx, mk, mv):
    b, t8, _ = qx.shape
    nm = mk.shape[1]
    qs = pl.BlockSpec((1, t8, X_WIDTH), lambda i: (i, 0, 0))
    ms = pl.BlockSpec((1, nm, X_HEADS, X_HEAD_DIM), lambda i: (i, 0, 0, 0))
    return pl.pallas_call(
        _xattn_sample_kernel,
        grid=(b,),
        in_specs=[qs, qs, ms, ms],
        out_specs=qs,
        out_shape=jax.ShapeDtypeStruct((b, t8, X_WIDTH), F32),
        compiler_params=_cparams(("parallel",)),
        name="xattn_sample",
    )(qx, gx, mk, mv)


def _outproj_tail(x_ref, ys_ref, att_proj, yx_ref, w_ref, fw_ref, o_ref):
    h = x_ref[...] + _dot(ys_ref[...].astype(BF16), w_ref[0:SSM_WIDTH, :])
    h = h + att_proj
    h = h + _dot(yx_ref[...].astype(BF16), w_ref[SSM_WIDTH + ATT_WIDTH:, :])
    o_ref[...] = _rms(h, fw_ref[...])


def _outproj_kernel(x_ref, ys_ref, ya_ref, ga_ref, yx_ref, w_ref, fw_ref, o_ref):
    att = (ya_ref[...] * _silu(ga_ref[...])).astype(BF16)
    att_proj = _dot(att, w_ref[SSM_WIDTH:SSM_WIDTH + ATT_WIDTH, :])
    _outproj_tail(x_ref, ys_ref, att_proj, yx_ref, w_ref, fw_ref, o_ref)


def _outproj_t_kernel(x_ref, ys_ref, yaT_ref, yx_ref, w_ref, fw_ref, o_ref):
    att_proj = _dot_tn(yaT_ref[...].astype(BF16), w_ref[SSM_WIDTH:SSM_WIDTH + ATT_WIDTH, :])
    _outproj_tail(x_ref, ys_ref, att_proj, yx_ref, w_ref, fw_ref, o_ref)


def _out_proj(x, y_ssd, y_att, ga, y_x, w_out, final_norm_w, *, tm):
    m, d = x.shape
    row = lambda w: pl.BlockSpec((tm, w), lambda i: (i, 0))
    full = lambda a: pl.BlockSpec(a.shape, lambda i: (0,) * a.ndim)
    wb = w_out.astype(BF16)
    fw = final_norm_w.reshape(1, d)
    if ga is None:
        body, att_args = _outproj_t_kernel, (y_att,)
        att_specs = [pl.BlockSpec((ATT_WIDTH, tm), lambda i: (0, i))]
    else:
        body, att_args = _outproj_kernel, (y_att, ga)
        att_specs = [row(ATT_WIDTH), row(ATT_WIDTH)]
    return pl.pallas_call(
        body,
        grid=(m // tm,),
        in_specs=[row(d), row(SSM_WIDTH)] + att_specs + [row(X_WIDTH), full(wb), full(fw)],
        out_specs=row(d),
        out_shape=jax.ShapeDtypeStruct((m, d), F32),
        compiler_params=_cparams(("parallel",)),
        name="out_proj",
    )(x, y_ssd, *att_args, y_x, wb, fw)


def _ssd_sample_kernel(x_ref, c_ref, dt_ref, z_ref, h0_ref, w_ref, b_ref, dtb_ref, alog_ref, dsk_ref,
                       nw_ref, sel_ref, y_ref, h_ref, y_sc, *, T):
    P = SSM_HEAD_DIM
    N = SSM_STATE
    cr = c_ref[0]
    xr = x_ref[0]
    ext = lambda i: cr[i:i + 1, :] if i < CONV_W - 1 else xr[i - (CONV_W - 1):i - (CONV_W - 2), :]
    w = w_ref[...]
    rows = []
    for t in range(T):
        u = b_ref[...]
        for j in range(CONV_W):
            u = u + w[j:j + 1, :] * ext(t + j)
        rows.append(u)
    u = _silu(jnp.concatenate(rows + [jnp.zeros((SUBLANES - T, CONV_DIM), F32)], axis=0))
    xs = u[:, :SSM_WIDTH]
    dt = _softplus(dt_ref[0] + dtb_ref[...])
    dec = jnp.exp(dt * (-jnp.exp(alog_ref[...])))
    xdt = xs[0:T] * dt
    sel = sel_ref[...]
    bc = [_dot_tn_exactrhs(u[:, SSM_WIDTH + k * N:SSM_WIDTH + (k + 1) * N], sel)
          for k in range(2 * SSM_GROUPS)]
    for pair in range(SSM_HEADS // 2):
        g = (2 * pair) // SSM_HG
        lanes = slice(pair * 2 * P, (pair + 1) * 2 * P)
        hT = jnp.concatenate([h0_ref[0, 2 * pair], h0_ref[0, 2 * pair + 1]], axis=0).T
        for t in range(T):
            tl = slice(t * N, (t + 1) * N)
            hT = hT * dec[t:t + 1, lanes] + bc[g][:, tl] * xdt[t:t + 1, lanes]
            y_sc[t:t + 1, lanes] = jnp.sum(hT * bc[SSM_GROUPS + g][:, tl], axis=0, keepdims=True)
        h2 = hT.T
        h_ref[0, 2 * pair] = h2[:P]
        h_ref[0, 2 * pair + 1] = h2[P:]
    y = y_sc[0:T, :] + dsk_ref[...] * xs[0:T]
    gt = y * _silu(z_ref[0])
    outs = []
    for g in range(SSM_GROUPS):
        gg = gt[:, g * SSM_GW:(g + 1) * SSM_GW]
        outs.append(gg * lax.rsqrt(jnp.mean(gg * gg, axis=-1, keepdims=True) + EPS))
    y_ref[0] = jnp.concatenate(outs, axis=1) * nw_ref[...]


def _dot_tn_exactrhs(a, b_bf16):
    out = None
    for part in _split3(a):
        term = _dot_tn(part, b_bf16)
        out = term if out is None else out + term
    return out


def _ssd_sample(xbc, state_conv, dt_raw, z, state_ssm, conv_w, conv_b, dt_bias, a_log, d_skip, ssm_norm_w):
    b, T, _ = xbc.shape
    per = lambda a: pl.BlockSpec((1,) + a.shape[1:], lambda i: (i,) + (0,) * (a.ndim - 1))
    full = lambda a: pl.BlockSpec(a.shape, lambda i: (0,) * a.ndim)
    per_head = lambda v: jnp.repeat(v, SSM_HEAD_DIM, axis=-1)
    dt_e = per_head(dt_raw)
    br = conv_b.reshape(1, CONV_DIM)
    dtb = per_head(dt_bias).reshape(1, SSM_WIDTH)
    alog = per_head(a_log).reshape(1, SSM_WIDTH)
    dsk = per_head(d_skip).reshape(1, SSM_WIDTH)
    nw = ssm_norm_w.reshape(1, SSM_WIDTH)
    sel = np.zeros((SUBLANES, T * SSM_STATE), np.float32)
    for t in range(T):
        sel[t, t * SSM_STATE:(t + 1) * SSM_STATE] = 1.0
    sel = jnp.asarray(sel, BF16)
    args = (xbc, state_conv, dt_e, z, state_ssm, conv_w, br, dtb, alog, dsk, nw, sel)
    in_specs = [per(a) for a in args[:5]] + [full(a) for a in args[5:]]
    return pl.pallas_call(
        functools.partial(_ssd_sample_kernel, T=T),
        grid=(b,),
        in_specs=in_specs,
        out_specs=[pl.BlockSpec((1, T, SSM_WIDTH), lambda i: (i, 0, 0)), per(state_ssm)],
        out_shape=[jax.ShapeDtypeStruct((b, T, SSM_WIDTH), F32),
                   jax.ShapeDtypeStruct(state_ssm.shape, F32)],
        scratch_shapes=[pltpu.VMEM((SUBLANES, SSM_WIDTH), F32)],
        compiler_params=_cparams(("parallel",)),
        name="ssd_sample",
    )(*args)


_KM_PAGES = 32


def _kmean_pages_kernel(pt_ref, *refs):
    o_ref = refs[-1]
    si = pl.program_id(1)
    nb = o_ref.shape[2]

    @pl.when(si == 0)
    def _():
        o_ref[...] = jnp.zeros_like(o_ref)

    lane = lax.broadcasted_iota(jnp.int32, (ATT_WIDTH, nb), 1)
    acc = o_ref[0]
    for b in range(_KM_PAGES // PAGES_PER_BLOCK):
        tok = refs[b * PAGES_PER_BLOCK][0].reshape(ATT_WIDTH, PAGE_SIZE)
        for pg in range(1, PAGES_PER_BLOCK):
            tok = tok + refs[b * PAGES_PER_BLOCK + pg][0].reshape(ATT_WIDTH, PAGE_SIZE)
        mean = jnp.sum(tok, axis=1, keepdims=True) * (1.0 / MOBA_BLOCK)
        acc = jnp.where(lane == si * (_KM_PAGES // PAGES_PER_BLOCK) + b, mean, acc)
    o_ref[0] = acc


def _kmean_sample(pool_kt, page_table):
    b, n_pages = page_table.shape
    nb = n_pages // PAGES_PER_BLOCK
    steps = n_pages // _KM_PAGES
    page_spec = lambda r: pl.BlockSpec(
        (1, ATT_HEADS, ATT_HEAD_DIM, PAGE_SIZE),
        lambda bi, si, pt: (pt[bi * n_pages + si * _KM_PAGES + r], 0, 0, 0))
    grid_spec = pltpu.PrefetchScalarGridSpec(
        num_scalar_prefetch=1,
        grid=(b, steps),
        in_specs=[page_spec(r) for r in range(_KM_PAGES)],
        out_specs=pl.BlockSpec((1, ATT_WIDTH, nb), lambda bi, si, pt: (bi, 0, 0)),
    )
    return pl.pallas_call(
        _kmean_pages_kernel,
        grid_spec=grid_spec,
        out_shape=jax.ShapeDtypeStruct((b, ATT_WIDTH, nb), F32),
        compiler_params=_cparams(("parallel", "arbitrary")),
        name="moba_kmean_pages",
    )(page_table.reshape(-1), *([pool_kt] * _KM_PAGES))


def _select_sample_kernel(q_ref, km_ref, o_ref):
    q = q_ref[0]
    nb = km_ref.shape[2]
    t8 = q.shape[0]
    blk = lax.broadcasted_iota(jnp.int32, (t8, nb), 1).astype(F32)
    lane = lax.broadcasted_iota(jnp.int32, (t8, LANES), 1)
    for h in range(ATT_HEADS):
        rows = slice(h * ATT_HEAD_DIM, (h + 1) * ATT_HEAD_DIM)
        s = _dot_hi(q[:, rows], km_ref[0, rows, :])
        tile = jnp.zeros((t8, LANES), F32)
        for r, (idx, _) in enumerate(_topk_hits(s, blk, 1)):
            tile = jnp.where(lane == r, idx, tile)
        o_ref[0, h] = tile.astype(jnp.int32)


def _select_sample(q8, kmeanT):
    b, t8, _ = q8.shape
    nb = kmeanT.shape[2]
    assert nb >= MOBA_TOPK
    return pl.pallas_call(
        _select_sample_kernel,
        grid=(b,),
        in_specs=[pl.BlockSpec((1, t8, ATT_WIDTH), lambda i: (i, 0, 0)),
                  pl.BlockSpec((1, ATT_WIDTH, nb), lambda i: (i, 0, 0))],
        out_specs=pl.BlockSpec((1, ATT_HEADS, t8, LANES), lambda i: (i, 0, 0, 0)),
        out_shape=jax.ShapeDtypeStruct((b, ATT_HEADS, t8, LANES), jnp.int32),
        compiler_params=_cparams(("parallel",)),
        name="moba_select_sample",
    )(q8, kmeanT)


_N_SLAB = MOBA_TOPK * PAGES_PER_BLOCK


def _attend_sample_kernel(sel_ref, pt_ref, rb_ref, q_ref, kn_ref, vn_ref, near_ref, pk_ref, pv_ref,
                          o_ref, kbuf, vbuf, sem, *, T, n_pages):
    bi = pl.program_id(0)
    t = pl.program_id(1)
    step = bi * T + t
    nsteps = pl.num_programs(0) * T
    nb = n_pages // PAGES_PER_BLOCK
    D = ATT_HEAD_DIM

    def copies(s, slot):
        b_s = s // T
        out = []
        for h in range(ATT_HEADS):
            for r in range(MOBA_TOPK):
                blk = sel_ref[(s * ATT_HEADS + h) * MOBA_TOPK + r]
                for pg in range(PAGES_PER_BLOCK):
                    phys = pt_ref[b_s * n_pages + blk * PAGES_PER_BLOCK + pg]
                    i = r * PAGES_PER_BLOCK + pg
                    out.append(pltpu.make_async_copy(pk_ref.at[phys, h], kbuf.at[slot, h, i], sem.at[0, slot]))
                    out.append(pltpu.make_async_copy(pv_ref.at[phys, h], vbuf.at[slot, h, i], sem.at[1, slot]))
        return out

    @pl.when(step == 0)
    def _():
        for cp in copies(step, 0):
            cp.start()

    @pl.when(step + 1 < nsteps)
    def _():
        for cp in copies(step + 1, (step + 1) % 2):
            cp.start()

    slot = step % 2
    for cp in copies(step, slot):
        cp.wait()

    first_row = jnp.where(lax.broadcasted_iota(jnp.int32, (SUBLANES, PAGE_SIZE), 0) == 0, 1.0, 0.0).astype(BF16)
    q8 = jnp.concatenate([q_ref[0], jnp.zeros((SUBLANES - 1, ATT_WIDTH), F32)], axis=0)
    qb = _dot_tn_exactrhs(q8, first_row)
    lane_t = lax.broadcasted_iota(jnp.int32, (1, T), 1)
    neg_tail = jnp.full((1, PAGE_SIZE - T), NEG, F32)
    neg_row = jnp.full((1, PAGE_SIZE), NEG, F32)

    tiles = []
    for h in range(ATT_HEADS):
        rows = slice(h * D, (h + 1) * D)
        cfar = rb_ref[(RPE_BUCKETS - 1) * ATT_HEADS + h]
        s_rows = []
        for r in range(MOBA_TOPK):
            blk = sel_ref[(step * ATT_HEADS + h) * MOBA_TOPK + r]
            for pg in range(PAGES_PER_BLOCK):
                s = jnp.sum(kbuf[slot, h, r * PAGES_PER_BLOCK + pg] * qb[rows], axis=0, keepdims=True)
                s_rows.append(s + jnp.where(blk == nb - 1, near_ref[h, 0, pg:pg + 1, :], cfar))
        s_own = jnp.sum(kn_ref[0, rows, :] * qb[rows, 0:T], axis=0, keepdims=True)
        b_own = jnp.zeros((1, T), F32)
        for tp in range(T):
            b_own = jnp.where(lane_t == tp, rb_ref[jnp.maximum(t - tp, 0) * ATT_HEADS + h], b_own)
        s_own = jnp.where(lane_t <= t, s_own + b_own, NEG)
        tiles.append(jnp.concatenate(s_rows + [jnp.concatenate([s_own, neg_tail], axis=1), neg_row], axis=0))

    ones8 = jnp.ones((SUBLANES, PAGE_SIZE), BF16)
    zero_tail = jnp.zeros((D, PAGE_SIZE - T), F32)
    outs = []
    for h, s in enumerate(tiles):
        rows = slice(h * D, (h + 1) * D)
        m = jnp.max(jnp.max(s, axis=1, keepdims=True), axis=0, keepdims=True)
        p = jnp.exp(s - m)
        l = jnp.sum(jnp.sum(p, axis=1, keepdims=True), axis=0, keepdims=True)
        acc = jnp.concatenate([vn_ref[0, rows, :] * p[_N_SLAB:_N_SLAB + 1, 0:T], zero_tail], axis=1)
        for n in range(_N_SLAB):
            acc = acc + vbuf[slot, h, n] * p[n:n + 1, :]
        o = None
        for part in _split3(acc):
            term = _dot_nt(ones8, part)
            o = term if o is None else o + term
        outs.append(o[0:1, :] / l)
    o_ref[0] = jnp.concatenate(outs, axis=1)


def _attend_sample(sel, page_table, rel_bias, q_col, k_newT, v_newT, near, pool_kt, pool_vt):
    b, _, T = k_newT.shape
    n_pages = page_table.shape[1]
    assert T <= RPE_BUCKETS // 2
    slab = (2, ATT_HEADS, _N_SLAB, ATT_HEAD_DIM, PAGE_SIZE)
    grid_spec = pltpu.PrefetchScalarGridSpec(
        num_scalar_prefetch=3,
        grid=(b, T),
        in_specs=[pl.BlockSpec((1, 1, ATT_WIDTH), lambda bi, t, *_: (bi * T + t, 0, 0)),
                  pl.BlockSpec((1, ATT_WIDTH, T), lambda bi, t, *_: (bi, 0, 0)),
                  pl.BlockSpec((1, ATT_WIDTH, T), lambda bi, t, *_: (bi, 0, 0)),
                  pl.BlockSpec((ATT_HEADS, 1, PAGES_PER_BLOCK, PAGE_SIZE), lambda bi, t, *_: (0, t, 0, 0)),
                  pl.BlockSpec(memory_space=pl.ANY),
                  pl.BlockSpec(memory_space=pl.ANY)],
        out_specs=pl.BlockSpec((1, 1, ATT_WIDTH), lambda bi, t, *_: (bi * T + t, 0, 0)),
        scratch_shapes=[pltpu.VMEM(slab, F32), pltpu.VMEM(slab, F32), pltpu.SemaphoreType.DMA((2, 2))],
    )
    return pl.pallas_call(
        functools.partial(_attend_sample_kernel, T=T, n_pages=n_pages),
        grid_spec=grid_spec,
        out_shape=jax.ShapeDtypeStruct((b * T, 1, ATT_WIDTH), F32),
        compiler_params=_cparams(("arbitrary", "arbitrary")),
        name="moba_attend_sample",
    )(sel, page_table.reshape(-1), rel_bias.reshape(-1), q_col, k_newT, v_newT, near, pool_kt, pool_vt)


def _prompt_layer(x, mem, norm_w, w_in, conv_w, conv_b, dt_bias, a_log, d_skip, ssm_norm_w,
                  mem_norm_w, w_mem_kv, w_out, rel_bias, final_norm_w):
    s = x.shape[0]
    W = MOBA_BLOCK
    z, xbc, gaT, qx, gx, kb3, km3, qT3, kT, vT, vT3, dtT = _in_proj_prompt(x, norm_w, w_in)
    y_ssd, state = _ssd_prompt(xbc, dtT, z, conv_w, conv_b, dt_bias, a_log, d_skip, ssm_norm_w)
    ki = np.arange(W)[:, None]
    qi = np.arange(W)[None, :]
    own_map = np.where(ki <= qi, _bucket_np(qi - ki), -1)
    d01 = _bias_table(rel_bias, np.stack([own_map, _bucket_np(W + qi - ki)]), LOG2E)
    y_attT = _moba_prompt(qT3, kb3, vT3, km3.reshape(-1, ATT_WIDTH), d01[:, 0], d01[:, 1], gaT, rel_bias)
    mk, mv = _memory_kv(mem, mem_norm_w, w_mem_kv)
    y_x = _xattn_prompt(qx, gx, mk, mv, tq=512)
    y = _out_proj(x, y_ssd, y_attT, None, y_x, w_out, final_norm_w, tm=512)
    conv_state = xbc[s - (CONV_W - 1):]
    heads = lambda aT: aT.reshape(ATT_HEADS, ATT_HEAD_DIM, s).transpose(2, 0, 1)
    return y, heads(kT), heads(vT), mk, mv, state, conv_state


def _sample_layer(x, cache_k, cache_v, page_table, mem_k, mem_v, state_ssm, state_conv, norm_w, w_in,
                  conv_w, conv_b, dt_bias, a_log, d_skip, ssm_norm_w, w_out, rel_bias, final_norm_w):
    b, T, d = x.shape
    n_pages = page_table.shape[1]
    assert (n_pages * PAGE_SIZE) % MOBA_BLOCK == 0 and n_pages % _KM_PAGES == 0
    assert CONV_W - 1 <= T <= SUBLANES
    xf = x.reshape(b * T, d)
    z, xbc, q, k, v, ga, qx, gx, dtT = _in_proj_sample(xf, norm_w, w_in)
    per = lambda a: a.reshape(b, T, a.shape[-1])
    y_ssd, h_new = _ssd_sample(per(xbc), state_conv, per(dtT.T), per(z), state_ssm,
                               conv_w, conv_b, dt_bias, a_log, d_skip, ssm_norm_w)
    pad8 = lambda a: jnp.pad(per(a), ((0, 0), (0, SUBLANES - T), (0, 0)))
    pool_kt = cache_k.transpose(0, 2, 3, 1)
    pool_vt = cache_v.transpose(0, 2, 3, 1)
    kmeanT = _kmean_sample(pool_kt, page_table)
    sel = _select_sample(pad8(q), kmeanT)[:, :, :T, :MOBA_TOPK]
    sel = sel.transpose(0, 2, 1, 3).reshape(-1)
    kk = (np.arange(PAGES_PER_BLOCK)[:, None] * PAGE_SIZE + np.arange(PAGE_SIZE)[None, :])[None]
    tt = np.arange(T)[:, None, None]
    near = _bias_table(rel_bias, _bucket_np(MOBA_BLOCK + tt - kk))
    q_col = (q * (ATT_HEAD_DIM ** -0.5)).reshape(b * T, 1, ATT_WIDTH)
    y_att = _attend_sample(sel, page_table, rel_bias, q_col, per(k).transpose(0, 2, 1),
                           per(v).transpose(0, 2, 1), near, pool_kt, pool_vt)
    y_att = y_att.reshape(b * T, ATT_WIDTH)
    y_x = _xattn_sample(pad8(qx), pad8(gx), mem_k, mem_v)
    y_x = y_x[:, :T].reshape(b * T, X_WIDTH)
    y = _out_proj(xf, y_ssd.reshape(b * T, SSM_WIDTH), y_att, ga, y_x, w_out, final_norm_w, tm=b * T)
    conv_state = per(xbc)[:, T - (CONV_W - 1):]
    return y.reshape(b, T, d), per(k), per(v), h_new, conv_state


def kernel(x_prompt, x_sample, mem_prompt, cache_k, cache_v, page_table, cache_mem_k, cache_mem_v,
           state_ssm, state_conv, norm_w, w_in, conv_w, conv_b, dt_bias, a_log, d_skip, ssm_norm_w,
           mem_norm_w, w_mem_kv, w_out, rel_bias, final_norm_w):
    bp, s, d = x_prompt.shape
    bs, T, _ = x_sample.shape
    depth = w_in.shape[0]
    assert bp == 1 and depth == 1
    l = 0
    y_p, k_p, v_p, mk, mv, ssm_p, conv_p = _prompt_layer(
        x_prompt[0], mem_prompt[0], norm_w[l], w_in[l], conv_w[l], conv_b[l], dt_bias[l], a_log[l],
        d_skip[l], ssm_norm_w[l], mem_norm_w[l], w_mem_kv[l], w_out[l], rel_bias, final_norm_w)
    y_s, k_s, v_s, ssm_s, conv_s = _sample_layer(
        x_sample, cache_k[l], cache_v[l], page_table, cache_mem_k[l], cache_mem_v[l], state_ssm[l],
        state_conv[l], norm_w[l], w_in[l], conv_w[l], conv_b[l], dt_bias[l], a_log[l], d_skip[l],
        ssm_norm_w[l], w_out[l], rel_bias, final_norm_w)
    nm = mem_prompt.shape[1]
    return (y_p[None],
            y_s,
            k_p[None, None],
            v_p[None, None],
            mk.reshape(1, 1, nm, X_HEADS, X_HEAD_DIM),
            mv.reshape(1, 1, nm, X_HEADS, X_HEAD_DIM),
            ssm_p[None, None],
            conv_p[None, None],
            k_s.reshape(1, bs, T, ATT_HEADS, ATT_HEAD_DIM),
            v_s.reshape(1, bs, T, ATT_HEADS, ATT_HEAD_DIM),
            ssm_s[None],
            conv_s[None])
```

```python
import functools
import math

import numpy as np
import jax
import jax.numpy as jnp
from jax import lax
from jax.experimental import pallas as pl
from jax.experimental.pallas import tpu as pltpu

F32 = jnp.float32
BF16 = jnp.bfloat16
EPS = 1e-6

SSM_HEADS = 16
SSM_HEAD_DIM = 64
SSM_GROUPS = 2
SSM_HG = SSM_HEADS // SSM_GROUPS
SSM_STATE = 128
SSM_WIDTH = SSM_HEADS * SSM_HEAD_DIM
SSM_GW = SSM_WIDTH // SSM_GROUPS
SSM_CHUNK = 128
CONV_W = 4
CONV_DIM = SSM_WIDTH + 2 * SSM_GROUPS * SSM_STATE
ATT_HEADS = 8
ATT_HEAD_DIM = 64
ATT_WIDTH = ATT_HEADS * ATT_HEAD_DIM
MOBA_BLOCK = 256
MOBA_TOPK = 3
PAGE_SIZE = 128
PAGES_PER_BLOCK = MOBA_BLOCK // PAGE_SIZE
RPE_BUCKETS = 32
RPE_MAX_DIST = 128
X_HEADS = 4
X_HEAD_DIM = 128
X_WIDTH = X_HEADS * X_HEAD_DIM

LANES = 128
SUBLANES = 8
VMEM_LIMIT = 56 * 1024 * 1024
NEG = -1e30
BIG = 2.0 ** 100
LOG2E = math.log2(math.e)


def _cparams(sem, vmem=VMEM_LIMIT):
    return pltpu.CompilerParams(dimension_semantics=sem, vmem_limit_bytes=vmem)


def _split2(x):
    hi = x.astype(BF16)
    lo = (x - hi.astype(F32)).astype(BF16)
    return hi, lo


def _split3(x):
    hi = x.astype(BF16)
    r = x - hi.astype(F32)
    mid = r.astype(BF16)
    lo = (r - mid.astype(F32)).astype(BF16)
    return hi, mid, lo


def _dot(a, b):
    return jnp.dot(a, b, preferred_element_type=F32)


def _dot_nt(a, b):
    return lax.dot_general(a, b, (((1,), (1,)), ((), ())), preferred_element_type=F32)


def _dot_tn(a, b):
    return lax.dot_general(a, b, (((0,), (0,)), ((), ())), preferred_element_type=F32)


def _dot_f32_exactrhs(a, b_bf16, passes=3):
    parts = _split3(a) if passes == 3 else _split2(a)
    out = _dot(parts[0], b_bf16)
    for p in parts[1:]:
        out = out + _dot(p, b_bf16)
    return out


def _dot_hi(a, b):
    ah, al = _split2(a)
    bh, bl = _split2(b)
    return _dot(ah, bh) + _dot(al, bh) + _dot(ah, bl)


def _silu(x):
    return x / (1.0 + jnp.exp(-x))


def _softplus(x):
    return jnp.maximum(x, 0.0) + jnp.log1p(jnp.exp(-jnp.abs(x)))


def _rms(x, g):
    ms = jnp.mean(x * x, axis=-1, keepdims=True)
    return (x * lax.rsqrt(ms + EPS)) * g


_MAIN_SPLITS = (SSM_WIDTH, CONV_DIM, ATT_WIDTH, ATT_WIDTH, ATT_WIDTH, ATT_WIDTH, X_WIDTH, X_WIDTH)


_W_ROW_CHUNK = 512


def _inproj_kernel(x_ref, nw_ref, wT_ref, z_ref, xbc_ref, q_ref, k_ref, v_ref, ga_ref, qx_ref, gx_ref, dtT_ref):
    xn = _rms(x_ref[...], nw_ref[...])
    xh, xl = _split2(xn)

    def proj(lo, width):
        wh, wl = _split2(wT_ref[lo:lo + width, :])
        return _dot_nt(xh, wh) + _dot_nt(xl, wh) + _dot_nt(xh, wl)

    outs = (z_ref, xbc_ref, None, q_ref, k_ref, v_ref, ga_ref, qx_ref, gx_ref)
    lo = 0
    for o_ref, width in zip(outs, _IN_SPLITS):
        if o_ref is None:
            wh, wl = _split2(wT_ref[lo:lo + width, :])
            dtT_ref[...] = _dot_nt(wh, xh) + _dot_nt(wh, xl) + _dot_nt(wl, xh)
        else:
            for c in range(0, width, _W_ROW_CHUNK):
                o_ref[:, c:c + _W_ROW_CHUNK] = proj(lo + c, _W_ROW_CHUNK)
        lo += width


_IN_SPLITS = (SSM_WIDTH, CONV_DIM, SSM_HEADS, ATT_WIDTH, ATT_WIDTH, ATT_WIDTH, ATT_WIDTH, X_WIDTH, X_WIDTH)


def _in_proj_sample(x, norm_w, w_in):
    m, d = x.shape
    wT = w_in.T
    full = lambda a: pl.BlockSpec(a.shape, lambda: (0,) * a.ndim)
    out_shapes = [jax.ShapeDtypeStruct((m, w), F32) for w in _MAIN_SPLITS]
    out_shapes += [jax.ShapeDtypeStruct((SSM_HEADS, m), F32)]
    nw = norm_w.reshape(1, d)
    return pl.pallas_call(
        _inproj_kernel,
        in_specs=[full(x), full(nw), full(wT)],
        out_specs=[pl.BlockSpec(s.shape, lambda: (0, 0)) for s in out_shapes],
        out_shape=out_shapes,
        compiler_params=pltpu.CompilerParams(vmem_limit_bytes=VMEM_LIMIT),
        name="in_proj_sample",
    )(x, nw, wT)


_PROMPT_NN = (SSM_WIDTH, CONV_DIM, X_WIDTH, X_WIDTH, ATT_WIDTH)


def _inproj_prompt_kernel(x_ref, nw_ref, wn_ref, wt_ref, z_ref, xbc_ref, gaT_ref, qx_ref, gx_ref, kb_ref,
                          km_ref, qT_ref, kT_ref, vT_ref, vTb_ref, dtT_ref):
    xh = _rms(x_ref[...], nw_ref[...]).astype(BF16)
    lo = 0
    for o_ref, width in zip((z_ref, xbc_ref, qx_ref, gx_ref), _PROMPT_NN[:-1]):
        val = _dot(xh, wn_ref[:, lo:lo + width])
        if o_ref is qx_ref:
            val = (val * (X_HEAD_DIM ** -0.5)).astype(BF16)
        o_ref[...] = val
        lo += width
    k = _dot(xh, wn_ref[:, lo:lo + ATT_WIDTH])
    kb_ref[0] = k.astype(BF16)
    km_ref[0] = jnp.mean(k, axis=0, keepdims=True)
    qT_ref[0] = _dot_nt(wt_ref[0:ATT_WIDTH, :], xh)
    kT_ref[...] = _dot_nt(wt_ref[ATT_WIDTH:2 * ATT_WIDTH, :], xh)
    vT = _dot_nt(wt_ref[2 * ATT_WIDTH:3 * ATT_WIDTH, :], xh)
    vT_ref[...] = vT
    vTb_ref[0] = vT.astype(BF16)
    gaT_ref[...] = _dot_nt(wt_ref[3 * ATT_WIDTH:4 * ATT_WIDTH, :], xh)
    dtT_ref[...] = _dot_nt(wt_ref[4 * ATT_WIDTH:, :], xh)


def _in_proj_prompt(x, norm_w, w_in):
    m, d = x.shape
    tm = MOBA_BLOCK
    nb = m // tm
    c_z, c_dt = 0, SSM_WIDTH + CONV_DIM
    c_q = c_dt + SSM_HEADS
    c_k, c_qx = c_q + ATT_WIDTH, c_q + 4 * ATT_WIDTH
    cols = lambda lo, w: w_in[:, lo:lo + w]
    wn = jnp.concatenate([cols(c_z, SSM_WIDTH + CONV_DIM), cols(c_qx, 2 * X_WIDTH),
                          cols(c_k, ATT_WIDTH)], axis=1).astype(BF16)
    wt = jnp.concatenate([cols(c_q, 4 * ATT_WIDTH), cols(c_dt, SSM_HEADS)], axis=1).T.astype(BF16)
    nw = norm_w.reshape(1, d)
    row = lambda w: pl.BlockSpec((tm, w), lambda i: (i, 0))
    col = lambda h: pl.BlockSpec((h, tm), lambda i: (0, i))
    full = lambda a: pl.BlockSpec(a.shape, lambda i: (0,) * a.ndim)
    tile3 = lambda a, b: pl.BlockSpec((1, a, b), lambda i: (i, 0, 0))
    out_specs = [row(SSM_WIDTH), row(CONV_DIM), col(ATT_WIDTH), row(X_WIDTH), row(X_WIDTH),
                 tile3(tm, ATT_WIDTH), tile3(1, ATT_WIDTH), tile3(ATT_WIDTH, tm),
                 col(ATT_WIDTH), col(ATT_WIDTH), tile3(ATT_WIDTH, tm), col(SSM_HEADS)]
    sds = jax.ShapeDtypeStruct
    out_shape = [sds((m, SSM_WIDTH), F32), sds((m, CONV_DIM), F32), sds((ATT_WIDTH, m), F32),
                 sds((m, X_WIDTH), BF16), sds((m, X_WIDTH), F32),
                 sds((nb, tm, ATT_WIDTH), BF16), sds((nb, 1, ATT_WIDTH), F32), sds((nb, ATT_WIDTH, tm), F32),
                 sds((ATT_WIDTH, m), F32), sds((ATT_WIDTH, m), F32), sds((nb, ATT_WIDTH, tm), BF16),
                 sds((SSM_HEADS, m), F32)]
    return pl.pallas_call(
        _inproj_prompt_kernel,
        grid=(nb,),
        in_specs=[row(d), full(nw), full(wn), full(wt)],
        out_specs=out_specs,
        out_shape=out_shape,
        compiler_params=_cparams(("parallel",)),
        name="in_proj_prompt",
    )(x, nw, wn, wt)


_SSD_CHUNKS = 2
def _ssd_kernel(xbc_ref, dtT_ref, z_ref, convw_ref, convb_ref, dtb_ref, alog_ref, dskip_ref, nw_ref,
                e_ref, y_ref, hT_ref, ext_sc, h_sc):
    c = pl.program_id(0)
    L = SSM_CHUNK

    @pl.when(c == 0)
    def _():
        ext_sc[0:SUBLANES, :] = jnp.zeros((SUBLANES, CONV_DIM), F32)
        h_sc[...] = jnp.zeros_like(h_sc)

    for sub in range(_SSD_CHUNKS):
        rows = slice(sub * L, (sub + 1) * L)
        x = xbc_ref[rows, :]
        ext_sc[SUBLANES:SUBLANES + L, :] = x
        w = convw_ref[...]
        acc = x * w[CONV_W - 1:CONV_W, :] + convb_ref[...]
        for k in range(1, CONV_W):
            acc = acc + ext_sc[SUBLANES - k:SUBLANES - k + L, :] * w[CONV_W - 1 - k:CONV_W - k, :]
        ext_sc[0:SUBLANES, :] = x[L - SUBLANES:L]
        u = _silu(acc)
        xs = u[:, :SSM_WIDTH]
        bm = u[:, SSM_WIDTH:SSM_WIDTH + SSM_GROUPS * SSM_STATE]
        cm = u[:, SSM_WIDTH + SSM_GROUPS * SSM_STATE:]

        dtT = _softplus(dtT_ref[:, rows] + dtb_ref[...])
        aT = dtT * (-jnp.exp(alog_ref[...]))
        r_i = lax.broadcasted_iota(jnp.int32, (L, L), 0)
        c_i = lax.broadcasted_iota(jnp.int32, (L, L), 1)
        upper = (r_i <= c_i).astype(BF16)
        acumT = _dot_f32_exactrhs(aT, upper)
        a_last = acumT[:, L - 1:L]
        to_endT = jnp.exp(a_last - acumT) * dtT
        eacT = jnp.exp(acumT)
        stack = jnp.concatenate(
            [acumT, dtT, to_endT, eacT, jnp.zeros((L - 4 * SSM_HEADS, L), F32)], axis=0)
        st = stack.T
        e = e_ref[...]
        dt_e = _dot_f32_exactrhs(st[:, SSM_HEADS:2 * SSM_HEADS], e, 2)
        toend_e = _dot_f32_exactrhs(st[:, 2 * SSM_HEADS:3 * SSM_HEADS], e, 2)
        eac_e = _dot_f32_exactrhs(st[:, 3 * SSM_HEADS:4 * SSM_HEADS], e, 2)
        cd_e = eac_e[L - 1:L, :]

        xd = (xs * dt_e).astype(BF16)
        xw = (xs * toend_e).astype(BF16)
        causal = r_i >= c_i
        ys = []
        yoffs = []
        for g in range(SSM_GROUPS):
            bg = bm[:, g * SSM_STATE:(g + 1) * SSM_STATE].astype(BF16)
            cg = cm[:, g * SSM_STATE:(g + 1) * SSM_STATE].astype(BF16)
            cb = jnp.where(causal, _dot_nt(cg, bg), 0.0)
            for hh in range(SSM_HG):
                h = g * SSM_HG + hh
                seg = st[:, h:h + 1] - acumT[h:h + 1, :]
                mix = (cb * jnp.exp(jnp.minimum(seg, 0.0))).astype(BF16)
                ys.append(_dot(mix, xd[:, h * SSM_HEAD_DIM:(h + 1) * SSM_HEAD_DIM]))
            hprev = h_sc[g]
            yoffs.append(_dot(cg, hprev.astype(BF16)))
            s_new = _dot_tn(bg, xw[:, g * SSM_GW:(g + 1) * SSM_GW])
            h_sc[g] = hprev * cd_e[:, g * SSM_GW:(g + 1) * SSM_GW] + s_new

        y = jnp.concatenate(ys, axis=1) + jnp.concatenate(yoffs, axis=1) * eac_e + dskip_ref[...] * xs
        gt = y * _silu(z_ref[rows, :])
        outs = []
        for g in range(SSM_GROUPS):
            gg = gt[:, g * SSM_GW:(g + 1) * SSM_GW]
            outs.append(gg * lax.rsqrt(jnp.mean(gg * gg, axis=-1, keepdims=True) + EPS))
        y_ref[rows, :] = (jnp.concatenate(outs, axis=1) * nw_ref[...]).astype(y_ref.dtype)

    @pl.when(c == pl.num_programs(0) - 1)
    def _():
        hT_ref[...] = h_sc[...]


def _head_expand():
    return jnp.asarray(np.repeat(np.eye(SSM_HEADS, dtype=np.float32), SSM_HEAD_DIM, axis=1), BF16)


def _ssd_prompt(xbc, dtT, z, conv_w, conv_b, dt_bias, a_log, d_skip, ssm_norm_w):
    m = xbc.shape[0]
    L = SSM_CHUNK
    step = L * _SSD_CHUNKS
    row = lambda w: pl.BlockSpec((step, w), lambda i: (i, 0))
    full = lambda a: pl.BlockSpec(a.shape, lambda i: (0,) * a.ndim)
    convb = conv_b.reshape(1, CONV_DIM)
    dtb = dt_bias.reshape(SSM_HEADS, 1)
    alog = a_log.reshape(SSM_HEADS, 1)
    dskip = jnp.repeat(d_skip, SSM_HEAD_DIM).reshape(1, SSM_WIDTH)
    nw = ssm_norm_w.reshape(1, SSM_WIDTH)
    e = _head_expand()
    y, hT = pl.pallas_call(
        _ssd_kernel,
        grid=(m // step,),
        in_specs=[row(CONV_DIM), pl.BlockSpec((SSM_HEADS, step), lambda i: (0, i)), row(SSM_WIDTH),
                  full(conv_w), full(convb), full(dtb), full(alog), full(dskip), full(nw), full(e)],
        out_specs=[row(SSM_WIDTH), pl.BlockSpec((SSM_GROUPS, SSM_STATE, SSM_GW), lambda i: (0, 0, 0))],
        out_shape=[jax.ShapeDtypeStruct((m, SSM_WIDTH), BF16),
                   jax.ShapeDtypeStruct((SSM_GROUPS, SSM_STATE, SSM_GW), F32)],
        scratch_shapes=[pltpu.VMEM((SUBLANES + L, CONV_DIM), F32),
                        pltpu.VMEM((SSM_GROUPS, SSM_STATE, SSM_GW), F32)],
        compiler_params=_cparams(("arbitrary",)),
        name="ssd_prompt",
    )(xbc, dtT, z, conv_w, convb, dtb, alog, dskip, nw, e)
    state = hT.reshape(SSM_GROUPS, SSM_STATE, SSM_HG, SSM_HEAD_DIM).transpose(0, 2, 3, 1)
    return y, state.reshape(SSM_HEADS, SSM_HEAD_DIM, SSM_STATE)


def _bucket_np(dist):
    n = np.maximum(dist, 0)
    exact = RPE_BUCKETS // 2
    nf = np.maximum(n, 1).astype(np.float32)
    scaled = (np.log(nf / np.float32(exact)) / np.float32(math.log(RPE_MAX_DIST / exact))
              * np.float32(RPE_BUCKETS - exact))
    large = exact + scaled.astype(np.int32)
    return np.where(n < exact, n, np.minimum(large, RPE_BUCKETS - 1)).astype(np.int32)


def _bias_kernel(rb_ref, map_ref, out_ref, *, scale):
    h = pl.program_id(0)
    bmap = map_ref[...]
    acc = jnp.full(bmap.shape, -2.0 * BIG, F32)
    for b in range(RPE_BUCKETS):
        acc = jnp.where(bmap == b, rb_ref[b * ATT_HEADS + h] * scale, acc)
    out_ref[0] = acc


def _bias_table(rel_bias, bucket_map, scale=1.0):
    bmap = jnp.asarray(bucket_map, jnp.int32)
    nd = bmap.ndim
    return pl.pallas_call(
        functools.partial(_bias_kernel, scale=scale),
        grid=(ATT_HEADS,),
        in_specs=[pl.BlockSpec(memory_space=pltpu.SMEM),
                  pl.BlockSpec(bmap.shape, lambda h: (0,) * nd)],
        out_specs=pl.BlockSpec((1,) + bmap.shape, lambda h: (h,) + (0,) * nd),
        out_shape=jax.ShapeDtypeStruct((ATT_HEADS,) + bmap.shape, F32),
        compiler_params=_cparams(("parallel",)),
        name="rel_bias_table",
    )(rel_bias.reshape(-1), bmap)


def _topk_hits(s, blk, axis):
    picks = []
    for _ in range(MOBA_TOPK):
        mx = jnp.max(s, axis=axis, keepdims=True)
        idx = jnp.min(jnp.where(s == mx, blk, float(2 * LANES)), axis=axis, keepdims=True)
        picks.append((idx, jnp.where(mx > -jnp.inf, 1.0, 0.0)))
        s = jnp.where(blk == idx, -jnp.inf, s)
    return picks


_HPS = 8
_K_HEADS = 4
_SPLIT = 2
_ONES_ROWS = 16


def _moba_kernel(cfar_ref, qT_ref, kb_ref, vT_ref, km_ref, d0_ref, d1_ref, gaT_ref, o_ref,
                 pen_sc, acc_sc, m_sc):
    hp = pl.program_id(0)
    i = pl.program_id(1)
    W = MOBA_BLOCK
    D = ATT_HEAD_DIM
    nb = km_ref.shape[0]
    blk = lax.broadcasted_iota(jnp.int32, (nb, W), 0).astype(F32)
    i_f = i.astype(F32)
    near = jnp.maximum(i - 1, 0)
    q_augs, cfars = [], []
    acc_sc[...] = jnp.zeros_like(acc_sc)
    for hh in range(_HPS):
        rows = slice(hh * D, (hh + 1) * D)
        cfar2 = cfar_ref[hp * _HPS + hh] * LOG2E
        qT = qT_ref[0, rows, :]
        sT = _dot_hi(km_ref[:, rows], qT)
        sT = jnp.where(blk < i_f, sT, -jnp.inf)
        hit = jnp.zeros((nb, W), F32)
        for idx, valid in _topk_hits(sT, blk, 0):
            hit = jnp.maximum(hit, jnp.where(blk == idx, valid, 0.0))
        pen_sc[hh] = jnp.where(hit > 0.0, cfar2, -2.0 * BIG)

        qs = (qT * (D ** -0.5 * LOG2E)).astype(BF16)
        parts = [jnp.zeros((D, W), BF16)] * _K_HEADS
        parts[hh % _K_HEADS] = qs
        q_augs.append(jnp.concatenate(parts, axis=0))
        cfars.append(cfar2)

    m_sc[...] = jnp.full(m_sc.shape, -BIG, F32)
    ones = jnp.ones((_ONES_ROWS, W), BF16)

    def scores(hh, j, bias_ref):
        g = hh // _K_HEADS
        lanes = slice(g * _K_HEADS * D, (g + 1) * _K_HEADS * D)
        s = _dot(kb_ref[j, :, lanes], q_augs[hh])
        return s if bias_ref is None else s + bias_ref[hh]

    def absorb(chains, ss):
        ps = []
        for (hh, a, j, _, pen), s in zip(chains, ss):
            m = m_sc[hh, a, 0:1, :]
            m_new = jnp.maximum(m, jnp.max(s, axis=0, keepdims=True) + pen)
            m_sc[hh, a, 0:1, :] = m_new
            ps.append((jnp.exp2(m - m_new), jnp.exp2(s - (m_new - pen)).astype(BF16)))
        for (hh, a, j, _, _), (alpha, p) in zip(chains, ps):
            v1 = jnp.concatenate([vT_ref[j, hh * D:(hh + 1) * D, :], ones], axis=0)
            acc_sc[hh, a] = alpha * acc_sc[hh, a] + _dot(v1, p)

    def group(chains):
        absorb(chains, [scores(hh, j, bias_ref) for hh, _, j, bias_ref, _ in chains])

    zero_row = jnp.zeros((1, W), F32)
    group([(hh, 0, i, d0_ref, zero_row) for hh in range(_HPS)]
          + [(hh, 1, near, d1_ref, pen_sc[hh, pl.ds(near, 1), :] - cfars[hh]) for hh in range(_HPS)])

    def far_chains(first, count):
        return [(hh, a, first + a, None, pen_sc[hh, pl.ds(first + a, 1), :])
                for a in range(count) for hh in range(_HPS)]

    def far_group(first, count):
        group(far_chains(first, count))

    def far_body(trip, carry):
        far_group(_SPLIT * trip, _SPLIT)
        return carry

    lax.fori_loop(0, near // _SPLIT, far_body, 0)
    done = (near // _SPLIT) * _SPLIT
    count = _SPLIT // 2
    while count >= 1:
        take = (near & count) != 0

        @pl.when(take)
        def _(done=done, count=count):
            far_group(done, count)

        done = done + jnp.where(take, count, 0)
        count //= 2

    outs = []
    for hh in range(_HPS):
        m = m_sc[hh, 0, 0:1, :]
        for a in range(1, _SPLIT):
            m = jnp.maximum(m, m_sc[hh, a, 0:1, :])
        acc = jnp.zeros((D + _ONES_ROWS, W), F32)
        for a in range(_SPLIT):
            acc = acc + jnp.exp2(m_sc[hh, a, 0:1, :] - m) * acc_sc[hh, a]
        outs.append(acc[:D] / acc[D:D + 1])
    o_ref[...] = (jnp.concatenate(outs, axis=0) * _silu(gaT_ref[...])).astype(o_ref.dtype)


def _moba_prompt(qT3, kb3, vT3, kmean, d0T, d1T, gaT, rel_bias):
    nb = qT3.shape[0]
    W = MOBA_BLOCK
    HW = _HPS * ATT_HEAD_DIM
    assert nb - 1 >= MOBA_TOPK and nb <= LANES and ATT_HEADS % _HPS == 0 and HW % LANES == 0
    cfar = rel_bias[RPE_BUCKETS - 1]
    return pl.pallas_call(
        _moba_kernel,
        grid=(ATT_HEADS // _HPS, nb),
        in_specs=[pl.BlockSpec(memory_space=pltpu.SMEM),
                  pl.BlockSpec((1, HW, W), lambda hp, i: (i, hp, 0)),
                  pl.BlockSpec((nb, W, HW), lambda hp, i: (0, 0, hp), pipeline_mode=pl.Buffered(1)),
                  pl.BlockSpec((nb, HW, W), lambda hp, i: (0, hp, 0), pipeline_mode=pl.Buffered(1)),
                  pl.BlockSpec((nb, HW), lambda hp, i: (0, hp)),
                  pl.BlockSpec((_HPS, W, W), lambda hp, i: (hp, 0, 0)),
                  pl.BlockSpec((_HPS, W, W), lambda hp, i: (hp, 0, 0)),
                  pl.BlockSpec((HW, W), lambda hp, i: (hp, i))],
        out_specs=pl.BlockSpec((HW, W), lambda hp, i: (hp, i)),
        out_shape=jax.ShapeDtypeStruct((ATT_WIDTH, nb * W), BF16),
        scratch_shapes=[pltpu.VMEM((_HPS, nb, W), F32),
                        pltpu.VMEM((_HPS, _SPLIT, ATT_HEAD_DIM + _ONES_ROWS, W), F32),
                        pltpu.VMEM((_HPS, _SPLIT, SUBLANES, W), F32)],
        compiler_params=_cparams(("parallel", "arbitrary")),
        name="moba_prompt",
    )(cfar, qT3, kb3, vT3, kmean, d0T, d1T, gaT)


def _memkv_kernel(mem_ref, g_ref, w_ref, k_ref, v_ref):
    xn = _rms(mem_ref[...], g_ref[...]).astype(BF16)
    kv = _dot(xn, w_ref[...])
    k_ref[...] = kv[:, :X_WIDTH]
    v_ref[...] = kv[:, X_WIDTH:]


def _memory_kv(mem, g, w):
    nm, d = mem.shape
    full = lambda a: pl.BlockSpec(a.shape, lambda: (0,) * a.ndim)
    g2 = g.reshape(1, d)
    wb = w.astype(BF16)
    return pl.pallas_call(
        _memkv_kernel,
        in_specs=[full(mem), full(g2), full(wb)],
        out_specs=[pl.BlockSpec((nm, X_WIDTH), lambda: (0, 0))] * 2,
        out_shape=[jax.ShapeDtypeStruct((nm, X_WIDTH), F32)] * 2,
        compiler_params=pltpu.CompilerParams(vmem_limit_bytes=VMEM_LIMIT),
        name="memory_kv",
    )(mem, g2, wb)


def _xattn_kernel(q_ref, gx_ref, mk_ref, mv_ref, o_ref):
    q = q_ref[...]
    mk = mk_ref[...].astype(BF16)
    mv = mv_ref[...].astype(BF16)
    outs = []
    for h in range(X_HEADS):
        sl = slice(h * X_HEAD_DIM, (h + 1) * X_HEAD_DIM)
        s = _dot_nt(q[:, sl], mk[:, sl])
        m = jnp.max(s, axis=1, keepdims=True)
        p = jnp.exp(s - m)
        l = jnp.sum(p, axis=1, keepdims=True)
        outs.append(_dot(p.astype(BF16), mv[:, sl]) / l)
    o_ref[...] = (jnp.concatenate(outs, axis=1) * _silu(gx_ref[...])).astype(o_ref.dtype)


def _xattn_prompt(qx, gx, mk, mv, *, tq):
    m = qx.shape[0]
    row = pl.BlockSpec((tq, X_WIDTH), lambda i: (i, 0))
    full = pl.BlockSpec(mk.shape, lambda i: (0, 0))
    return pl.pallas_call(
        _xattn_kernel,
        grid=(m // tq,),
        in_specs=[row, row, full, full],
        out_specs=row,
        out_shape=jax.ShapeDtypeStruct((m, X_WIDTH), BF16),
        compiler_params=_cparams(("parallel",)),
        name="xattn_prompt",
    )(qx, gx, mk, mv)


def _xattn_sample_kernel(q_ref, gx_ref, mk_ref, mv_ref, o_ref):
    q = q_ref[0]
    outs = []
    for h in range(X_HEADS):
        sl = slice(h * X_HEAD_DIM, (h + 1) * X_HEAD_DIM)
        s = _dot_nt((q[:, sl] * (X_HEAD_DIM ** -0.5)).astype(BF16), mk_ref[0, :, h, :].astype(BF16))
        m = jnp.max(s, axis=1, keepdims=True)
        p = jnp.exp(s - m)
        l = jnp.sum(p, axis=1, keepdims=True)
        outs.append(_dot(p.astype(BF16), mv_ref[0, :, h, :].astype(BF16)) / l)
    o_ref[0] = jnp.concatenate(outs, axis=1) * _silu(gx_ref[0])


def _xattn_sample(qx, gx, mk, mv):
    b, t8, _ = qx.shape
    nm = mk.shape[1]
    qs = pl.BlockSpec((1, t8, X_WIDTH), lambda i: (i, 0, 0))
    ms = pl.BlockSpec((1, nm, X_HEADS, X_HEAD_DIM), lambda i: (i, 0, 0, 0))
    return pl.pallas_call(
        _xattn_sample_kernel,
        grid=(b,),
        in_specs=[qs, qs, ms, ms],
        out_specs=qs,
        out_shape=jax.ShapeDtypeStruct((b, t8, X_WIDTH), F32),
        compiler_params=_cparams(("parallel",)),
        name="xattn_sample",
    )(qx, gx, mk, mv)


def _outproj_tail(x_ref, ys_ref, att_proj, yx_ref, w_ref, fw_ref, o_ref):
    h = x_ref[...] + _dot(ys_ref[...].astype(BF16), w_ref[0:SSM_WIDTH, :])
    h = h + att_proj
    h = h + _dot(yx_ref[...].astype(BF16), w_ref[SSM_WIDTH + ATT_WIDTH:, :])
    o_ref[...] = _rms(h, fw_ref[...])


def _outproj_kernel(x_ref, ys_ref, ya_ref, ga_ref, yx_ref, w_ref, fw_ref, o_ref):
    att = (ya_ref[...] * _silu(ga_ref[...])).astype(BF16)
    att_proj = _dot(att, w_ref[SSM_WIDTH:SSM_WIDTH + ATT_WIDTH, :])
    _outproj_tail(x_ref, ys_ref, att_proj, yx_ref, w_ref, fw_ref, o_ref)


def _outproj_t_kernel(x_ref, ys_ref, yaT_ref, yx_ref, w_ref, fw_ref, o_ref):
    att_proj = _dot_tn(yaT_ref[...].astype(BF16), w_ref[SSM_WIDTH:SSM_WIDTH + ATT_WIDTH, :])
    _outproj_tail(x_ref, ys_ref, att_proj, yx_ref, w_ref, fw_ref, o_ref)


def _out_proj(x, y_ssd, y_att, ga, y_x, w_out, final_norm_w, *, tm):
    m, d = x.shape
    row = lambda w: pl.BlockSpec((tm, w), lambda i: (i, 0))
    full = lambda a: pl.BlockSpec(a.shape, lambda i: (0,) * a.ndim)
    wb = w_out.astype(BF16)
    fw = final_norm_w.reshape(1, d)
    if ga is None:
        body, att_args = _outproj_t_kernel, (y_att,)
        att_specs = [pl.BlockSpec((ATT_WIDTH, tm), lambda i: (0, i))]
    else:
        body, att_args = _outproj_kernel, (y_att, ga)
        att_specs = [row(ATT_WIDTH), row(ATT_WIDTH)]
    return pl.pallas_call(
        body,
        grid=(m // tm,),
        in_specs=[row(d), row(SSM_WIDTH)] + att_specs + [row(X_WIDTH), full(wb), full(fw)],
        out_specs=row(d),
        out_shape=jax.ShapeDtypeStruct((m, d), F32),
        compiler_params=_cparams(("parallel",)),
        name="out_proj",
    )(x, y_ssd, *att_args, y_x, wb, fw)


def _ssd_sample_kernel(x_ref, c_ref, dt_ref, z_ref, h0_ref, w_ref, b_ref, dtb_ref, alog_ref, dsk_ref,
                       nw_ref, sel_ref, y_ref, h_ref, y_sc, *, T):
    P = SSM_HEAD_DIM
    N = SSM_STATE
    cr = c_ref[0]
    xr = x_ref[0]
    ext = lambda i: cr[i:i + 1, :] if i < CONV_W - 1 else xr[i - (CONV_W - 1):i - (CONV_W - 2), :]
    w = w_ref[...]
    rows = []
    for t in range(T):
        u = b_ref[...]
        for j in range(CONV_W):
            u = u + w[j:j + 1, :] * ext(t + j)
        rows.append(u)
    u = _silu(jnp.concatenate(rows + [jnp.zeros((SUBLANES - T, CONV_DIM), F32)], axis=0))
    xs = u[:, :SSM_WIDTH]
    dt = _softplus(dt_ref[0] + dtb_ref[...])
    dec = jnp.exp(dt * (-jnp.exp(alog_ref[...])))
    xdt = xs[0:T] * dt
    sel = sel_ref[...]
    bc = [_dot_tn_exactrhs(u[:, SSM_WIDTH + k * N:SSM_WIDTH + (k + 1) * N], sel)
          for k in range(2 * SSM_GROUPS)]
    for pair in range(SSM_HEADS // 2):
        g = (2 * pair) // SSM_HG
        lanes = slice(pair * 2 * P, (pair + 1) * 2 * P)
        hT = jnp.concatenate([h0_ref[0, 2 * pair], h0_ref[0, 2 * pair + 1]], axis=0).T
        for t in range(T):
            tl = slice(t * N, (t + 1) * N)
            hT = hT * dec[t:t + 1, lanes] + bc[g][:, tl] * xdt[t:t + 1, lanes]
            y_sc[t:t + 1, lanes] = jnp.sum(hT * bc[SSM_GROUPS + g][:, tl], axis=0, keepdims=True)
        h2 = hT.T
        h_ref[0, 2 * pair] = h2[:P]
        h_ref[0, 2 * pair + 1] = h2[P:]
    y = y_sc[0:T, :] + dsk_ref[...] * xs[0:T]
    gt = y * _silu(z_ref[0])
    outs = []
    for g in range(SSM_GROUPS):
        gg = gt[:, g * SSM_GW:(g + 1) * SSM_GW]
        outs.append(gg * lax.rsqrt(jnp.mean(gg * gg, axis=-1, keepdims=True) + EPS))
    y_ref[0] = jnp.concatenate(outs, axis=1) * nw_ref[...]


def _dot_tn_exactrhs(a, b_bf16):
    out = None
    for part in _split3(a):
        term = _dot_tn(part, b_bf16)
        out = term if out is None else out + term
    return out


def _ssd_sample(xbc, state_conv, dt_raw, z, state_ssm, conv_w, conv_b, dt_bias, a_log, d_skip, ssm_norm_w):
    b, T, _ = xbc.shape
    per = lambda a: pl.BlockSpec((1,) + a.shape[1:], lambda i: (i,) + (0,) * (a.ndim - 1))
    full = lambda a: pl.BlockSpec(a.shape, lambda i: (0,) * a.ndim)
    per_head = lambda v: jnp.repeat(v, SSM_HEAD_DIM, axis=-1)
    dt_e = per_head(dt_raw)
    br = conv_b.reshape(1, CONV_DIM)
    dtb = per_head(dt_bias).reshape(1, SSM_WIDTH)
    alog = per_head(a_log).reshape(1, SSM_WIDTH)
    dsk = per_head(d_skip).reshape(1, SSM_WIDTH)
    nw = ssm_norm_w.reshape(1, SSM_WIDTH)
    sel = np.zeros((SUBLANES, T * SSM_STATE), np.float32)
    for t in range(T):
        sel[t, t * SSM_STATE:(t + 1) * SSM_STATE] = 1.0
    sel = jnp.asarray(sel, BF16)
    args = (xbc, state_conv, dt_e, z, state_ssm, conv_w, br, dtb, alog, dsk, nw, sel)
    in_specs = [per(a) for a in args[:5]] + [full(a) for a in args[5:]]
    return pl.pallas_call(
        functools.partial(_ssd_sample_kernel, T=T),
        grid=(b,),
        in_specs=in_specs,
        out_specs=[pl.BlockSpec((1, T, SSM_WIDTH), lambda i: (i, 0, 0)), per(state_ssm)],
        out_shape=[jax.ShapeDtypeStruct((b, T, SSM_WIDTH), F32),
                   jax.ShapeDtypeStruct(state_ssm.shape, F32)],
        scratch_shapes=[pltpu.VMEM((SUBLANES, SSM_WIDTH), F32)],
        compiler_params=_cparams(("parallel",)),
        name="ssd_sample",
    )(*args)


_KM_PAGES = 32


def _kmean_pages_kernel(pt_ref, *refs):
    o_ref = refs[-1]
    si = pl.program_id(1)
    nb = o_ref.shape[2]

    @pl.when(si == 0)
    def _():
        o_ref[...] = jnp.zeros_like(o_ref)

    lane = lax.broadcasted_iota(jnp.int32, (ATT_WIDTH, nb), 1)
    acc = o_ref[0]
    for b in range(_KM_PAGES // PAGES_PER_BLOCK):
        tok = refs[b * PAGES_PER_BLOCK][0].reshape(ATT_WIDTH, PAGE_SIZE)
        for pg in range(1, PAGES_PER_BLOCK):
            tok = tok + refs[b * PAGES_PER_BLOCK + pg][0].reshape(ATT_WIDTH, PAGE_SIZE)
        mean = jnp.sum(tok, axis=1, keepdims=True) * (1.0 / MOBA_BLOCK)
        acc = jnp.where(lane == si * (_KM_PAGES // PAGES_PER_BLOCK) + b, mean, acc)
    o_ref[0] = acc


def _kmean_sample(pool_kt, page_table):
    b, n_pages = page_table.shape
    nb = n_pages // PAGES_PER_BLOCK
    steps = n_pages // _KM_PAGES
    page_spec = lambda r: pl.BlockSpec(
        (1, ATT_HEADS, ATT_HEAD_DIM, PAGE_SIZE),
        lambda bi, si, pt: (pt[bi * n_pages + si * _KM_PAGES + r], 0, 0, 0))
    grid_spec = pltpu.PrefetchScalarGridSpec(
        num_scalar_prefetch=1,
        grid=(b, steps),
        in_specs=[page_spec(r) for r in range(_KM_PAGES)],
        out_specs=pl.BlockSpec((1, ATT_WIDTH, nb), lambda bi, si, pt: (bi, 0, 0)),
    )
    return pl.pallas_call(
        _kmean_pages_kernel,
        grid_spec=grid_spec,
        out_shape=jax.ShapeDtypeStruct((b, ATT_WIDTH, nb), F32),
        compiler_params=_cparams(("parallel", "arbitrary")),
        name="moba_kmean_pages",
    )(page_table.reshape(-1), *([pool_kt] * _KM_PAGES))


def _select_sample_kernel(q_ref, km_ref, o_ref):
    q = q_ref[0]
    nb = km_ref.shape[2]
    t8 = q.shape[0]
    blk = lax.broadcasted_iota(jnp.int32, (t8, nb), 1).astype(F32)
    lane = lax.broadcasted_iota(jnp.int32, (t8, LANES), 1)
    for h in range(ATT_HEADS):
        rows = slice(h * ATT_HEAD_DIM, (h + 1) * ATT_HEAD_DIM)
        s = _dot_hi(q[:, rows], km_ref[0, rows, :])
        tile = jnp.zeros((t8, LANES), F32)
        for r, (idx, _) in enumerate(_topk_hits(s, blk, 1)):
            tile = jnp.where(lane == r, idx, tile)
        o_ref[0, h] = tile.astype(jnp.int32)


def _select_sample(q8, kmeanT):
    b, t8, _ = q8.shape
    nb = kmeanT.shape[2]
    assert nb >= MOBA_TOPK
    return pl.pallas_call(
        _select_sample_kernel,
        grid=(b,),
        in_specs=[pl.BlockSpec((1, t8, ATT_WIDTH), lambda i: (i, 0, 0)),
                  pl.BlockSpec((1, ATT_WIDTH, nb), lambda i: (i, 0, 0))],
        out_specs=pl.BlockSpec((1, ATT_HEADS, t8, LANES), lambda i: (i, 0, 0, 0)),
        out_shape=jax.ShapeDtypeStruct((b, ATT_HEADS, t8, LANES), jnp.int32),
        compiler_params=_cparams(("parallel",)),
        name="moba_select_sample",
    )(q8, kmeanT)


_N_SLAB = MOBA_TOPK * PAGES_PER_BLOCK


def _attend_sample_kernel(sel_ref, pt_ref, rb_ref, q_ref, kn_ref, vn_ref, near_ref, pk_ref, pv_ref,
                          o_ref, kbuf, vbuf, sem, *, T, n_pages):
    bi = pl.program_id(0)
    t = pl.program_id(1)
    step = bi * T + t
    nsteps = pl.num_programs(0) * T
    nb = n_pages // PAGES_PER_BLOCK
    D = ATT_HEAD_DIM

    def copies(s, slot):
        b_s = s // T
        out = []
        for h in range(ATT_HEADS):
            for r in range(MOBA_TOPK):
                blk = sel_ref[(s * ATT_HEADS + h) * MOBA_TOPK + r]
                for pg in range(PAGES_PER_BLOCK):
                    phys = pt_ref[b_s * n_pages + blk * PAGES_PER_BLOCK + pg]
                    i = r * PAGES_PER_BLOCK + pg
                    out.append(pltpu.make_async_copy(pk_ref.at[phys, h], kbuf.at[slot, h, i], sem.at[0, slot]))
                    out.append(pltpu.make_async_copy(pv_ref.at[phys, h], vbuf.at[slot, h, i], sem.at[1, slot]))
        return out

    @pl.when(step == 0)
    def _():
        for cp in copies(step, 0):
            cp.start()

    @pl.when(step + 1 < nsteps)
    def _():
        for cp in copies(step + 1, (step + 1) % 2):
            cp.start()

    slot = step % 2
    for cp in copies(step, slot):
        cp.wait()

    first_row = jnp.where(lax.broadcasted_iota(jnp.int32, (SUBLANES, PAGE_SIZE), 0) == 0, 1.0, 0.0).astype(BF16)
    q8 = jnp.concatenate([q_ref[0], jnp.zeros((SUBLANES - 1, ATT_WIDTH), F32)], axis=0)
    qb = _dot_tn_exactrhs(q8, first_row)
    lane_t = lax.broadcasted_iota(jnp.int32, (1, T), 1)
    neg_tail = jnp.full((1, PAGE_SIZE - T), NEG, F32)
    neg_row = jnp.full((1, PAGE_SIZE), NEG, F32)

    tiles = []
    for h in range(ATT_HEADS):
        rows = slice(h * D, (h + 1) * D)
        cfar = rb_ref[(RPE_BUCKETS - 1) * ATT_HEADS + h]
        s_rows = []
        for r in range(MOBA_TOPK):
            blk = sel_ref[(step * ATT_HEADS + h) * MOBA_TOPK + r]
            for pg in range(PAGES_PER_BLOCK):
                s = jnp.sum(kbuf[slot, h, r * PAGES_PER_BLOCK + pg] * qb[rows], axis=0, keepdims=True)
                s_rows.append(s + jnp.where(blk == nb - 1, near_ref[h, 0, pg:pg + 1, :], cfar))
        s_own = jnp.sum(kn_ref[0, rows, :] * qb[rows, 0:T], axis=0, keepdims=True)
        b_own = jnp.zeros((1, T), F32)
        for tp in range(T):
            b_own = jnp.where(lane_t == tp, rb_ref[jnp.maximum(t - tp, 0) * ATT_HEADS + h], b_own)
        s_own = jnp.where(lane_t <= t, s_own + b_own, NEG)
        tiles.append(jnp.concatenate(s_rows + [jnp.concatenate([s_own, neg_tail], axis=1), neg_row], axis=0))

    ones8 = jnp.ones((SUBLANES, PAGE_SIZE), BF16)
    zero_tail = jnp.zeros((D, PAGE_SIZE - T), F32)
    outs = []
    for h, s in enumerate(tiles):
        rows = slice(h * D, (h + 1) * D)
        m = jnp.max(jnp.max(s, axis=1, keepdims=True), axis=0, keepdims=True)
        p = jnp.exp(s - m)
        l = jnp.sum(jnp.sum(p, axis=1, keepdims=True), axis=0, keepdims=True)
        acc = jnp.concatenate([vn_ref[0, rows, :] * p[_N_SLAB:_N_SLAB + 1, 0:T], zero_tail], axis=1)
        for n in range(_N_SLAB):
            acc = acc + vbuf[slot, h, n] * p[n:n + 1, :]
        o = None
        for part in _split3(acc):
            term = _dot_nt(ones8, part)
            o = term if o is None else o + term
        outs.append(o[0:1, :] / l)
    o_ref[0] = jnp.concatenate(outs, axis=1)


def _attend_sample(sel, page_table, rel_bias, q_col, k_newT, v_newT, near, pool_kt, pool_vt):
    b, _, T = k_newT.shape
    n_pages = page_table.shape[1]
    assert T <= RPE_BUCKETS // 2
    slab = (2, ATT_HEADS, _N_SLAB, ATT_HEAD_DIM, PAGE_SIZE)
    grid_spec = pltpu.PrefetchScalarGridSpec(
        num_scalar_prefetch=3,
        grid=(b, T),
        in_specs=[pl.BlockSpec((1, 1, ATT_WIDTH), lambda bi, t, *_: (bi * T + t, 0, 0)),
                  pl.BlockSpec((1, ATT_WIDTH, T), lambda bi, t, *_: (bi, 0, 0)),
                  pl.BlockSpec((1, ATT_WIDTH, T), lambda bi, t, *_: (bi, 0, 0)),
                  pl.BlockSpec((ATT_HEADS, 1, PAGES_PER_BLOCK, PAGE_SIZE), lambda bi, t, *_: (0, t, 0, 0)),
                  pl.BlockSpec(memory_space=pl.ANY),
                  pl.BlockSpec(memory_space=pl.ANY)],
        out_specs=pl.BlockSpec((1, 1, ATT_WIDTH), lambda bi, t, *_: (bi * T + t, 0, 0)),
        scratch_shapes=[pltpu.VMEM(slab, F32), pltpu.VMEM(slab, F32), pltpu.SemaphoreType.DMA((2, 2))],
    )
    return pl.pallas_call(
        functools.partial(_attend_sample_kernel, T=T, n_pages=n_pages),
        grid_spec=grid_spec,
        out_shape=jax.ShapeDtypeStruct((b * T, 1, ATT_WIDTH), F32),
        compiler_params=_cparams(("arbitrary", "arbitrary")),
        name="moba_attend_sample",
    )(sel, page_table.reshape(-1), rel_bias.reshape(-1), q_col, k_newT, v_newT, near, pool_kt, pool_vt)


def _prompt_layer(x, mem, norm_w, w_in, conv_w, conv_b, dt_bias, a_log, d_skip, ssm_norm_w,
                  mem_norm_w, w_mem_kv, w_out, rel_bias, final_norm_w):
    s = x.shape[0]
    W = MOBA_BLOCK
    z, xbc, gaT, qx, gx, kb3, km3, qT3, kT, vT, vT3, dtT = _in_proj_prompt(x, norm_w, w_in)
    y_ssd, state = _ssd_prompt(xbc, dtT, z, conv_w, conv_b, dt_bias, a_log, d_skip, ssm_norm_w)
    ki = np.arange(W)[:, None]
    qi = np.arange(W)[None, :]
    own_map = np.where(ki <= qi, _bucket_np(qi - ki), -1)
    d01 = _bias_table(rel_bias, np.stack([own_map, _bucket_np(W + qi - ki)]), LOG2E)
    y_attT = _moba_prompt(qT3, kb3, vT3, km3.reshape(-1, ATT_WIDTH), d01[:, 0], d01[:, 1], gaT, rel_bias)
    mk, mv = _memory_kv(mem, mem_norm_w, w_mem_kv)
    y_x = _xattn_prompt(qx, gx, mk, mv, tq=512)
    y = _out_proj(x, y_ssd, y_attT, None, y_x, w_out, final_norm_w, tm=512)
    conv_state = xbc[s - (CONV_W - 1):]
    heads = lambda aT: aT.reshape(ATT_HEADS, ATT_HEAD_DIM, s).transpose(2, 0, 1)
    return y, heads(kT), heads(vT), mk, mv, state, conv_state


def _sample_layer(x, cache_k, cache_v, page_table, mem_k, mem_v, state_ssm, state_conv, norm_w, w_in,
                  conv_w, conv_b, dt_bias, a_log, d_skip, ssm_norm_w, w_out, rel_bias, final_norm_w):
    b, T, d = x.shape
    n_pages = page_table.shape[1]
    assert (n_pages * PAGE_SIZE) % MOBA_BLOCK == 0 and n_pages % _KM_PAGES == 0
    assert CONV_W - 1 <= T <= SUBLANES
    xf = x.reshape(b * T, d)
    z, xbc, q, k, v, ga, qx, gx, dtT = _in_proj_sample(xf, norm_w, w_in)
    per = lambda a: a.reshape(b, T, a.shape[-1])
    y_ssd, h_new = _ssd_sample(per(xbc), state_conv, per(dtT.T), per(z), state_ssm,
                               conv_w, conv_b, dt_bias, a_log, d_skip, ssm_norm_w)
    pad8 = lambda a: jnp.pad(per(a), ((0, 0), (0, SUBLANES - T), (0, 0)))
    pool_kt = cache_k.transpose(0, 2, 3, 1)
    pool_vt = cache_v.transpose(0, 2, 3, 1)
    kmeanT = _kmean_sample(pool_kt, page_table)
    sel = _select_sample(pad8(q), kmeanT)[:, :, :T, :MOBA_TOPK]
    sel = sel.transpose(0, 2, 1, 3).reshape(-1)
    kk = (np.arange(PAGES_PER_BLOCK)[:, None] * PAGE_SIZE + np.arange(PAGE_SIZE)[None, :])[None]
    tt = np.arange(T)[:, None, None]
    near = _bias_table(rel_bias, _bucket_np(MOBA_BLOCK + tt - kk))
    q_col = (q * (ATT_HEAD_DIM ** -0.5)).reshape(b * T, 1, ATT_WIDTH)
    y_att = _attend_sample(sel, page_table, rel_bias, q_col, per(k).transpose(0, 2, 1),
                           per(v).transpose(0, 2, 1), near, pool_kt, pool_vt)
    y_att = y_att.reshape(b * T, ATT_WIDTH)
    y_x = _xattn_sample(pad8(qx), pad8(gx), mem_k, mem_v)
    y_x = y_x[:, :T].reshape(b * T, X_WIDTH)
    y = _out_proj(xf, y_ssd.reshape(b * T, SSM_WIDTH), y_att, ga, y_x, w_out, final_norm_w, tm=b * T)
    conv_state = per(xbc)[:, T - (CONV_W - 1):]
    return y.reshape(b, T, d), per(k), per(v), h_new, conv_state


def kernel(x_prompt, x_sample, mem_prompt, cache_k, cache_v, page_table, cache_mem_k, cache_mem_v,
           state_ssm, state_conv, norm_w, w_in, conv_w, conv_b, dt_bias, a_log, d_skip, ssm_norm_w,
           mem_norm_w, w_mem_kv, w_out, rel_bias, final_norm_w):
    bp, s, d = x_prompt.shape
    bs, T, _ = x_sample.shape
    depth = w_in.shape[0]
    assert bp == 1 and depth == 1
    l = 0
    y_p, k_p, v_p, mk, mv, ssm_p, conv_p = _prompt_layer(
        x_prompt[0], mem_prompt[0], norm_w[l], w_in[l], conv_w[l], conv_b[l], dt_bias[l], a_log[l],
        d_skip[l], ssm_norm_w[l], mem_norm_w[l], w_mem_kv[l], w_out[l], rel_bias, final_norm_w)
    y_s, k_s, v_s, ssm_s, conv_s = _sample_layer(
        x_sample, cache_k[l], cache_v[l], page_table, cache_mem_k[l], cache_mem_v[l], state_ssm[l],
        state_conv[l], norm_w[l], w_in[l], conv_w[l], conv_b[l], dt_bias[l], a_log[l], d_skip[l],
        ssm_norm_w[l], w_out[l], rel_bias, final_norm_w)
    nm = mem_prompt.shape[1]
    return (y_p[None],
            y_s,
            k_p[None, None],
            v_p[None, None],
            mk.reshape(1, 1, nm, X_HEADS, X_HEAD_DIM),
            mv.reshape(1, 1, nm, X_HEADS, X_HEAD_DIM),
            ssm_p[None, None],
            conv_p[None, None],
            k_s.reshape(1, bs, T, ATT_HEADS, ATT_HEAD_DIM),
            v_s.reshape(1, bs, T, ATT_HEADS, ATT_HEAD_DIM),
            ssm_s[None],
            conv_s[None])
```

```python
import functools
import math

import numpy as np
import jax
import jax.numpy as jnp
from jax import lax
from jax.experimental import pallas as pl
from jax.experimental.pallas import tpu as pltpu

F32 = jnp.float32
BF16 = jnp.bfloat16
EPS = 1e-6

SSM_HEADS = 16
SSM_HEAD_DIM = 64
SSM_GROUPS = 2
SSM_HG = SSM_HEADS // SSM_GROUPS
SSM_STATE = 128
SSM_WIDTH = SSM_HEADS * SSM_HEAD_DIM
SSM_GW = SSM_WIDTH // SSM_GROUPS
SSM_CHUNK = 128
CONV_W = 4
CONV_DIM = SSM_WIDTH + 2 * SSM_GROUPS * SSM_STATE
ATT_HEADS = 8
ATT_HEAD_DIM = 64
ATT_WIDTH = ATT_HEADS * ATT_HEAD_DIM
MOBA_BLOCK = 256
MOBA_TOPK = 3
PAGE_SIZE = 128
PAGES_PER_BLOCK = MOBA_BLOCK // PAGE_SIZE
RPE_BUCKETS = 32
RPE_MAX_DIST = 128
X_HEADS = 4
X_HEAD_DIM = 128
X_WIDTH = X_HEADS * X_HEAD_DIM

LANES = 128
SUBLANES = 8
VMEM_LIMIT = 56 * 1024 * 1024
NEG = -1e30
BIG = 2.0 ** 100
LOG2E = math.log2(math.e)


def _cparams(sem, vmem=VMEM_LIMIT):
    return pltpu.CompilerParams(dimension_semantics=sem, vmem_limit_bytes=vmem)


def _split2(x):
    hi = x.astype(BF16)
    lo = (x - hi.astype(F32)).astype(BF16)
    return hi, lo


def _split3(x):
    hi = x.astype(BF16)
    r = x - hi.astype(F32)
    mid = r.astype(BF16)
    lo = (r - mid.astype(F32)).astype(BF16)
    return hi, mid, lo


def _dot(a, b):
    return jnp.dot(a, b, preferred_element_type=F32)


def _dot_nt(a, b):
    return lax.dot_general(a, b, (((1,), (1,)), ((), ())), preferred_element_type=F32)


def _dot_tn(a, b):
    return lax.dot_general(a, b, (((0,), (0,)), ((), ())), preferred_element_type=F32)


def _dot_f32_exactrhs(a, b_bf16, passes=3):
    parts = _split3(a) if passes == 3 else _split2(a)
    out = _dot(parts[0], b_bf16)
    for p in parts[1:]:
        out = out + _dot(p, b_bf16)
    return out


def _dot_hi(a, b):
    ah, al = _split2(a)
    bh, bl = _split2(b)
    return _dot(ah, bh) + _dot(al, bh) + _dot(ah, bl)


def _silu(x):
    return x / (1.0 + jnp.exp(-x))


def _softplus(x):
    return jnp.maximum(x, 0.0) + jnp.log1p(jnp.exp(-jnp.abs(x)))


def _rms(x, g):
    ms = jnp.mean(x * x, axis=-1, keepdims=True)
    return (x * lax.rsqrt(ms + EPS)) * g


_MAIN_SPLITS = (SSM_WIDTH, CONV_DIM, ATT_WIDTH, ATT_WIDTH, ATT_WIDTH, ATT_WIDTH, X_WIDTH, X_WIDTH)


_W_ROW_CHUNK = 512


def _inproj_kernel(x_ref, nw_ref, wT_ref, z_ref, xbc_ref, q_ref, k_ref, v_ref, ga_ref, qx_ref, gx_ref, dtT_ref):
    xn = _rms(x_ref[...], nw_ref[...])
    xh, xl = _split2(xn)

    def proj(lo, width):
        wh, wl = _split2(wT_ref[lo:lo + width, :])
        return _dot_nt(xh, wh) + _dot_nt(xl, wh) + _dot_nt(xh, wl)

    outs = (z_ref, xbc_ref, None, q_ref, k_ref, v_ref, ga_ref, qx_ref, gx_ref)
    lo = 0
    for o_ref, width in zip(outs, _IN_SPLITS):
        if o_ref is None:
            wh, wl = _split2(wT_ref[lo:lo + width, :])
            dtT_ref[...] = _dot_nt(wh, xh) + _dot_nt(wh, xl) + _dot_nt(wl, xh)
        else:
            for c in range(0, width, _W_ROW_CHUNK):
                o_ref[:, c:c + _W_ROW_CHUNK] = proj(lo + c, _W_ROW_CHUNK)
        lo += width


_IN_SPLITS = (SSM_WIDTH, CONV_DIM, SSM_HEADS, ATT_WIDTH, ATT_WIDTH, ATT_WIDTH, ATT_WIDTH, X_WIDTH, X_WIDTH)


def _in_proj_sample(x, norm_w, w_in):
    m, d = x.shape
    wT = w_in.T
    full = lambda a: pl.BlockSpec(a.shape, lambda: (0,) * a.ndim)
    out_shapes = [jax.ShapeDtypeStruct((m, w), F32) for w in _MAIN_SPLITS]
    out_shapes += [jax.ShapeDtypeStruct((SSM_HEADS, m), F32)]
    nw = norm_w.reshape(1, d)
    return pl.pallas_call(
        _inproj_kernel,
        in_specs=[full(x), full(nw), full(wT)],
        out_specs=[pl.BlockSpec(s.shape, lambda: (0, 0)) for s in out_shapes],
        out_shape=out_shapes,
        compiler_params=pltpu.CompilerParams(vmem_limit_bytes=VMEM_LIMIT),
        name="in_proj_sample",
    )(x, nw, wT)


_PROMPT_NN = (SSM_WIDTH, CONV_DIM, X_WIDTH, X_WIDTH, ATT_WIDTH)


def _inproj_prompt_kernel(x_ref, nw_ref, wn_ref, wt_ref, z_ref, xbc_ref, gaT_ref, qx_ref, gx_ref, kb_ref,
                          km_ref, qT_ref, kT_ref, vT_ref, vTb_ref, dtT_ref):
    xh = _rms(x_ref[...], nw_ref[...]).astype(BF16)
    lo = 0
    for o_ref, width in zip((z_ref, xbc_ref, qx_ref, gx_ref), _PROMPT_NN[:-1]):
        val = _dot(xh, wn_ref[:, lo:lo + width])
        if o_ref is qx_ref:
            val = (val * (X_HEAD_DIM ** -0.5)).astype(BF16)
        o_ref[...] = val
        lo += width
    k = _dot(xh, wn_ref[:, lo:lo + ATT_WIDTH])
    kb_ref[0] = k.astype(BF16)
    km_ref[0] = jnp.mean(k, axis=0, keepdims=True)
    qT_ref[0] = _dot_nt(wt_ref[0:ATT_WIDTH, :], xh)
    kT_ref[...] = _dot_nt(wt_ref[ATT_WIDTH:2 * ATT_WIDTH, :], xh)
    vT = _dot_nt(wt_ref[2 * ATT_WIDTH:3 * ATT_WIDTH, :], xh)
    vT_ref[...] = vT
    vTb_ref[0] = vT.astype(BF16)
    gaT_ref[...] = _dot_nt(wt_ref[3 * ATT_WIDTH:4 * ATT_WIDTH, :], xh)
    dtT_ref[...] = _dot_nt(wt_ref[4 * ATT_WIDTH:, :], xh)


def _in_proj_prompt(x, norm_w, w_in):
    m, d = x.shape
    tm = MOBA_BLOCK
    nb = m // tm
    c_z, c_dt = 0, SSM_WIDTH + CONV_DIM
    c_q = c_dt + SSM_HEADS
    c_k, c_qx = c_q + ATT_WIDTH, c_q + 4 * ATT_WIDTH
    cols = lambda lo, w: w_in[:, lo:lo + w]
    wn = jnp.concatenate([cols(c_z, SSM_WIDTH + CONV_DIM), cols(c_qx, 2 * X_WIDTH),
                          cols(c_k, ATT_WIDTH)], axis=1).astype(BF16)
    wt = jnp.concatenate([cols(c_q, 4 * ATT_WIDTH), cols(c_dt, SSM_HEADS)], axis=1).T.astype(BF16)
    nw = norm_w.reshape(1, d)
    row = lambda w: pl.BlockSpec((tm, w), lambda i: (i, 0))
    col = lambda h: pl.BlockSpec((h, tm), lambda i: (0, i))
    full = lambda a: pl.BlockSpec(a.shape, lambda i: (0,) * a.ndim)
    tile3 = lambda a, b: pl.BlockSpec((1, a, b), lambda i: (i, 0, 0))
    out_specs = [row(SSM_WIDTH), row(CONV_DIM), col(ATT_WIDTH), row(X_WIDTH), row(X_WIDTH),
                 tile3(tm, ATT_WIDTH), tile3(1, ATT_WIDTH), tile3(ATT_WIDTH, tm),
                 col(ATT_WIDTH), col(ATT_WIDTH), tile3(ATT_WIDTH, tm), col(SSM_HEADS)]
    sds = jax.ShapeDtypeStruct
    out_shape = [sds((m, SSM_WIDTH), F32), sds((m, CONV_DIM), F32), sds((ATT_WIDTH, m), F32),
                 sds((m, X_WIDTH), BF16), sds((m, X_WIDTH), F32),
                 sds((nb, tm, ATT_WIDTH), BF16), sds((nb, 1, ATT_WIDTH), F32), sds((nb, ATT_WIDTH, tm), F32),
                 sds((ATT_WIDTH, m), F32), sds((ATT_WIDTH, m), F32), sds((nb, ATT_WIDTH, tm), BF16),
                 sds((SSM_HEADS, m), F32)]
    return pl.pallas_call(
        _inproj_prompt_kernel,
        grid=(nb,),
        in_specs=[row(d), full(nw), full(wn), full(wt)],
        out_specs=out_specs,
        out_shape=out_shape,
        compiler_params=_cparams(("parallel",)),
        name="in_proj_prompt",
    )(x, nw, wn, wt)


_SSD_CHUNKS = 2
def _ssd_kernel(xbc_ref, dtT_ref, z_ref, convw_ref, convb_ref, dtb_ref, alog_ref, dskip_ref, nw_ref,
                e_ref, y_ref, hT_ref, ext_sc, h_sc):
    c = pl.program_id(0)
    L = SSM_CHUNK

    @pl.when(c == 0)
    def _():
        ext_sc[0:SUBLANES, :] = jnp.zeros((SUBLANES, CONV_DIM), F32)
        h_sc[...] = jnp.zeros_like(h_sc)

    for sub in range(_SSD_CHUNKS):
        rows = slice(sub * L, (sub + 1) * L)
        x = xbc_ref[rows, :]
        ext_sc[SUBLANES:SUBLANES + L, :] = x
        w = convw_ref[...]
        acc = x * w[CONV_W - 1:CONV_W, :] + convb_ref[...]
        for k in range(1, CONV_W):
            acc = acc + ext_sc[SUBLANES - k:SUBLANES - k + L, :] * w[CONV_W - 1 - k:CONV_W - k, :]
        ext_sc[0:SUBLANES, :] = x[L - SUBLANES:L]
        u = _silu(acc)
        xs = u[:, :SSM_WIDTH]
        bm = u[:, SSM_WIDTH:SSM_WIDTH + SSM_GROUPS * SSM_STATE]
        cm = u[:, SSM_WIDTH + SSM_GROUPS * SSM_STATE:]

        dtT = _softplus(dtT_ref[:, rows] + dtb_ref[...])
        aT = dtT * (-jnp.exp(alog_ref[...]))
        r_i = lax.broadcasted_iota(jnp.int32, (L, L), 0)
        c_i = lax.broadcasted_iota(jnp.int32, (L, L), 1)
        upper = (r_i <= c_i).astype(BF16)
        acumT = _dot_f32_exactrhs(aT, upper)
        a_last = acumT[:, L - 1:L]
        to_endT = jnp.exp(a_last - acumT) * dtT
        eacT = jnp.exp(acumT)
        stack = jnp.concatenate(
            [acumT, dtT, to_endT, eacT, jnp.zeros((L - 4 * SSM_HEADS, L), F32)], axis=0)
        st = stack.T
        e = e_ref[...]
        dt_e = _dot_f32_exactrhs(st[:, SSM_HEADS:2 * SSM_HEADS], e, 2)
        toend_e = _dot_f32_exactrhs(st[:, 2 * SSM_HEADS:3 * SSM_HEADS], e, 2)
        eac_e = _dot_f32_exactrhs(st[:, 3 * SSM_HEADS:4 * SSM_HEADS], e, 2)
        cd_e = eac_e[L - 1:L, :]

        xd = (xs * dt_e).astype(BF16)
        xw = (xs * toend_e).astype(BF16)
        causal = r_i >= c_i
        ys = []
        yoffs = []
        for g in range(SSM_GROUPS):
            bg = bm[:, g * SSM_STATE:(g + 1) * SSM_STATE].astype(BF16)
            cg = cm[:, g * SSM_STATE:(g + 1) * SSM_STATE].astype(BF16)
            cb = jnp.where(causal, _dot_nt(cg, bg), 0.0)
            for hh in range(SSM_HG):
                h = g * SSM_HG + hh
                seg = st[:, h:h + 1] - acumT[h:h + 1, :]
                mix = (cb * jnp.exp(jnp.minimum(seg, 0.0))).astype(BF16)
                ys.append(_dot(mix, xd[:, h * SSM_HEAD_DIM:(h + 1) * SSM_HEAD_DIM]))
            hprev = h_sc[g]
            yoffs.append(_dot(cg, hprev.astype(BF16)))
            s_new = _dot_tn(bg, xw[:, g * SSM_GW:(g + 1) * SSM_GW])
            h_sc[g] = hprev * cd_e[:, g * SSM_GW:(g + 1) * SSM_GW] + s_new

        y = jnp.concatenate(ys, axis=1) + jnp.concatenate(yoffs, axis=1) * eac_e + dskip_ref[...] * xs
        gt = y * _silu(z_ref[rows, :])
        outs = []
        for g in range(SSM_GROUPS):
            gg = gt[:, g * SSM_GW:(g + 1) * SSM_GW]
            outs.append(gg * lax.rsqrt(jnp.mean(gg * gg, axis=-1, keepdims=True) + EPS))
        y_ref[rows, :] = (jnp.concatenate(outs, axis=1) * nw_ref[...]).astype(y_ref.dtype)

    @pl.when(c == pl.num_programs(0) - 1)
    def _():
        hT_ref[...] = h_sc[...]


def _head_expand():
    return jnp.asarray(np.repeat(np.eye(SSM_HEADS, dtype=np.float32), SSM_HEAD_DIM, axis=1), BF16)


def _ssd_prompt(xbc, dtT, z, conv_w, conv_b, dt_bias, a_log, d_skip, ssm_norm_w):
    m = xbc.shape[0]
    L = SSM_CHUNK
    step = L * _SSD_CHUNKS
    row = lambda w: pl.BlockSpec((step, w), lambda i: (i, 0))
    full = lambda a: pl.BlockSpec(a.shape, lambda i: (0,) * a.ndim)
    convb = conv_b.reshape(1, CONV_DIM)
    dtb = dt_bias.reshape(SSM_HEADS, 1)
    alog = a_log.reshape(SSM_HEADS, 1)
    dskip = jnp.repeat(d_skip, SSM_HEAD_DIM).reshape(1, SSM_WIDTH)
    nw = ssm_norm_w.reshape(1, SSM_WIDTH)
    e = _head_expand()
    y, hT = pl.pallas_call(
        _ssd_kernel,
        grid=(m // step,),
        in_specs=[row(CONV_DIM), pl.BlockSpec((SSM_HEADS, step), lambda i: (0, i)), row(SSM_WIDTH),
                  full(conv_w), full(convb), full(dtb), full(alog), full(dskip), full(nw), full(e)],
        out_specs=[row(SSM_WIDTH), pl.BlockSpec((SSM_GROUPS, SSM_STATE, SSM_GW), lambda i: (0, 0, 0))],
        out_shape=[jax.ShapeDtypeStruct((m, SSM_WIDTH), BF16),
                   jax.ShapeDtypeStruct((SSM_GROUPS, SSM_STATE, SSM_GW), F32)],
        scratch_shapes=[pltpu.VMEM((SUBLANES + L, CONV_DIM), F32),
                        pltpu.VMEM((SSM_GROUPS, SSM_STATE, SSM_GW), F32)],
        compiler_params=_cparams(("arbitrary",)),
        name="ssd_prompt",
    )(xbc, dtT, z, conv_w, convb, dtb, alog, dskip, nw, e)
    state = hT.reshape(SSM_GROUPS, SSM_STATE, SSM_HG, SSM_HEAD_DIM).transpose(0, 2, 3, 1)
    return y, state.reshape(SSM_HEADS, SSM_HEAD_DIM, SSM_STATE)


def _bucket_np(dist):
    n = np.maximum(dist, 0)
    exact = RPE_BUCKETS // 2
    nf = np.maximum(n, 1).astype(np.float32)
    scaled = (np.log(nf / np.float32(exact)) / np.float32(math.log(RPE_MAX_DIST / exact))
              * np.float32(RPE_BUCKETS - exact))
    large = exact + scaled.astype(np.int32)
    return np.where(n < exact, n, np.minimum(large, RPE_BUCKETS - 1)).astype(np.int32)


def _bias_kernel(rb_ref, map_ref, out_ref, *, scale):
    h = pl.program_id(0)
    bmap = map_ref[...]
    acc = jnp.full(bmap.shape, -2.0 * BIG, F32)
    for b in range(RPE_BUCKETS):
        acc = jnp.where(bmap == b, rb_ref[b * ATT_HEADS + h] * scale, acc)
    out_ref[0] = acc


def _bias_table(rel_bias, bucket_map, scale=1.0):
    bmap = jnp.asarray(bucket_map, jnp.int32)
    nd = bmap.ndim
    return pl.pallas_call(
        functools.partial(_bias_kernel, scale=scale),
        grid=(ATT_HEADS,),
        in_specs=[pl.BlockSpec(memory_space=pltpu.SMEM),
                  pl.BlockSpec(bmap.shape, lambda h: (0,) * nd)],
        out_specs=pl.BlockSpec((1,) + bmap.shape, lambda h: (h,) + (0,) * nd),
        out_shape=jax.ShapeDtypeStruct((ATT_HEADS,) + bmap.shape, F32),
        compiler_params=_cparams(("parallel",)),
        name="rel_bias_table",
    )(rel_bias.reshape(-1), bmap)


def _topk_hits(s, blk, axis):
    picks = []
    for _ in range(MOBA_TOPK):
        mx = jnp.max(s, axis=axis, keepdims=True)
        idx = jnp.min(jnp.where(s == mx, blk, float(2 * LANES)), axis=axis, keepdims=True)
        picks.append((idx, jnp.where(mx > -jnp.inf, 1.0, 0.0)))
        s = jnp.where(blk == idx, -jnp.inf, s)
    return picks


_HPS = 8
_K_HEADS = 4
_SPLIT = 4
_ONES_ROWS = 16


def _moba_kernel(cfar_ref, qT_ref, kb_ref, vT_ref, km_ref, d0_ref, d1_ref, gaT_ref, o_ref,
                 pen_sc, acc_sc, m_sc):
    hp = pl.program_id(0)
    i = pl.program_id(1)
    W = MOBA_BLOCK
    D = ATT_HEAD_DIM
    nb = km_ref.shape[0]
    blk = lax.broadcasted_iota(jnp.int32, (nb, W), 0).astype(F32)
    i_f = i.astype(F32)
    near = jnp.maximum(i - 1, 0)
    q_augs, cfars = [], []
    acc_sc[...] = jnp.zeros_like(acc_sc)
    for hh in range(_HPS):
        rows = slice(hh * D, (hh + 1) * D)
        cfar2 = cfar_ref[hp * _HPS + hh] * LOG2E
        qT = qT_ref[0, rows, :]
        sT = _dot_hi(km_ref[:, rows], qT)
        sT = jnp.where(blk < i_f, sT, -jnp.inf)
        hit = jnp.zeros((nb, W), F32)
        for idx, valid in _topk_hits(sT, blk, 0):
            hit = jnp.maximum(hit, jnp.where(blk == idx, valid, 0.0))
        pen_sc[hh] = jnp.where(hit > 0.0, cfar2, -2.0 * BIG)

        qs = (qT * (D ** -0.5 * LOG2E)).astype(BF16)
        parts = [jnp.zeros((D, W), BF16)] * _K_HEADS
        parts[hh % _K_HEADS] = qs
        q_augs.append(jnp.concatenate(parts, axis=0))
        cfars.append(cfar2)

    m_sc[...] = jnp.full(m_sc.shape, -BIG, F32)
    ones = jnp.ones((_ONES_ROWS, W), BF16)

    def scores(hh, j, bias_ref):
        g = hh // _K_HEADS
        lanes = slice(g * _K_HEADS * D, (g + 1) * _K_HEADS * D)
        s = _dot(kb_ref[j, :, lanes], q_augs[hh])
        return s if bias_ref is None else s + bias_ref[hh]

    def absorb(chains, ss):
        ps = []
        for (hh, a, j, _, pen), s in zip(chains, ss):
            m = m_sc[hh, a, 0:1, :]
            m_new = jnp.maximum(m, jnp.max(s, axis=0, keepdims=True) + pen)
            m_sc[hh, a, 0:1, :] = m_new
            ps.append((jnp.exp2(m - m_new), jnp.exp2(s - (m_new - pen)).astype(BF16)))
        for (hh, a, j, _, _), (alpha, p) in zip(chains, ps):
            v1 = jnp.concatenate([vT_ref[j, hh * D:(hh + 1) * D, :], ones], axis=0)
            acc_sc[hh, a] = alpha * acc_sc[hh, a] + _dot(v1, p)

    def group(chains):
        absorb(chains, [scores(hh, j, bias_ref) for hh, _, j, bias_ref, _ in chains])

    zero_row = jnp.zeros((1, W), F32)
    group([(hh, 0, i, d0_ref, zero_row) for hh in range(_HPS)]
          + [(hh, 1, near, d1_ref, pen_sc[hh, pl.ds(near, 1), :] - cfars[hh]) for hh in range(_HPS)])

    def far_chains(first, count):
        return [(hh, a, first + a, None, pen_sc[hh, pl.ds(first + a, 1), :])
                for a in range(count) for hh in range(_HPS)]

    def far_group(first, count):
        group(far_chains(first, count))

    def far_body(trip, carry):
        far_group(_SPLIT * trip, _SPLIT)
        return carry

    lax.fori_loop(0, near // _SPLIT, far_body, 0)
    done = (near // _SPLIT) * _SPLIT
    count = _SPLIT // 2
    while count >= 1:
        take = (near & count) != 0

        @pl.when(take)
        def _(done=done, count=count):
            far_group(done, count)

        done = done + jnp.where(take, count, 0)
        count //= 2

    outs = []
    for hh in range(_HPS):
        m = m_sc[hh, 0, 0:1, :]
        for a in range(1, _SPLIT):
            m = jnp.maximum(m, m_sc[hh, a, 0:1, :])
        acc = jnp.zeros((D + _ONES_ROWS, W), F32)
        for a in range(_SPLIT):
            acc = acc + jnp.exp2(m_sc[hh, a, 0:1, :] - m) * acc_sc[hh, a]
        outs.append(acc[:D] / acc[D:D + 1])
    o_ref[...] = (jnp.concatenate(outs, axis=0) * _silu(gaT_ref[...])).astype(o_ref.dtype)


def _moba_prompt(qT3, kb3, vT3, kmean, d0T, d1T, gaT, rel_bias):
    nb = qT3.shape[0]
    W = MOBA_BLOCK
    HW = _HPS * ATT_HEAD_DIM
    assert nb - 1 >= MOBA_TOPK and nb <= LANES and ATT_HEADS % _HPS == 0 and HW % LANES == 0
    cfar = rel_bias[RPE_BUCKETS - 1]
    return pl.pallas_call(
        _moba_kernel,
        grid=(ATT_HEADS // _HPS, nb),
        in_specs=[pl.BlockSpec(memory_space=pltpu.SMEM),
                  pl.BlockSpec((1, HW, W), lambda hp, i: (i, hp, 0)),
                  pl.BlockSpec((nb, W, HW), lambda hp, i: (0, 0, hp), pipeline_mode=pl.Buffered(1)),
                  pl.BlockSpec((nb, HW, W), lambda hp, i: (0, hp, 0), pipeline_mode=pl.Buffered(1)),
                  pl.BlockSpec((nb, HW), lambda hp, i: (0, hp)),
                  pl.BlockSpec((_HPS, W, W), lambda hp, i: (hp, 0, 0)),
                  pl.BlockSpec((_HPS, W, W), lambda hp, i: (hp, 0, 0)),
                  pl.BlockSpec((HW, W), lambda hp, i: (hp, i))],
        out_specs=pl.BlockSpec((HW, W), lambda hp, i: (hp, i)),
        out_shape=jax.ShapeDtypeStruct((ATT_WIDTH, nb * W), BF16),
        scratch_shapes=[pltpu.VMEM((_HPS, nb, W), F32),
                        pltpu.VMEM((_HPS, _SPLIT, ATT_HEAD_DIM + _ONES_ROWS, W), F32),
                        pltpu.VMEM((_HPS, _SPLIT, SUBLANES, W), F32)],
        compiler_params=_cparams(("parallel", "arbitrary")),
        name="moba_prompt",
    )(cfar, qT3, kb3, vT3, kmean, d0T, d1T, gaT)


def _memkv_kernel(mem_ref, g_ref, w_ref, k_ref, v_ref):
    xn = _rms(mem_ref[...], g_ref[...]).astype(BF16)
    kv = _dot(xn, w_ref[...])
    k_ref[...] = kv[:, :X_WIDTH]
    v_ref[...] = kv[:, X_WIDTH:]


def _memory_kv(mem, g, w):
    nm, d = mem.shape
    full = lambda a: pl.BlockSpec(a.shape, lambda: (0,) * a.ndim)
    g2 = g.reshape(1, d)
    wb = w.astype(BF16)
    return pl.pallas_call(
        _memkv_kernel,
        in_specs=[full(mem), full(g2), full(wb)],
        out_specs=[pl.BlockSpec((nm, X_WIDTH), lambda: (0, 0))] * 2,
        out_shape=[jax.ShapeDtypeStruct((nm, X_WIDTH), F32)] * 2,
        compiler_params=pltpu.CompilerParams(vmem_limit_bytes=VMEM_LIMIT),
        name="memory_kv",
    )(mem, g2, wb)


def _xattn_kernel(q_ref, gx_ref, mk_ref, mv_ref, o_ref):
    q = q_ref[...]
    mk = mk_ref[...].astype(BF16)
    mv = mv_ref[...].astype(BF16)
    outs = []
    for h in range(X_HEADS):
        sl = slice(h * X_HEAD_DIM, (h + 1) * X_HEAD_DIM)
        s = _dot_nt(q[:, sl], mk[:, sl])
        m = jnp.max(s, axis=1, keepdims=True)
        p = jnp.exp(s - m)
        l = jnp.sum(p, axis=1, keepdims=True)
        outs.append(_dot(p.astype(BF16), mv[:, sl]) / l)
    o_ref[...] = (jnp.concatenate(outs, axis=1) * _silu(gx_ref[...])).astype(o_ref.dtype)


def _xattn_prompt(qx, gx, mk, mv, *, tq):
    m = qx.shape[0]
    row = pl.BlockSpec((tq, X_WIDTH), lambda i: (i, 0))
    full = pl.BlockSpec(mk.shape, lambda i: (0, 0))
    return pl.pallas_call(
        _xattn_kernel,
        grid=(m // tq,),
        in_specs=[row, row, full, full],
        out_specs=row,
        out_shape=jax.ShapeDtypeStruct((m, X_WIDTH), BF16),
        compiler_params=_cparams(("parallel",)),
        name="xattn_prompt",
    )(qx, gx, mk, mv)


def _xattn_sample_kernel(q_ref, gx_ref, mk_ref, mv_ref, o_ref):
    q = q_ref[0]
    outs = []
    for h in range(X_HEADS):
        sl = slice(h * X_HEAD_DIM, (h + 1) * X_HEAD_DIM)
        s = _dot_nt((q[:, sl] * (X_HEAD_DIM ** -0.5)).astype(BF16), mk_ref[0, :, h, :].astype(BF16))
        m = jnp.max(s, axis=1, keepdims=True)
        p = jnp.exp(s - m)
        l = jnp.sum(p, axis=1, keepdims=True)
        outs.append(_dot(p.astype(BF16), mv_ref[0, :, h, :].astype(BF16)) / l)
    o_ref[0] = jnp.concatenate(outs, axis=1) * _silu(gx_ref[0])


def _xattn_sample(qx, gx, mk, mv):
    b, t8, _ = qx.shape
    nm = mk.shape[1]
    qs = pl.BlockSpec((1, t8, X_WIDTH), lambda i: (i, 0, 0))
    ms = pl.BlockSpec((1, nm, X_HEADS, X_HEAD_DIM), lambda i: (i, 0, 0, 0))
    return pl.pallas_call(
        _xattn_sample_kernel,
        grid=(b,),
        in_specs=[qs, qs, ms, ms],
        out_specs=qs,
        out_shape=jax.ShapeDtypeStruct((b, t8, X_WIDTH), F32),
        compiler_params=_cparams(("parallel",)),
        name="xattn_sample",
    )(qx, gx, mk, mv)


def _outproj_tail(x_ref, ys_ref, att_proj, yx_ref, w_ref, fw_ref, o_ref):
    h = x_ref[...] + _dot(ys_ref[...].astype(BF16), w_ref[0:SSM_WIDTH, :])
    h = h + att_proj
    h = h + _dot(yx_ref[...].astype(BF16), w_ref[SSM_WIDTH + ATT_WIDTH:, :])
    o_ref[...] = _rms(h, fw_ref[...])


def _outproj_kernel(x_ref, ys_ref, ya_ref, ga_ref, yx_ref, w_ref, fw_ref, o_ref):
    att = (ya_ref[...] * _silu(ga_ref[...])).astype(BF16)
    att_proj = _dot(att, w_ref[SSM_WIDTH:SSM_WIDTH + ATT_WIDTH, :])
    _outproj_tail(x_ref, ys_ref, att_proj, yx_ref, w_ref, fw_ref, o_ref)


def _outproj_t_kernel(x_ref, ys_ref, yaT_ref, yx_ref, w_ref, fw_ref, o_ref):
    att_proj = _dot_tn(yaT_ref[...].astype(BF16), w_ref[SSM_WIDTH:SSM_WIDTH + ATT_WIDTH, :])
    _outproj_tail(x_ref, ys_ref, att_proj, yx_ref, w_ref, fw_ref, o_ref)


def _out_proj(x, y_ssd, y_att, ga, y_x, w_out, final_norm_w, *, tm):
    m, d = x.shape
    row = lambda w: pl.BlockSpec((tm, w), lambda i: (i, 0))
    full = lambda a: pl.BlockSpec(a.shape, lambda i: (0,) * a.ndim)
    wb = w_out.astype(BF16)
    fw = final_norm_w.reshape(1, d)
    if ga is None:
        body, att_args = _outproj_t_kernel, (y_att,)
        att_specs = [pl.BlockSpec((ATT_WIDTH, tm), lambda i: (0, i))]
    else:
        body, att_args = _outproj_kernel, (y_att, ga)
        att_specs = [row(ATT_WIDTH), row(ATT_WIDTH)]
    return pl.pallas_call(
        body,
        grid=(m // tm,),
        in_specs=[row(d), row(SSM_WIDTH)] + att_specs + [row(X_WIDTH), full(wb), full(fw)],
        out_specs=row(d),
        out_shape=jax.ShapeDtypeStruct((m, d), F32),
        compiler_params=_cparams(("parallel",)),
        name="out_proj",
    )(x, y_ssd, *att_args, y_x, wb, fw)


def _ssd_sample_kernel(x_ref, c_ref, dt_ref, z_ref, h0_ref, w_ref, b_ref, dtb_ref, alog_ref, dsk_ref,
                       nw_ref, sel_ref, y_ref, h_ref, y_sc, *, T):
    P = SSM_HEAD_DIM
    N = SSM_STATE
    cr = c_ref[0]
    xr = x_ref[0]
    ext = lambda i: cr[i:i + 1, :] if i < CONV_W - 1 else xr[i - (CONV_W - 1):i - (CONV_W - 2), :]
    w = w_ref[...]
    rows = []
    for t in range(T):
        u = b_ref[...]
        for j in range(CONV_W):
            u = u + w[j:j + 1, :] * ext(t + j)
        rows.append(u)
    u = _silu(jnp.concatenate(rows + [jnp.zeros((SUBLANES - T, CONV_DIM), F32)], axis=0))
    xs = u[:, :SSM_WIDTH]
    dt = _softplus(dt_ref[0] + dtb_ref[...])
    dec = jnp.exp(dt * (-jnp.exp(alog_ref[...])))
    xdt = xs[0:T] * dt
    sel = sel_ref[...]
    bc = [_dot_tn_exactrhs(u[:, SSM_WIDTH + k * N:SSM_WIDTH + (k + 1) * N], sel)
          for k in range(2 * SSM_GROUPS)]
    for pair in range(SSM_HEADS // 2):
        g = (2 * pair) // SSM_HG
        lanes = slice(pair * 2 * P, (pair + 1) * 2 * P)
        hT = jnp.concatenate([h0_ref[0, 2 * pair], h0_ref[0, 2 * pair + 1]], axis=0).T
        for t in range(T):
            tl = slice(t * N, (t + 1) * N)
            hT = hT * dec[t:t + 1, lanes] + bc[g][:, tl] * xdt[t:t + 1, lanes]
            y_sc[t:t + 1, lanes] = jnp.sum(hT * bc[SSM_GROUPS + g][:, tl], axis=0, keepdims=True)
        h2 = hT.T
        h_ref[0, 2 * pair] = h2[:P]
        h_ref[0, 2 * pair + 1] = h2[P:]
    y = y_sc[0:T, :] + dsk_ref[...] * xs[0:T]
    gt = y * _silu(z_ref[0])
    outs = []
    for g in range(SSM_GROUPS):
        gg = gt[:, g * SSM_GW:(g + 1) * SSM_GW]
        outs.append(gg * lax.rsqrt(jnp.mean(gg * gg, axis=-1, keepdims=True) + EPS))
    y_ref[0] = jnp.concatenate(outs, axis=1) * nw_ref[...]


def _dot_tn_exactrhs(a, b_bf16):
    out = None
    for part in _split3(a):
        term = _dot_tn(part, b_bf16)
        out = term if out is None else out + term
    return out


def _ssd_sample(xbc, state_conv, dt_raw, z, state_ssm, conv_w, conv_b, dt_bias, a_log, d_skip, ssm_norm_w):
    b, T, _ = xbc.shape
    per = lambda a: pl.BlockSpec((1,) + a.shape[1:], lambda i: (i,) + (0,) * (a.ndim - 1))
    full = lambda a: pl.BlockSpec(a.shape, lambda i: (0,) * a.ndim)
    per_head = lambda v: jnp.repeat(v, SSM_HEAD_DIM, axis=-1)
    dt_e = per_head(dt_raw)
    br = conv_b.reshape(1, CONV_DIM)
    dtb = per_head(dt_bias).reshape(1, SSM_WIDTH)
    alog = per_head(a_log).reshape(1, SSM_WIDTH)
    dsk = per_head(d_skip).reshape(1, SSM_WIDTH)
    nw = ssm_norm_w.reshape(1, SSM_WIDTH)
    sel = np.zeros((SUBLANES, T * SSM_STATE), np.float32)
    for t in range(T):
        sel[t, t * SSM_STATE:(t + 1) * SSM_STATE] = 1.0
    sel = jnp.asarray(sel, BF16)
    args = (xbc, state_conv, dt_e, z, state_ssm, conv_w, br, dtb, alog, dsk, nw, sel)
    in_specs = [per(a) for a in args[:5]] + [full(a) for a in args[5:]]
    return pl.pallas_call(
        functools.partial(_ssd_sample_kernel, T=T),
        grid=(b,),
        in_specs=in_specs,
        out_specs=[pl.BlockSpec((1, T, SSM_WIDTH), lambda i: (i, 0, 0)), per(state_ssm)],
        out_shape=[jax.ShapeDtypeStruct((b, T, SSM_WIDTH), F32),
                   jax.ShapeDtypeStruct(state_ssm.shape, F32)],
        scratch_shapes=[pltpu.VMEM((SUBLANES, SSM_WIDTH), F32)],
        compiler_params=_cparams(("parallel",)),
        name="ssd_sample",
    )(*args)


_KM_PAGES = 32


def _select_pages_kernel(pt_ref, q_ref, *refs):
    page_refs, o_ref, km_sc = refs[:-2], refs[-2], refs[-1]
    si = pl.program_id(1)
    nb = km_sc.shape[1]

    @pl.when(si == 0)
    def _():
        km_sc[...] = jnp.zeros_like(km_sc)

    lane = lax.broadcasted_iota(jnp.int32, (ATT_WIDTH, nb), 1)
    acc = km_sc[...]
    for b in range(_KM_PAGES // PAGES_PER_BLOCK):
        tok = page_refs[b * PAGES_PER_BLOCK][0].reshape(ATT_WIDTH, PAGE_SIZE)
        for pg in range(1, PAGES_PER_BLOCK):
            tok = tok + page_refs[b * PAGES_PER_BLOCK + pg][0].reshape(ATT_WIDTH, PAGE_SIZE)
        mean = jnp.sum(tok, axis=1, keepdims=True) * (1.0 / MOBA_BLOCK)
        acc = jnp.where(lane == si * (_KM_PAGES // PAGES_PER_BLOCK) + b, mean, acc)
    km_sc[...] = acc

    @pl.when(si == pl.num_programs(1) - 1)
    def _():
        q = q_ref[0]
        t8 = q.shape[0]
        blk = lax.broadcasted_iota(jnp.int32, (t8, nb), 1).astype(F32)
        pick_lane = lax.broadcasted_iota(jnp.int32, (t8, LANES), 1)
        for h in range(ATT_HEADS):
            rows = slice(h * ATT_HEAD_DIM, (h + 1) * ATT_HEAD_DIM)
            s = _dot_hi(q[:, rows], km_sc[rows, :])
            tile = jnp.zeros((t8, LANES), F32)
            for r, (idx, _) in enumerate(_topk_hits(s, blk, 1)):
                tile = jnp.where(pick_lane == r, idx, tile)
            o_ref[0, h] = tile.astype(jnp.int32)


def _select_sample(q8, pool_kt, page_table):
    b, n_pages = page_table.shape
    t8 = q8.shape[1]
    nb = n_pages // PAGES_PER_BLOCK
    assert nb >= MOBA_TOPK
    steps = n_pages // _KM_PAGES
    page_spec = lambda r: pl.BlockSpec(
        (1, ATT_HEADS, ATT_HEAD_DIM, PAGE_SIZE),
        lambda bi, si, pt: (pt[bi * n_pages + si * _KM_PAGES + r], 0, 0, 0))
    grid_spec = pltpu.PrefetchScalarGridSpec(
        num_scalar_prefetch=1,
        grid=(b, steps),
        in_specs=[pl.BlockSpec((1, t8, ATT_WIDTH), lambda bi, si, pt: (bi, 0, 0))]
        + [page_spec(r) for r in range(_KM_PAGES)],
        out_specs=pl.BlockSpec((1, ATT_HEADS, t8, LANES), lambda bi, si, pt: (bi, 0, 0, 0)),
        scratch_shapes=[pltpu.VMEM((ATT_WIDTH, nb), F32)],
    )
    return pl.pallas_call(
        _select_pages_kernel,
        grid_spec=grid_spec,
        out_shape=jax.ShapeDtypeStruct((b, ATT_HEADS, t8, LANES), jnp.int32),
        compiler_params=_cparams(("parallel", "arbitrary")),
        name="moba_select_pages",
    )(page_table.reshape(-1), q8, *([pool_kt] * _KM_PAGES))


_N_SLAB = MOBA_TOPK * PAGES_PER_BLOCK


def _attend_sample_kernel(sel_ref, pt_ref, rb_ref, q_ref, kn_ref, vn_ref, near_ref, pk_ref, pv_ref,
                          o_ref, kbuf, vbuf, sem, *, T, n_pages):
    bi = pl.program_id(0)
    t = pl.program_id(1)
    step = bi * T + t
    nsteps = pl.num_programs(0) * T
    nb = n_pages // PAGES_PER_BLOCK
    D = ATT_HEAD_DIM

    def copies(s, slot):
        b_s = s // T
        out = []
        for h in range(ATT_HEADS):
            for r in range(MOBA_TOPK):
                blk = sel_ref[(s * ATT_HEADS + h) * MOBA_TOPK + r]
                for pg in range(PAGES_PER_BLOCK):
                    phys = pt_ref[b_s * n_pages + blk * PAGES_PER_BLOCK + pg]
                    i = r * PAGES_PER_BLOCK + pg
                    out.append(pltpu.make_async_copy(pk_ref.at[phys, h], kbuf.at[slot, h, i], sem.at[0, slot]))
                    out.append(pltpu.make_async_copy(pv_ref.at[phys, h], vbuf.at[slot, h, i], sem.at[1, slot]))
        return out

    @pl.when(step == 0)
    def _():
        for cp in copies(step, 0):
            cp.start()

    @pl.when(step + 1 < nsteps)
    def _():
        for cp in copies(step + 1, (step + 1) % 2):
            cp.start()

    slot = step % 2
    for cp in copies(step, slot):
        cp.wait()

    first_row = jnp.where(lax.broadcasted_iota(jnp.int32, (SUBLANES, PAGE_SIZE), 0) == 0, 1.0, 0.0).astype(BF16)
    q8 = jnp.concatenate([q_ref[0], jnp.zeros((SUBLANES - 1, ATT_WIDTH), F32)], axis=0)
    qb = _dot_tn_exactrhs(q8, first_row)
    lane_t = lax.broadcasted_iota(jnp.int32, (1, T), 1)
    neg_tail = jnp.full((1, PAGE_SIZE - T), NEG, F32)
    neg_row = jnp.full((1, PAGE_SIZE), NEG, F32)

    tiles = []
    for h in range(ATT_HEADS):
        rows = slice(h * D, (h + 1) * D)
        cfar = rb_ref[(RPE_BUCKETS - 1) * ATT_HEADS + h]
        s_rows = []
        for r in range(MOBA_TOPK):
            blk = sel_ref[(step * ATT_HEADS + h) * MOBA_TOPK + r]
            for pg in range(PAGES_PER_BLOCK):
                s = jnp.sum(kbuf[slot, h, r * PAGES_PER_BLOCK + pg] * qb[rows], axis=0, keepdims=True)
                s_rows.append(s + jnp.where(blk == nb - 1, near_ref[h, 0, pg:pg + 1, :], cfar))
        s_own = jnp.sum(kn_ref[0, rows, :] * qb[rows, 0:T], axis=0, keepdims=True)
        b_own = jnp.zeros((1, T), F32)
        for tp in range(T):
            b_own = jnp.where(lane_t == tp, rb_ref[jnp.maximum(t - tp, 0) * ATT_HEADS + h], b_own)
        s_own = jnp.where(lane_t <= t, s_own + b_own, NEG)
        tiles.append(jnp.concatenate(s_rows + [jnp.concatenate([s_own, neg_tail], axis=1), neg_row], axis=0))

    ones8 = jnp.ones((SUBLANES, PAGE_SIZE), BF16)
    zero_tail = jnp.zeros((D, PAGE_SIZE - T), F32)
    outs = []
    for h, s in enumerate(tiles):
        rows = slice(h * D, (h + 1) * D)
        m = jnp.max(jnp.max(s, axis=1, keepdims=True), axis=0, keepdims=True)
        p = jnp.exp(s - m)
        l = jnp.sum(jnp.sum(p, axis=1, keepdims=True), axis=0, keepdims=True)
        acc = jnp.concatenate([vn_ref[0, rows, :] * p[_N_SLAB:_N_SLAB + 1, 0:T], zero_tail], axis=1)
        for n in range(_N_SLAB):
            acc = acc + vbuf[slot, h, n] * p[n:n + 1, :]
        o = None
        for part in _split3(acc):
            term = _dot_nt(ones8, part)
            o = term if o is None else o + term
        outs.append(o[0:1, :] / l)
    o_ref[0] = jnp.concatenate(outs, axis=1)


def _attend_sample(sel, page_table, rel_bias, q_col, k_newT, v_newT, near, pool_kt, pool_vt):
    b, _, T = k_newT.shape
    n_pages = page_table.shape[1]
    assert T <= RPE_BUCKETS // 2
    slab = (2, ATT_HEADS, _N_SLAB, ATT_HEAD_DIM, PAGE_SIZE)
    grid_spec = pltpu.PrefetchScalarGridSpec(
        num_scalar_prefetch=3,
        grid=(b, T),
        in_specs=[pl.BlockSpec((1, 1, ATT_WIDTH), lambda bi, t, *_: (bi * T + t, 0, 0)),
                  pl.BlockSpec((1, ATT_WIDTH, T), lambda bi, t, *_: (bi, 0, 0)),
                  pl.BlockSpec((1, ATT_WIDTH, T), lambda bi, t, *_: (bi, 0, 0)),
                  pl.BlockSpec((ATT_HEADS, 1, PAGES_PER_BLOCK, PAGE_SIZE), lambda bi, t, *_: (0, t, 0, 0)),
                  pl.BlockSpec(memory_space=pl.ANY),
                  pl.BlockSpec(memory_space=pl.ANY)],
        out_specs=pl.BlockSpec((1, 1, ATT_WIDTH), lambda bi, t, *_: (bi * T + t, 0, 0)),
        scratch_shapes=[pltpu.VMEM(slab, F32), pltpu.VMEM(slab, F32), pltpu.SemaphoreType.DMA((2, 2))],
    )
    return pl.pallas_call(
        functools.partial(_attend_sample_kernel, T=T, n_pages=n_pages),
        grid_spec=grid_spec,
        out_shape=jax.ShapeDtypeStruct((b * T, 1, ATT_WIDTH), F32),
        compiler_params=_cparams(("arbitrary", "arbitrary")),
        name="moba_attend_sample",
    )(sel, page_table.reshape(-1), rel_bias.reshape(-1), q_col, k_newT, v_newT, near, pool_kt, pool_vt)


def _prompt_layer(x, mem, norm_w, w_in, conv_w, conv_b, dt_bias, a_log, d_skip, ssm_norm_w,
                  mem_norm_w, w_mem_kv, w_out, rel_bias, final_norm_w):
    s = x.shape[0]
    W = MOBA_BLOCK
    z, xbc, gaT, qx, gx, kb3, km3, qT3, kT, vT, vT3, dtT = _in_proj_prompt(x, norm_w, w_in)
    y_ssd, state = _ssd_prompt(xbc, dtT, z, conv_w, conv_b, dt_bias, a_log, d_skip, ssm_norm_w)
    ki = np.arange(W)[:, None]
    qi = np.arange(W)[None, :]
    own_map = np.where(ki <= qi, _bucket_np(qi - ki), -1)
    d01 = _bias_table(rel_bias, np.stack([own_map, _bucket_np(W + qi - ki)]), LOG2E)
    y_attT = _moba_prompt(qT3, kb3, vT3, km3.reshape(-1, ATT_WIDTH), d01[:, 0], d01[:, 1], gaT, rel_bias)
    mk, mv = _memory_kv(mem, mem_norm_w, w_mem_kv)
    y_x = _xattn_prompt(qx, gx, mk, mv, tq=512)
    y = _out_proj(x, y_ssd, y_attT, None, y_x, w_out, final_norm_w, tm=512)
    conv_state = xbc[s - (CONV_W - 1):]
    heads = lambda aT: aT.reshape(ATT_HEADS, ATT_HEAD_DIM, s).transpose(2, 0, 1)
    return y, heads(kT), heads(vT), mk, mv, state, conv_state


def _sample_layer(x, cache_k, cache_v, page_table, mem_k, mem_v, state_ssm, state_conv, norm_w, w_in,
                  conv_w, conv_b, dt_bias, a_log, d_skip, ssm_norm_w, w_out, rel_bias, final_norm_w):
    b, T, d = x.shape
    n_pages = page_table.shape[1]
    assert (n_pages * PAGE_SIZE) % MOBA_BLOCK == 0 and n_pages % _KM_PAGES == 0
    assert CONV_W - 1 <= T <= SUBLANES
    xf = x.reshape(b * T, d)
    z, xbc, q, k, v, ga, qx, gx, dtT = _in_proj_sample(xf, norm_w, w_in)
    per = lambda a: a.reshape(b, T, a.shape[-1])
    y_ssd, h_new = _ssd_sample(per(xbc), state_conv, per(dtT.T), per(z), state_ssm,
                               conv_w, conv_b, dt_bias, a_log, d_skip, ssm_norm_w)
    pad8 = lambda a: jnp.pad(per(a), ((0, 0), (0, SUBLANES - T), (0, 0)))
    pool_kt = cache_k.transpose(0, 2, 3, 1)
    pool_vt = cache_v.transpose(0, 2, 3, 1)
    sel = _select_sample(pad8(q), pool_kt, page_table)[:, :, :T, :MOBA_TOPK]
    sel = sel.transpose(0, 2, 1, 3).reshape(-1)
    kk = (np.arange(PAGES_PER_BLOCK)[:, None] * PAGE_SIZE + np.arange(PAGE_SIZE)[None, :])[None]
    tt = np.arange(T)[:, None, None]
    near = _bias_table(rel_bias, _bucket_np(MOBA_BLOCK + tt - kk))
    q_col = (q * (ATT_HEAD_DIM ** -0.5)).reshape(b * T, 1, ATT_WIDTH)
    y_att = _attend_sample(sel, page_table, rel_bias, q_col, per(k).transpose(0, 2, 1),
                           per(v).transpose(0, 2, 1), near, pool_kt, pool_vt)
    y_att = y_att.reshape(b * T, ATT_WIDTH)
    y_x = _xattn_sample(pad8(qx), pad8(gx), mem_k, mem_v)
    y_x = y_x[:, :T].reshape(b * T, X_WIDTH)
    y = _out_proj(xf, y_ssd.reshape(b * T, SSM_WIDTH), y_att, ga, y_x, w_out, final_norm_w, tm=b * T)
    conv_state = per(xbc)[:, T - (CONV_W - 1):]
    return y.reshape(b, T, d), per(k), per(v), h_new, conv_state


def kernel(x_prompt, x_sample, mem_prompt, cache_k, cache_v, page_table, cache_mem_k, cache_mem_v,
           state_ssm, state_conv, norm_w, w_in, conv_w, conv_b, dt_bias, a_log, d_skip, ssm_norm_w,
           mem_norm_w, w_mem_kv, w_out, rel_bias, final_norm_w):
    bp, s, d = x_prompt.shape
    bs, T, _ = x_sample.shape
    depth = w_in.shape[0]
    assert bp == 1 and depth == 1
    l = 0
    y_p, k_p, v_p, mk, mv, ssm_p, conv_p = _prompt_layer(
        x_prompt[0], mem_prompt[0], norm_w[l], w_in[l], conv_w[l], conv_b[l], dt_bias[l], a_log[l],
        d_skip[l], ssm_norm_w[l], mem_norm_w[l], w_mem_kv[l], w_out[l], rel_bias, final_norm_w)
    y_s, k_s, v_s, ssm_s, conv_s = _sample_layer(
        x_sample, cache_k[l], cache_v[l], page_table, cache_mem_k[l], cache_mem_v[l], state_ssm[l],
        state_conv[l], norm_w[l], w_in[l], conv_w[l], conv_b[l], dt_bias[l], a_log[l], d_skip[l],
        ssm_norm_w[l], w_out[l], rel_bias, final_norm_w)
    nm = mem_prompt.shape[1]
    return (y_p[None],
            y_s,
            k_p[None, None],
            v_p[None, None],
            mk.reshape(1, 1, nm, X_HEADS, X_HEAD_DIM),
            mv.reshape(1, 1, nm, X_HEADS, X_HEAD_DIM),
            ssm_p[None, None],
            conv_p[None, None],
            k_s.reshape(1, bs, T, ATT_HEADS, ATT_HEAD_DIM),
            v_s.reshape(1, bs, T, ATT_HEADS, ATT_HEAD_DIM),
            ssm_s[None],
            conv_s[None])
```

```python
import functools
import math

import numpy as np
import jax
import jax.numpy as jnp
from jax import lax
from jax.experimental import pallas as pl
from jax.experimental.pallas import tpu as pltpu

F32 = jnp.float32
BF16 = jnp.bfloat16
EPS = 1e-6

SSM_HEADS = 16
SSM_HEAD_DIM = 64
SSM_GROUPS = 2
SSM_HG = SSM_HEADS // SSM_GROUPS
SSM_STATE = 128
SSM_WIDTH = SSM_HEADS * SSM_HEAD_DIM
SSM_GW = SSM_WIDTH // SSM_GROUPS
SSM_CHUNK = 128
CONV_W = 4
CONV_DIM = SSM_WIDTH + 2 * SSM_GROUPS * SSM_STATE
ATT_HEADS = 8
ATT_HEAD_DIM = 64
ATT_WIDTH = ATT_HEADS * ATT_HEAD_DIM
MOBA_BLOCK = 256
MOBA_TOPK = 3
PAGE_SIZE = 128
PAGES_PER_BLOCK = MOBA_BLOCK // PAGE_SIZE
RPE_BUCKETS = 32
RPE_MAX_DIST = 128
X_HEADS = 4
X_HEAD_DIM = 128
X_WIDTH = X_HEADS * X_HEAD_DIM

LANES = 128
SUBLANES = 8
VMEM_LIMIT = 56 * 1024 * 1024
NEG = -1e30
BIG = 2.0 ** 100
LOG2E = math.log2(math.e)


def _cparams(sem, vmem=VMEM_LIMIT):
    return pltpu.CompilerParams(dimension_semantics=sem, vmem_limit_bytes=vmem)


def _split2(x):
    hi = x.astype(BF16)
    lo = (x - hi.astype(F32)).astype(BF16)
    return hi, lo


def _split3(x):
    hi = x.astype(BF16)
    r = x - hi.astype(F32)
    mid = r.astype(BF16)
    lo = (r - mid.astype(F32)).astype(BF16)
    return hi, mid, lo


def _dot(a, b):
    return jnp.dot(a, b, preferred_element_type=F32)


def _dot_nt(a, b):
    return lax.dot_general(a, b, (((1,), (1,)), ((), ())), preferred_element_type=F32)


def _dot_tn(a, b):
    return lax.dot_general(a, b, (((0,), (0,)), ((), ())), preferred_element_type=F32)


def _dot_f32_exactrhs(a, b_bf16, passes=3):
    parts = _split3(a) if passes == 3 else _split2(a)
    out = _dot(parts[0], b_bf16)
    for p in parts[1:]:
        out = out + _dot(p, b_bf16)
    return out


def _dot_hi(a, b):
    ah, al = _split2(a)
    bh, bl = _split2(b)
    return _dot(ah, bh) + _dot(al, bh) + _dot(ah, bl)


def _silu(x):
    return x / (1.0 + jnp.exp(-x))


def _softplus(x):
    return jnp.maximum(x, 0.0) + jnp.log1p(jnp.exp(-jnp.abs(x)))


def _rms(x, g):
    ms = jnp.mean(x * x, axis=-1, keepdims=True)
    return (x * lax.rsqrt(ms + EPS)) * g


_MAIN_SPLITS = (SSM_WIDTH, CONV_DIM, ATT_WIDTH, ATT_WIDTH, ATT_WIDTH, ATT_WIDTH, X_WIDTH, X_WIDTH)


_W_ROW_CHUNK = 512


def _inproj_kernel(x_ref, nw_ref, wT_ref, z_ref, xbc_ref, q_ref, k_ref, v_ref, ga_ref, qx_ref, gx_ref, dtT_ref):
    xn = _rms(x_ref[...], nw_ref[...])
    xh, xl = _split2(xn)

    def proj(lo, width):
        wh, wl = _split2(wT_ref[lo:lo + width, :])
        return _dot_nt(xh, wh) + _dot_nt(xl, wh) + _dot_nt(xh, wl)

    outs = (z_ref, xbc_ref, None, q_ref, k_ref, v_ref, ga_ref, qx_ref, gx_ref)
    lo = 0
    for o_ref, width in zip(outs, _IN_SPLITS):
        if o_ref is None:
            wh, wl = _split2(wT_ref[lo:lo + width, :])
            dtT_ref[...] = _dot_nt(wh, xh) + _dot_nt(wh, xl) + _dot_nt(wl, xh)
        else:
            for c in range(0, width, _W_ROW_CHUNK):
                o_ref[:, c:c + _W_ROW_CHUNK] = proj(lo + c, _W_ROW_CHUNK)
        lo += width


_IN_SPLITS = (SSM_WIDTH, CONV_DIM, SSM_HEADS, ATT_WIDTH, ATT_WIDTH, ATT_WIDTH, ATT_WIDTH, X_WIDTH, X_WIDTH)


def _in_proj_sample(x, norm_w, w_in):
    m, d = x.shape
    wT = w_in.T
    full = lambda a: pl.BlockSpec(a.shape, lambda: (0,) * a.ndim)
    out_shapes = [jax.ShapeDtypeStruct((m, w), F32) for w in _MAIN_SPLITS]
    out_shapes += [jax.ShapeDtypeStruct((SSM_HEADS, m), F32)]
    nw = norm_w.reshape(1, d)
    return pl.pallas_call(
        _inproj_kernel,
        in_specs=[full(x), full(nw), full(wT)],
        out_specs=[pl.BlockSpec(s.shape, lambda: (0, 0)) for s in out_shapes],
        out_shape=out_shapes,
        compiler_params=pltpu.CompilerParams(vmem_limit_bytes=VMEM_LIMIT),
        name="in_proj_sample",
    )(x, nw, wT)


_PROMPT_NN = (SSM_WIDTH, CONV_DIM, X_WIDTH, X_WIDTH, ATT_WIDTH)


def _inproj_prompt_kernel(x_ref, nw_ref, wn_ref, wt_ref, z_ref, xbc_ref, gaT_ref, qx_ref, gx_ref, kb_ref,
                          km_ref, qT_ref, kT_ref, vT_ref, vTb_ref, dtT_ref):
    xh = _rms(x_ref[...], nw_ref[...]).astype(BF16)
    lo = 0
    for o_ref, width in zip((z_ref, xbc_ref, qx_ref, gx_ref), _PROMPT_NN[:-1]):
        val = _dot(xh, wn_ref[:, lo:lo + width])
        if o_ref is qx_ref:
            val = (val * (X_HEAD_DIM ** -0.5)).astype(BF16)
        o_ref[...] = val
        lo += width
    k = _dot(xh, wn_ref[:, lo:lo + ATT_WIDTH])
    kb_ref[0] = k.astype(BF16)
    km_ref[0] = jnp.mean(k, axis=0, keepdims=True)
    qT_ref[0] = _dot_nt(wt_ref[0:ATT_WIDTH, :], xh)
    kT_ref[...] = _dot_nt(wt_ref[ATT_WIDTH:2 * ATT_WIDTH, :], xh)
    vT = _dot_nt(wt_ref[2 * ATT_WIDTH:3 * ATT_WIDTH, :], xh)
    vT_ref[...] = vT
    vTb_ref[0] = vT.astype(BF16)
    gaT_ref[...] = _dot_nt(wt_ref[3 * ATT_WIDTH:4 * ATT_WIDTH, :], xh)
    dtT_ref[...] = _dot_nt(wt_ref[4 * ATT_WIDTH:, :], xh)


def _in_proj_prompt(x, norm_w, w_in):
    m, d = x.shape
    tm = MOBA_BLOCK
    nb = m // tm
    c_z, c_dt = 0, SSM_WIDTH + CONV_DIM
    c_q = c_dt + SSM_HEADS
    c_k, c_qx = c_q + ATT_WIDTH, c_q + 4 * ATT_WIDTH
    cols = lambda lo, w: w_in[:, lo:lo + w]
    wn = jnp.concatenate([cols(c_z, SSM_WIDTH + CONV_DIM), cols(c_qx, 2 * X_WIDTH),
                          cols(c_k, ATT_WIDTH)], axis=1).astype(BF16)
    wt = jnp.concatenate([cols(c_q, 4 * ATT_WIDTH), cols(c_dt, SSM_HEADS)], axis=1).T.astype(BF16)
    nw = norm_w.reshape(1, d)
    row = lambda w: pl.BlockSpec((tm, w), lambda i: (i, 0))
    col = lambda h: pl.BlockSpec((h, tm), lambda i: (0, i))
    full = lambda a: pl.BlockSpec(a.shape, lambda i: (0,) * a.ndim)
    tile3 = lambda a, b: pl.BlockSpec((1, a, b), lambda i: (i, 0, 0))
    out_specs = [row(SSM_WIDTH), row(CONV_DIM), col(ATT_WIDTH), row(X_WIDTH), row(X_WIDTH),
                 tile3(tm, ATT_WIDTH), tile3(1, ATT_WIDTH), tile3(ATT_WIDTH, tm),
                 col(ATT_WIDTH), col(ATT_WIDTH), tile3(ATT_WIDTH, tm), col(SSM_HEADS)]
    sds = jax.ShapeDtypeStruct
    out_shape = [sds((m, SSM_WIDTH), F32), sds((m, CONV_DIM), F32), sds((ATT_WIDTH, m), F32),
                 sds((m, X_WIDTH), BF16), sds((m, X_WIDTH), F32),
                 sds((nb, tm, ATT_WIDTH), BF16), sds((nb, 1, ATT_WIDTH), F32), sds((nb, ATT_WIDTH, tm), F32),
                 sds((ATT_WIDTH, m), F32), sds((ATT_WIDTH, m), F32), sds((nb, ATT_WIDTH, tm), BF16),
                 sds((SSM_HEADS, m), F32)]
    return pl.pallas_call(
        _inproj_prompt_kernel,
        grid=(nb,),
        in_specs=[row(d), full(nw), full(wn), full(wt)],
        out_specs=out_specs,
        out_shape=out_shape,
        compiler_params=_cparams(("parallel",)),
        name="in_proj_prompt",
    )(x, nw, wn, wt)


_SSD_CHUNKS = 2
def _ssd_kernel(xbc_ref, dtT_ref, z_ref, convw_ref, convb_ref, dtb_ref, alog_ref, dskip_ref, nw_ref,
                e_ref, y_ref, hT_ref, ext_sc, h_sc):
    c = pl.program_id(0)
    L = SSM_CHUNK

    @pl.when(c == 0)
    def _():
        ext_sc[0:SUBLANES, :] = jnp.zeros((SUBLANES, CONV_DIM), F32)
        h_sc[...] = jnp.zeros_like(h_sc)

    for sub in range(_SSD_CHUNKS):
        rows = slice(sub * L, (sub + 1) * L)
        x = xbc_ref[rows, :]
        ext_sc[SUBLANES:SUBLANES + L, :] = x
        w = convw_ref[...]
        acc = x * w[CONV_W - 1:CONV_W, :] + convb_ref[...]
        for k in range(1, CONV_W):
            acc = acc + ext_sc[SUBLANES - k:SUBLANES - k + L, :] * w[CONV_W - 1 - k:CONV_W - k, :]
        ext_sc[0:SUBLANES, :] = x[L - SUBLANES:L]
        u = _silu(acc)
        xs = u[:, :SSM_WIDTH]
        bm = u[:, SSM_WIDTH:SSM_WIDTH + SSM_GROUPS * SSM_STATE]
        cm = u[:, SSM_WIDTH + SSM_GROUPS * SSM_STATE:]

        dtT = _softplus(dtT_ref[:, rows] + dtb_ref[...])
        aT = dtT * (-jnp.exp(alog_ref[...]))
        r_i = lax.broadcasted_iota(jnp.int32, (L, L), 0)
        c_i = lax.broadcasted_iota(jnp.int32, (L, L), 1)
        upper = (r_i <= c_i).astype(BF16)
        acumT = _dot_f32_exactrhs(aT, upper)
        a_last = acumT[:, L - 1:L]
        to_endT = jnp.exp(a_last - acumT) * dtT
        eacT = jnp.exp(acumT)
        stack = jnp.concatenate(
            [acumT, dtT, to_endT, eacT, jnp.zeros((L - 4 * SSM_HEADS, L), F32)], axis=0)
        st = stack.T
        e = e_ref[...]
        dt_e = _dot_f32_exactrhs(st[:, SSM_HEADS:2 * SSM_HEADS], e, 2)
        toend_e = _dot_f32_exactrhs(st[:, 2 * SSM_HEADS:3 * SSM_HEADS], e, 2)
        eac_e = _dot_f32_exactrhs(st[:, 3 * SSM_HEADS:4 * SSM_HEADS], e, 2)
        cd_e = eac_e[L - 1:L, :]

        xd = (xs * dt_e).astype(BF16)
        xw = (xs * toend_e).astype(BF16)
        causal = r_i >= c_i
        ys = []
        yoffs = []
        for g in range(SSM_GROUPS):
            bg = bm[:, g * SSM_STATE:(g + 1) * SSM_STATE].astype(BF16)
            cg = cm[:, g * SSM_STATE:(g + 1) * SSM_STATE].astype(BF16)
            cb = jnp.where(causal, _dot_nt(cg, bg), 0.0)
            for hh in range(SSM_HG):
                h = g * SSM_HG + hh
                seg = st[:, h:h + 1] - acumT[h:h + 1, :]
                mix = (cb * jnp.exp(jnp.minimum(seg, 0.0))).astype(BF16)
                ys.append(_dot(mix, xd[:, h * SSM_HEAD_DIM:(h + 1) * SSM_HEAD_DIM]))
            hprev = h_sc[g]
            yoffs.append(_dot(cg, hprev.astype(BF16)))
            s_new = _dot_tn(bg, xw[:, g * SSM_GW:(g + 1) * SSM_GW])
            h_sc[g] = hprev * cd_e[:, g * SSM_GW:(g + 1) * SSM_GW] + s_new

        y = jnp.concatenate(ys, axis=1) + jnp.concatenate(yoffs, axis=1) * eac_e + dskip_ref[...] * xs
        gt = y * _silu(z_ref[rows, :])
        outs = []
        for g in range(SSM_GROUPS):
            gg = gt[:, g * SSM_GW:(g + 1) * SSM_GW]
            outs.append(gg * lax.rsqrt(jnp.mean(gg * gg, axis=-1, keepdims=True) + EPS))
        y_ref[rows, :] = (jnp.concatenate(outs, axis=1) * nw_ref[...]).astype(y_ref.dtype)

    @pl.when(c == pl.num_programs(0) - 1)
    def _():
        hT_ref[...] = h_sc[...]


def _head_expand():
    return jnp.asarray(np.repeat(np.eye(SSM_HEADS, dtype=np.float32), SSM_HEAD_DIM, axis=1), BF16)


def _ssd_prompt(xbc, dtT, z, conv_w, conv_b, dt_bias, a_log, d_skip, ssm_norm_w):
    m = xbc.shape[0]
    L = SSM_CHUNK
    step = L * _SSD_CHUNKS
    row = lambda w: pl.BlockSpec((step, w), lambda i: (i, 0))
    full = lambda a: pl.BlockSpec(a.shape, lambda i: (0,) * a.ndim)
    convb = conv_b.reshape(1, CONV_DIM)
    dtb = dt_bias.reshape(SSM_HEADS, 1)
    alog = a_log.reshape(SSM_HEADS, 1)
    dskip = jnp.repeat(d_skip, SSM_HEAD_DIM).reshape(1, SSM_WIDTH)
    nw = ssm_norm_w.reshape(1, SSM_WIDTH)
    e = _head_expand()
    y, hT = pl.pallas_call(
        _ssd_kernel,
        grid=(m // step,),
        in_specs=[row(CONV_DIM), pl.BlockSpec((SSM_HEADS, step), lambda i: (0, i)), row(SSM_WIDTH),
                  full(conv_w), full(convb), full(dtb), full(alog), full(dskip), full(nw), full(e)],
        out_specs=[row(SSM_WIDTH), pl.BlockSpec((SSM_GROUPS, SSM_STATE, SSM_GW), lambda i: (0, 0, 0))],
        out_shape=[jax.ShapeDtypeStruct((m, SSM_WIDTH), BF16),
                   jax.ShapeDtypeStruct((SSM_GROUPS, SSM_STATE, SSM_GW), F32)],
        scratch_shapes=[pltpu.VMEM((SUBLANES + L, CONV_DIM), F32),
                        pltpu.VMEM((SSM_GROUPS, SSM_STATE, SSM_GW), F32)],
        compiler_params=_cparams(("arbitrary",)),
        name="ssd_prompt",
    )(xbc, dtT, z, conv_w, convb, dtb, alog, dskip, nw, e)
    state = hT.reshape(SSM_GROUPS, SSM_STATE, SSM_HG, SSM_HEAD_DIM).transpose(0, 2, 3, 1)
    return y, state.reshape(SSM_HEADS, SSM_HEAD_DIM, SSM_STATE)


def _bucket_np(dist):
    n = np.maximum(dist, 0)
    exact = RPE_BUCKETS // 2
    nf = np.maximum(n, 1).astype(np.float32)
    scaled = (np.log(nf / np.float32(exact)) / np.float32(math.log(RPE_MAX_DIST / exact))
              * np.float32(RPE_BUCKETS - exact))
    large = exact + scaled.astype(np.int32)
    return np.where(n < exact, n, np.minimum(large, RPE_BUCKETS - 1)).astype(np.int32)


def _bias_kernel(rb_ref, map_ref, out_ref, *, scale):
    h = pl.program_id(0)
    bmap = map_ref[...]
    acc = jnp.full(bmap.shape, -2.0 * BIG, F32)
    for b in range(RPE_BUCKETS):
        acc = jnp.where(bmap == b, rb_ref[b * ATT_HEADS + h] * scale, acc)
    out_ref[0] = acc


def _bias_table(rel_bias, bucket_map, scale=1.0):
    bmap = jnp.asarray(bucket_map, jnp.int32)
    nd = bmap.ndim
    return pl.pallas_call(
        functools.partial(_bias_kernel, scale=scale),
        grid=(ATT_HEADS,),
        in_specs=[pl.BlockSpec(memory_space=pltpu.SMEM),
                  pl.BlockSpec(bmap.shape, lambda h: (0,) * nd)],
        out_specs=pl.BlockSpec((1,) + bmap.shape, lambda h: (h,) + (0,) * nd),
        out_shape=jax.ShapeDtypeStruct((ATT_HEADS,) + bmap.shape, F32),
        compiler_params=_cparams(("parallel",)),
        name="rel_bias_table",
    )(rel_bias.reshape(-1), bmap)


def _topk_hits(s, blk, axis):
    picks = []
    for _ in range(MOBA_TOPK):
        mx = jnp.max(s, axis=axis, keepdims=True)
        idx = jnp.min(jnp.where(s == mx, blk, float(2 * LANES)), axis=axis, keepdims=True)
        picks.append((idx, jnp.where(mx > -jnp.inf, 1.0, 0.0)))
        s = jnp.where(blk == idx, -jnp.inf, s)
    return picks


_HPS = 8
_K_HEADS = 4
_SPLIT = 4
_ONES_ROWS = 16


def _moba_kernel(cfar_ref, qT_ref, kb_ref, vT_ref, km_ref, d0_ref, d1_ref, gaT_ref, o_ref,
                 pen_sc, acc_sc, m_sc):
    hp = pl.program_id(0)
    i = pl.program_id(1)
    W = MOBA_BLOCK
    D = ATT_HEAD_DIM
    nb = km_ref.shape[0]
    blk = lax.broadcasted_iota(jnp.int32, (nb, W), 0).astype(F32)
    i_f = i.astype(F32)
    near = jnp.maximum(i - 1, 0)
    q_augs, cfars = [], []
    acc_sc[...] = jnp.zeros_like(acc_sc)
    for hh in range(_HPS):
        rows = slice(hh * D, (hh + 1) * D)
        cfar2 = cfar_ref[hp * _HPS + hh] * LOG2E
        qT = qT_ref[0, rows, :]
        sT = _dot_hi(km_ref[:, rows], qT)
        sT = jnp.where(blk < i_f, sT, -jnp.inf)
        hit = jnp.zeros((nb, W), F32)
        for idx, valid in _topk_hits(sT, blk, 0):
            hit = jnp.maximum(hit, jnp.where(blk == idx, valid, 0.0))
        pen_sc[hh] = jnp.where(hit > 0.0, cfar2, -2.0 * BIG)

        qs = (qT * (D ** -0.5 * LOG2E)).astype(BF16)
        parts = [jnp.zeros((D, W), BF16)] * _K_HEADS
        parts[hh % _K_HEADS] = qs
        q_augs.append(jnp.concatenate(parts, axis=0))
        cfars.append(cfar2)

    m_sc[...] = jnp.full(m_sc.shape, -BIG, F32)
    ones = jnp.ones((_ONES_ROWS, W), BF16)

    def scores(hh, j, bias_ref):
        g = hh // _K_HEADS
        lanes = slice(g * _K_HEADS * D, (g + 1) * _K_HEADS * D)
        s = _dot(kb_ref[j, :, lanes], q_augs[hh])
        return s if bias_ref is None else s + bias_ref[hh]

    def absorb(chains, ss):
        ps = []
        for (hh, a, j, _, pen), s in zip(chains, ss):
            m = m_sc[hh, a, 0:1, :]
            m_new = jnp.maximum(m, jnp.max(s, axis=0, keepdims=True) + pen)
            m_sc[hh, a, 0:1, :] = m_new
            ps.append((jnp.exp2(m - m_new), jnp.exp2(s - (m_new - pen)).astype(BF16)))
        for (hh, a, j, _, _), (alpha, p) in zip(chains, ps):
            v1 = jnp.concatenate([vT_ref[j, hh * D:(hh + 1) * D, :], ones], axis=0)
            acc_sc[hh, a] = alpha * acc_sc[hh, a] + _dot(v1, p)

    def group(chains):
        absorb(chains, [scores(hh, j, bias_ref) for hh, _, j, bias_ref, _ in chains])

    zero_row = jnp.zeros((1, W), F32)
    group([(hh, 0, i, d0_ref, zero_row) for hh in range(_HPS)]
          + [(hh, 1, near, d1_ref, pen_sc[hh, pl.ds(near, 1), :] - cfars[hh]) for hh in range(_HPS)])

    def far_chains(first, count):
        return [(hh, a, first + a, None, pen_sc[hh, pl.ds(first + a, 1), :])
                for a in range(count) for hh in range(_HPS)]

    def far_group(first, count):
        group(far_chains(first, count))

    def far_body(trip, carry):
        far_group(_SPLIT * trip, _SPLIT)
        return carry

    lax.fori_loop(0, near // _SPLIT, far_body, 0)
    done = (near // _SPLIT) * _SPLIT
    count = _SPLIT // 2
    while count >= 1:
        take = (near & count) != 0

        @pl.when(take)
        def _(done=done, count=count):
            far_group(done, count)

        done = done + jnp.where(take, count, 0)
        count //= 2

    outs = []
    for hh in range(_HPS):
        m = m_sc[hh, 0, 0:1, :]
        for a in range(1, _SPLIT):
            m = jnp.maximum(m, m_sc[hh, a, 0:1, :])
        acc = jnp.zeros((D + _ONES_ROWS, W), F32)
        for a in range(_SPLIT):
            acc = acc + jnp.exp2(m_sc[hh, a, 0:1, :] - m) * acc_sc[hh, a]
        outs.append(acc[:D] / acc[D:D + 1])
    o_ref[...] = (jnp.concatenate(outs, axis=0) * _silu(gaT_ref[...])).astype(o_ref.dtype)


def _moba_prompt(qT3, kb3, vT3, kmean, d0T, d1T, gaT, rel_bias):
    nb = qT3.shape[0]
    W = MOBA_BLOCK
    HW = _HPS * ATT_HEAD_DIM
    assert nb - 1 >= MOBA_TOPK and nb <= LANES and ATT_HEADS % _HPS == 0 and HW % LANES == 0
    cfar = rel_bias[RPE_BUCKETS - 1]
    return pl.pallas_call(
        _moba_kernel,
        grid=(ATT_HEADS // _HPS, nb),
        in_specs=[pl.BlockSpec(memory_space=pltpu.SMEM),
                  pl.BlockSpec((1, HW, W), lambda hp, i: (i, hp, 0)),
                  pl.BlockSpec((nb, W, HW), lambda hp, i: (0, 0, hp), pipeline_mode=pl.Buffered(1)),
                  pl.BlockSpec((nb, HW, W), lambda hp, i: (0, hp, 0), pipeline_mode=pl.Buffered(1)),
                  pl.BlockSpec((nb, HW), lambda hp, i: (0, hp)),
                  pl.BlockSpec((_HPS, W, W), lambda hp, i: (hp, 0, 0)),
                  pl.BlockSpec((_HPS, W, W), lambda hp, i: (hp, 0, 0)),
                  pl.BlockSpec((HW, W), lambda hp, i: (hp, i))],
        out_specs=pl.BlockSpec((HW, W), lambda hp, i: (hp, i)),
        out_shape=jax.ShapeDtypeStruct((ATT_WIDTH, nb * W), BF16),
        scratch_shapes=[pltpu.VMEM((_HPS, nb, W), F32),
                        pltpu.VMEM((_HPS, _SPLIT, ATT_HEAD_DIM + _ONES_ROWS, W), F32),
                        pltpu.VMEM((_HPS, _SPLIT, SUBLANES, W), F32)],
        compiler_params=_cparams(("parallel", "arbitrary")),
        name="moba_prompt",
    )(cfar, qT3, kb3, vT3, kmean, d0T, d1T, gaT)


def _memkv_kernel(mem_ref, g_ref, w_ref, k_ref, v_ref):
    xn = _rms(mem_ref[...], g_ref[...]).astype(BF16)
    kv = _dot(xn, w_ref[...])
    k_ref[...] = kv[:, :X_WIDTH]
    v_ref[...] = kv[:, X_WIDTH:]


def _memory_kv(mem, g, w):
    nm, d = mem.shape
    full = lambda a: pl.BlockSpec(a.shape, lambda: (0,) * a.ndim)
    g2 = g.reshape(1, d)
    wb = w.astype(BF16)
    return pl.pallas_call(
        _memkv_kernel,
        in_specs=[full(mem), full(g2), full(wb)],
        out_specs=[pl.BlockSpec((nm, X_WIDTH), lambda: (0, 0))] * 2,
        out_shape=[jax.ShapeDtypeStruct((nm, X_WIDTH), F32)] * 2,
        compiler_params=pltpu.CompilerParams(vmem_limit_bytes=VMEM_LIMIT),
        name="memory_kv",
    )(mem, g2, wb)


def _xattn_kernel(q_ref, gx_ref, mk_ref, mv_ref, o_ref):
    q = q_ref[...]
    mk = mk_ref[...].astype(BF16)
    mv = mv_ref[...].astype(BF16)
    outs = []
    for h in range(X_HEADS):
        sl = slice(h * X_HEAD_DIM, (h + 1) * X_HEAD_DIM)
        s = _dot_nt(q[:, sl], mk[:, sl])
        m = jnp.max(s, axis=1, keepdims=True)
        p = jnp.exp(s - m)
        l = jnp.sum(p, axis=1, keepdims=True)
        outs.append(_dot(p.astype(BF16), mv[:, sl]) / l)
    o_ref[...] = (jnp.concatenate(outs, axis=1) * _silu(gx_ref[...])).astype(o_ref.dtype)


def _xattn_prompt(qx, gx, mk, mv, *, tq):
    m = qx.shape[0]
    row = pl.BlockSpec((tq, X_WIDTH), lambda i: (i, 0))
    full = pl.BlockSpec(mk.shape, lambda i: (0, 0))
    return pl.pallas_call(
        _xattn_kernel,
        grid=(m // tq,),
        in_specs=[row, row, full, full],
        out_specs=row,
        out_shape=jax.ShapeDtypeStruct((m, X_WIDTH), BF16),
        compiler_params=_cparams(("parallel",)),
        name="xattn_prompt",
    )(qx, gx, mk, mv)


def _xattn_sample_kernel(q_ref, gx_ref, mk_ref, mv_ref, o_ref):
    q = q_ref[0]
    outs = []
    for h in range(X_HEADS):
        sl = slice(h * X_HEAD_DIM, (h + 1) * X_HEAD_DIM)
        s = _dot_nt((q[:, sl] * (X_HEAD_DIM ** -0.5)).astype(BF16), mk_ref[0, :, h, :].astype(BF16))
        m = jnp.max(s, axis=1, keepdims=True)
        p = jnp.exp(s - m)
        l = jnp.sum(p, axis=1, keepdims=True)
        outs.append(_dot(p.astype(BF16), mv_ref[0, :, h, :].astype(BF16)) / l)
    o_ref[0] = jnp.concatenate(outs, axis=1) * _silu(gx_ref[0])


def _xattn_sample(qx, gx, mk, mv):
    b, t8, _ = qx.shape
    nm = mk.shape[1]
    qs = pl.BlockSpec((1, t8, X_WIDTH), lambda i: (i, 0, 0))
    ms = pl.BlockSpec((1, nm, X_HEADS, X_HEAD_DIM), lambda i: (i, 0, 0, 0))
    return pl.pallas_call(
        _xattn_sample_kernel,
        grid=(b,),
        in_specs=[qs, qs, ms, ms],
        out_specs=qs,
        out_shape=jax.ShapeDtypeStruct((b, t8, X_WIDTH), F32),
        compiler_params=_cparams(("parallel",)),
        name="xattn_sample",
    )(qx, gx, mk, mv)


def _outproj_tail(x_ref, ys_ref, att_proj, yx_ref, w_ref, fw_ref, o_ref):
    h = x_ref[...] + _dot(ys_ref[...].astype(BF16), w_ref[0:SSM_WIDTH, :])
    h = h + att_proj
    h = h + _dot(yx_ref[...].astype(BF16), w_ref[SSM_WIDTH + ATT_WIDTH:, :])
    o_ref[...] = _rms(h, fw_ref[...])


def _outproj_kernel(x_ref, ys_ref, ya_ref, ga_ref, yx_ref, w_ref, fw_ref, o_ref):
    att = (ya_ref[...] * _silu(ga_ref[...])).astype(BF16)
    att_proj = _dot(att, w_ref[SSM_WIDTH:SSM_WIDTH + ATT_WIDTH, :])
    _outproj_tail(x_ref, ys_ref, att_proj, yx_ref, w_ref, fw_ref, o_ref)


def _outproj_t_kernel(x_ref, ys_ref, yaT_ref, yx_ref, w_ref, fw_ref, o_ref):
    att_proj = _dot_tn(yaT_ref[...].astype(BF16), w_ref[SSM_WIDTH:SSM_WIDTH + ATT_WIDTH, :])
    _outproj_tail(x_ref, ys_ref, att_proj, yx_ref, w_ref, fw_ref, o_ref)


def _out_proj(x, y_ssd, y_att, ga, y_x, w_out, final_norm_w, *, tm):
    m, d = x.shape
    row = lambda w: pl.BlockSpec((tm, w), lambda i: (i, 0))
    full = lambda a: pl.BlockSpec(a.shape, lambda i: (0,) * a.ndim)
    wb = w_out.astype(BF16)
    fw = final_norm_w.reshape(1, d)
    if ga is None:
        body, att_args = _outproj_t_kernel, (y_att,)
        att_specs = [pl.BlockSpec((ATT_WIDTH, tm), lambda i: (0, i))]
    else:
        body, att_args = _outproj_kernel, (y_att, ga)
        att_specs = [row(ATT_WIDTH), row(ATT_WIDTH)]
    return pl.pallas_call(
        body,
        grid=(m // tm,),
        in_specs=[row(d), row(SSM_WIDTH)] + att_specs + [row(X_WIDTH), full(wb), full(fw)],
        out_specs=row(d),
        out_shape=jax.ShapeDtypeStruct((m, d), F32),
        compiler_params=_cparams(("parallel",)),
        name="out_proj",
    )(x, y_ssd, *att_args, y_x, wb, fw)


def _ssd_sample_kernel(x_ref, c_ref, dt_ref, z_ref, h0_ref, w_ref, b_ref, dtb_ref, alog_ref, dsk_ref,
                       nw_ref, sel_ref, y_ref, h_ref, y_sc, *, T):
    P = SSM_HEAD_DIM
    N = SSM_STATE
    cr = c_ref[0]
    xr = x_ref[0]
    ext = lambda i: cr[i:i + 1, :] if i < CONV_W - 1 else xr[i - (CONV_W - 1):i - (CONV_W - 2), :]
    w = w_ref[...]
    rows = []
    for t in range(T):
        u = b_ref[...]
        for j in range(CONV_W):
            u = u + w[j:j + 1, :] * ext(t + j)
        rows.append(u)
    u = _silu(jnp.concatenate(rows + [jnp.zeros((SUBLANES - T, CONV_DIM), F32)], axis=0))
    xs = u[:, :SSM_WIDTH]
    dt = _softplus(dt_ref[0] + dtb_ref[...])
    dec = jnp.exp(dt * (-jnp.exp(alog_ref[...])))
    xdt = xs[0:T] * dt
    sel = sel_ref[...]
    bc = [_dot_tn_exactrhs(u[:, SSM_WIDTH + k * N:SSM_WIDTH + (k + 1) * N], sel)
          for k in range(2 * SSM_GROUPS)]
    for pair in range(SSM_HEADS // 2):
        g = (2 * pair) // SSM_HG
        lanes = slice(pair * 2 * P, (pair + 1) * 2 * P)
        hT = jnp.concatenate([h0_ref[0, 2 * pair], h0_ref[0, 2 * pair + 1]], axis=0).T
        for t in range(T):
            tl = slice(t * N, (t + 1) * N)
            hT = hT * dec[t:t + 1, lanes] + bc[g][:, tl] * xdt[t:t + 1, lanes]
            y_sc[t:t + 1, lanes] = jnp.sum(hT * bc[SSM_GROUPS + g][:, tl], axis=0, keepdims=True)
        h2 = hT.T
        h_ref[0, 2 * pair] = h2[:P]
        h_ref[0, 2 * pair + 1] = h2[P:]
    y = y_sc[0:T, :] + dsk_ref[...] * xs[0:T]
    gt = y * _silu(z_ref[0])
    outs = []
    for g in range(SSM_GROUPS):
        gg = gt[:, g * SSM_GW:(g + 1) * SSM_GW]
        outs.append(gg * lax.rsqrt(jnp.mean(gg * gg, axis=-1, keepdims=True) + EPS))
    y_ref[0] = jnp.concatenate(outs, axis=1) * nw_ref[...]


def _dot_tn_exactrhs(a, b_bf16):
    out = None
    for part in _split3(a):
        term = _dot_tn(part, b_bf16)
        out = term if out is None else out + term
    return out


def _ssd_sample(xbc, state_conv, dt_raw, z, state_ssm, conv_w, conv_b, dt_bias, a_log, d_skip, ssm_norm_w):
    b, T, _ = xbc.shape
    per = lambda a: pl.BlockSpec((1,) + a.shape[1:], lambda i: (i,) + (0,) * (a.ndim - 1))
    full = lambda a: pl.BlockSpec(a.shape, lambda i: (0,) * a.ndim)
    per_head = lambda v: jnp.repeat(v, SSM_HEAD_DIM, axis=-1)
    dt_e = per_head(dt_raw)
    br = conv_b.reshape(1, CONV_DIM)
    dtb = per_head(dt_bias).reshape(1, SSM_WIDTH)
    alog = per_head(a_log).reshape(1, SSM_WIDTH)
    dsk = per_head(d_skip).reshape(1, SSM_WIDTH)
    nw = ssm_norm_w.reshape(1, SSM_WIDTH)
    sel = np.zeros((SUBLANES, T * SSM_STATE), np.float32)
    for t in range(T):
        sel[t, t * SSM_STATE:(t + 1) * SSM_STATE] = 1.0
    sel = jnp.asarray(sel, BF16)
    args = (xbc, state_conv, dt_e, z, state_ssm, conv_w, br, dtb, alog, dsk, nw, sel)
    in_specs = [per(a) for a in args[:5]] + [full(a) for a in args[5:]]
    return pl.pallas_call(
        functools.partial(_ssd_sample_kernel, T=T),
        grid=(b,),
        in_specs=in_specs,
        out_specs=[pl.BlockSpec((1, T, SSM_WIDTH), lambda i: (i, 0, 0)), per(state_ssm)],
        out_shape=[jax.ShapeDtypeStruct((b, T, SSM_WIDTH), F32),
                   jax.ShapeDtypeStruct(state_ssm.shape, F32)],
        scratch_shapes=[pltpu.VMEM((SUBLANES, SSM_WIDTH), F32)],
        compiler_params=_cparams(("parallel",)),
        name="ssd_sample",
    )(*args)


_KM_PAGES = 32


def _select_pages_kernel(pt_ref, q_ref, *refs):
    page_refs, o_ref, km_sc = refs[:-2], refs[-2], refs[-1]
    si = pl.program_id(1)
    nb = km_sc.shape[1]

    @pl.when(si == 0)
    def _():
        km_sc[...] = jnp.zeros_like(km_sc)

    lane = lax.broadcasted_iota(jnp.int32, (ATT_WIDTH, nb), 1)
    acc = km_sc[...]
    for b in range(_KM_PAGES // PAGES_PER_BLOCK):
        tok = page_refs[b * PAGES_PER_BLOCK][0].reshape(ATT_WIDTH, PAGE_SIZE)
        for pg in range(1, PAGES_PER_BLOCK):
            tok = tok + page_refs[b * PAGES_PER_BLOCK + pg][0].reshape(ATT_WIDTH, PAGE_SIZE)
        mean = jnp.sum(tok, axis=1, keepdims=True) * (1.0 / MOBA_BLOCK)
        acc = jnp.where(lane == si * (_KM_PAGES // PAGES_PER_BLOCK) + b, mean, acc)
    km_sc[...] = acc

    @pl.when(si == pl.num_programs(1) - 1)
    def _():
        q = q_ref[0]
        t8 = q.shape[0]
        blk = lax.broadcasted_iota(jnp.int32, (t8, nb), 1).astype(F32)
        pick_lane = lax.broadcasted_iota(jnp.int32, (t8, LANES), 1)
        for h in range(ATT_HEADS):
            rows = slice(h * ATT_HEAD_DIM, (h + 1) * ATT_HEAD_DIM)
            s = _dot_hi(q[:, rows], km_sc[rows, :])
            tile = jnp.zeros((t8, LANES), F32)
            for r, (idx, _) in enumerate(_topk_hits(s, blk, 1)):
                tile = jnp.where(pick_lane == r, jnp.minimum(idx, nb - 1.0), tile)
            o_ref[0, h] = tile.astype(jnp.int32)


def _select_sample(q8, pool_kt, page_table):
    b, n_pages = page_table.shape
    t8 = q8.shape[1]
    nb = n_pages // PAGES_PER_BLOCK
    assert nb >= MOBA_TOPK
    steps = n_pages // _KM_PAGES
    page_spec = lambda r: pl.BlockSpec(
        (1, ATT_HEADS, ATT_HEAD_DIM, PAGE_SIZE),
        lambda bi, si, pt: (pt[bi * n_pages + si * _KM_PAGES + r], 0, 0, 0))
    grid_spec = pltpu.PrefetchScalarGridSpec(
        num_scalar_prefetch=1,
        grid=(b, steps),
        in_specs=[pl.BlockSpec((1, t8, ATT_WIDTH), lambda bi, si, pt: (bi, 0, 0))]
        + [page_spec(r) for r in range(_KM_PAGES)],
        out_specs=pl.BlockSpec((1, ATT_HEADS, t8, LANES), lambda bi, si, pt: (bi, 0, 0, 0)),
        scratch_shapes=[pltpu.VMEM((ATT_WIDTH, nb), F32)],
    )
    return pl.pallas_call(
        _select_pages_kernel,
        grid_spec=grid_spec,
        out_shape=jax.ShapeDtypeStruct((b, ATT_HEADS, t8, LANES), jnp.int32),
        compiler_params=_cparams(("parallel", "arbitrary")),
        name="moba_select_pages",
    )(page_table.reshape(-1), q8, *([pool_kt] * _KM_PAGES))


_N_SLAB = MOBA_TOPK * PAGES_PER_BLOCK


def _attend_sample_kernel(sel_ref, pt_ref, rb_ref, q_ref, kn_ref, vn_ref, near_ref, pk_ref, pv_ref,
                          o_ref, kbuf, vbuf, sem, *, T, n_pages):
    bi = pl.program_id(0)
    t = pl.program_id(1)
    step = bi * T + t
    nsteps = pl.num_programs(0) * T
    nb = n_pages // PAGES_PER_BLOCK
    D = ATT_HEAD_DIM

    def copies(s, slot):
        b_s = s // T
        out = []
        for h in range(ATT_HEADS):
            for r in range(MOBA_TOPK):
                blk = sel_ref[(s * ATT_HEADS + h) * MOBA_TOPK + r]
                for pg in range(PAGES_PER_BLOCK):
                    phys = pt_ref[b_s * n_pages + blk * PAGES_PER_BLOCK + pg]
                    i = r * PAGES_PER_BLOCK + pg
                    out.append(pltpu.make_async_copy(pk_ref.at[phys, h], kbuf.at[slot, h, i], sem.at[0, slot]))
                    out.append(pltpu.make_async_copy(pv_ref.at[phys, h], vbuf.at[slot, h, i], sem.at[1, slot]))
        return out

    @pl.when(step == 0)
    def _():
        for cp in copies(step, 0):
            cp.start()

    @pl.when(step + 1 < nsteps)
    def _():
        for cp in copies(step + 1, (step + 1) % 2):
            cp.start()

    slot = step % 2
    for cp in copies(step, slot):
        cp.wait()

    first_row = jnp.where(lax.broadcasted_iota(jnp.int32, (SUBLANES, PAGE_SIZE), 0) == 0, 1.0, 0.0).astype(BF16)
    q8 = jnp.concatenate([q_ref[0], jnp.zeros((SUBLANES - 1, ATT_WIDTH), F32)], axis=0)
    qb = _dot_tn_exactrhs(q8, first_row)
    lane_t = lax.broadcasted_iota(jnp.int32, (1, T), 1)
    neg_tail = jnp.full((1, PAGE_SIZE - T), NEG, F32)
    neg_row = jnp.full((1, PAGE_SIZE), NEG, F32)

    tiles = []
    for h in range(ATT_HEADS):
        rows = slice(h * D, (h + 1) * D)
        cfar = rb_ref[(RPE_BUCKETS - 1) * ATT_HEADS + h]
        s_rows = []
        for r in range(MOBA_TOPK):
            blk = sel_ref[(step * ATT_HEADS + h) * MOBA_TOPK + r]
            for pg in range(PAGES_PER_BLOCK):
                s = jnp.sum(kbuf[slot, h, r * PAGES_PER_BLOCK + pg] * qb[rows], axis=0, keepdims=True)
                s_rows.append(s + jnp.where(blk == nb - 1, near_ref[h, 0, pg:pg + 1, :], cfar))
        s_own = jnp.sum(kn_ref[0, rows, :] * qb[rows, 0:T], axis=0, keepdims=True)
        b_own = jnp.zeros((1, T), F32)
        for tp in range(T):
            b_own = jnp.where(lane_t == tp, rb_ref[jnp.maximum(t - tp, 0) * ATT_HEADS + h], b_own)
        s_own = jnp.where(lane_t <= t, s_own + b_own, NEG)
        tiles.append(jnp.concatenate(s_rows + [jnp.concatenate([s_own, neg_tail], axis=1), neg_row], axis=0))

    ones8 = jnp.ones((SUBLANES, PAGE_SIZE), BF16)
    zero_tail = jnp.zeros((D, PAGE_SIZE - T), F32)
    outs = []
    for h, s in enumerate(tiles):
        rows = slice(h * D, (h + 1) * D)
        m = jnp.max(jnp.max(s, axis=1, keepdims=True), axis=0, keepdims=True)
        p = jnp.exp(s - m)
        l = jnp.sum(jnp.sum(p, axis=1, keepdims=True), axis=0, keepdims=True)
        acc = jnp.concatenate([vn_ref[0, rows, :] * p[_N_SLAB:_N_SLAB + 1, 0:T], zero_tail], axis=1)
        for n in range(_N_SLAB):
            acc = acc + vbuf[slot, h, n] * p[n:n + 1, :]
        o = None
        for part in _split3(acc):
            term = _dot_nt(ones8, part)
            o = term if o is None else o + term
        outs.append(o[0:1, :] / l)
    o_ref[0] = jnp.concatenate(outs, axis=1)


def _attend_sample(sel, page_table, rel_bias, q_col, k_newT, v_newT, near, pool_kt, pool_vt):
    b, _, T = k_newT.shape
    n_pages = page_table.shape[1]
    assert T <= RPE_BUCKETS // 2
    slab = (2, ATT_HEADS, _N_SLAB, ATT_HEAD_DIM, PAGE_SIZE)
    grid_spec = pltpu.PrefetchScalarGridSpec(
        num_scalar_prefetch=3,
        grid=(b, T),
        in_specs=[pl.BlockSpec((1, 1, ATT_WIDTH), lambda bi, t, *_: (bi * T + t, 0, 0)),
                  pl.BlockSpec((1, ATT_WIDTH, T), lambda bi, t, *_: (bi, 0, 0)),
                  pl.BlockSpec((1, ATT_WIDTH, T), lambda bi, t, *_: (bi, 0, 0)),
                  pl.BlockSpec((ATT_HEADS, 1, PAGES_PER_BLOCK, PAGE_SIZE), lambda bi, t, *_: (0, t, 0, 0)),
                  pl.BlockSpec(memory_space=pl.ANY),
                  pl.BlockSpec(memory_space=pl.ANY)],
        out_specs=pl.BlockSpec((1, 1, ATT_WIDTH), lambda bi, t, *_: (bi * T + t, 0, 0)),
        scratch_shapes=[pltpu.VMEM(slab, F32), pltpu.VMEM(slab, F32), pltpu.SemaphoreType.DMA((2, 2))],
    )
    return pl.pallas_call(
        functools.partial(_attend_sample_kernel, T=T, n_pages=n_pages),
        grid_spec=grid_spec,
        out_shape=jax.ShapeDtypeStruct((b * T, 1, ATT_WIDTH), F32),
        compiler_params=_cparams(("arbitrary", "arbitrary")),
        name="moba_attend_sample",
    )(sel, page_table.reshape(-1), rel_bias.reshape(-1), q_col, k_newT, v_newT, near, pool_kt, pool_vt)


_PROMPT_ROW_TILE = 1024


def _prompt_layer(x, mem, norm_w, w_in, conv_w, conv_b, dt_bias, a_log, d_skip, ssm_norm_w,
                  mem_norm_w, w_mem_kv, w_out, rel_bias, final_norm_w):
    s = x.shape[0]
    W = MOBA_BLOCK
    z, xbc, gaT, qx, gx, kb3, km3, qT3, kT, vT, vT3, dtT = _in_proj_prompt(x, norm_w, w_in)
    y_ssd, state = _ssd_prompt(xbc, dtT, z, conv_w, conv_b, dt_bias, a_log, d_skip, ssm_norm_w)
    ki = np.arange(W)[:, None]
    qi = np.arange(W)[None, :]
    own_map = np.where(ki <= qi, _bucket_np(qi - ki), -1)
    d01 = _bias_table(rel_bias, np.stack([own_map, _bucket_np(W + qi - ki)]), LOG2E)
    y_attT = _moba_prompt(qT3, kb3, vT3, km3.reshape(-1, ATT_WIDTH), d01[:, 0], d01[:, 1], gaT, rel_bias)
    mk, mv = _memory_kv(mem, mem_norm_w, w_mem_kv)
    y_x = _xattn_prompt(qx, gx, mk, mv, tq=_PROMPT_ROW_TILE)
    y = _out_proj(x, y_ssd, y_attT, None, y_x, w_out, final_norm_w, tm=_PROMPT_ROW_TILE)
    conv_state = xbc[s - (CONV_W - 1):]
    heads = lambda aT: aT.reshape(ATT_HEADS, ATT_HEAD_DIM, s).transpose(2, 0, 1)
    return y, heads(kT), heads(vT), mk, mv, state, conv_state


def _sample_layer(x, cache_k, cache_v, page_table, mem_k, mem_v, state_ssm, state_conv, norm_w, w_in,
                  conv_w, conv_b, dt_bias, a_log, d_skip, ssm_norm_w, w_out, rel_bias, final_norm_w):
    b, T, d = x.shape
    n_pages = page_table.shape[1]
    assert (n_pages * PAGE_SIZE) % MOBA_BLOCK == 0 and n_pages % _KM_PAGES == 0
    assert CONV_W - 1 <= T <= SUBLANES
    xf = x.reshape(b * T, d)
    z, xbc, q, k, v, ga, qx, gx, dtT = _in_proj_sample(xf, norm_w, w_in)
    per = lambda a: a.reshape(b, T, a.shape[-1])
    y_ssd, h_new = _ssd_sample(per(xbc), state_conv, per(dtT.T), per(z), state_ssm,
                               conv_w, conv_b, dt_bias, a_log, d_skip, ssm_norm_w)
    pad8 = lambda a: jnp.pad(per(a), ((0, 0), (0, SUBLANES - T), (0, 0)))
    pool_kt = cache_k.transpose(0, 2, 3, 1)
    pool_vt = cache_v.transpose(0, 2, 3, 1)
    sel = _select_sample(pad8(q), pool_kt, page_table)[:, :, :T, :MOBA_TOPK]
    sel = sel.transpose(0, 2, 1, 3).reshape(-1)
    kk = (np.arange(PAGES_PER_BLOCK)[:, None] * PAGE_SIZE + np.arange(PAGE_SIZE)[None, :])[None]
    tt = np.arange(T)[:, None, None]
    near = _bias_table(rel_bias, _bucket_np(MOBA_BLOCK + tt - kk))
    q_col = (q * (ATT_HEAD_DIM ** -0.5)).reshape(b * T, 1, ATT_WIDTH)
    y_att = _attend_sample(sel, page_table, rel_bias, q_col, per(k).transpose(0, 2, 1),
                           per(v).transpose(0, 2, 1), near, pool_kt, pool_vt)
    y_att = y_att.reshape(b * T, ATT_WIDTH)
    y_x = _xattn_sample(pad8(qx), pad8(gx), mem_k, mem_v)
    y_x = y_x[:, :T].reshape(b * T, X_WIDTH)
    y = _out_proj(xf, y_ssd.reshape(b * T, SSM_WIDTH), y_att, ga, y_x, w_out, final_norm_w, tm=b * T)
    conv_state = per(xbc)[:, T - (CONV_W - 1):]
    return y.reshape(b, T, d), per(k), per(v), h_new, conv_state


def kernel(x_prompt, x_sample, mem_prompt, cache_k, cache_v, page_table, cache_mem_k, cache_mem_v,
           state_ssm, state_conv, norm_w, w_in, conv_w, conv_b, dt_bias, a_log, d_skip, ssm_norm_w,
           mem_norm_w, w_mem_kv, w_out, rel_bias, final_norm_w):
    bp, s, d = x_prompt.shape
    bs, T, _ = x_sample.shape
    depth = w_in.shape[0]
    assert bp == 1 and depth == 1
    l = 0
    y_p, k_p, v_p, mk, mv, ssm_p, conv_p = _prompt_layer(
        x_prompt[0], mem_prompt[0], norm_w[l], w_in[l], conv_w[l], conv_b[l], dt_bias[l], a_log[l],
        d_skip[l], ssm_norm_w[l], mem_norm_w[l], w_mem_kv[l], w_out[l], rel_bias, final_norm_w)
    y_s, k_s, v_s, ssm_s, conv_s = _sample_layer(
        x_sample, cache_k[l], cache_v[l], page_table, cache_mem_k[l], cache_mem_v[l], state_ssm[l],
        state_conv[l], norm_w[l], w_in[l], conv_w[l], conv_b[l], dt_bias[l], a_log[l], d_skip[l],
        ssm_norm_w[l], w_out[l], rel_bias, final_norm_w)
    nm = mem_prompt.shape[1]
    return (y_p[None],
            y_s,
            k_p[None, None],
            v_p[None, None],
            mk.reshape(1, 1, nm, X_HEADS, X_HEAD_DIM),
            mv.reshape(1, 1, nm, X_HEADS, X_HEAD_DIM),
            ssm_p[None, None],
            conv_p[None, None],
            k_s.reshape(1, bs, T, ATT_HEADS, ATT_HEAD_DIM),
            v_s.reshape(1, bs, T, ATT_HEADS, ATT_HEAD_DIM),
            ssm_s[None],
            conv_s[None])
```

```python
import functools
import math

import numpy as np
import jax
import jax.numpy as jnp
from jax import lax
from jax.experimental import pallas as pl
from jax.experimental.pallas import tpu as pltpu

F32 = jnp.float32
BF16 = jnp.bfloat16
EPS = 1e-6

SSM_HEADS = 16
SSM_HEAD_DIM = 64
SSM_GROUPS = 2
SSM_HG = SSM_HEADS // SSM_GROUPS
SSM_STATE = 128
SSM_WIDTH = SSM_HEADS * SSM_HEAD_DIM
SSM_GW = SSM_WIDTH // SSM_GROUPS
SSM_CHUNK = 128
CONV_W = 4
CONV_DIM = SSM_WIDTH + 2 * SSM_GROUPS * SSM_STATE
ATT_HEADS = 8
ATT_HEAD_DIM = 64
ATT_WIDTH = ATT_HEADS * ATT_HEAD_DIM
MOBA_BLOCK = 256
MOBA_TOPK = 3
PAGE_SIZE = 128
PAGES_PER_BLOCK = MOBA_BLOCK // PAGE_SIZE
RPE_BUCKETS = 32
RPE_MAX_DIST = 128
X_HEADS = 4
X_HEAD_DIM = 128
X_WIDTH = X_HEADS * X_HEAD_DIM

LANES = 128
SUBLANES = 8
VMEM_LIMIT = 56 * 1024 * 1024
NEG = -1e30
BIG = 2.0 ** 100
LOG2E = math.log2(math.e)


def _cparams(sem, vmem=VMEM_LIMIT):
    return pltpu.CompilerParams(dimension_semantics=sem, vmem_limit_bytes=vmem)


def _split2(x):
    hi = x.astype(BF16)
    lo = (x - hi.astype(F32)).astype(BF16)
    return hi, lo


def _split3(x):
    hi = x.astype(BF16)
    r = x - hi.astype(F32)
    mid = r.astype(BF16)
    lo = (r - mid.astype(F32)).astype(BF16)
    return hi, mid, lo


def _dot(a, b):
    return jnp.dot(a, b, preferred_element_type=F32)


def _dot_nt(a, b):
    return lax.dot_general(a, b, (((1,), (1,)), ((), ())), preferred_element_type=F32)


def _dot_tn(a, b):
    return lax.dot_general(a, b, (((0,), (0,)), ((), ())), preferred_element_type=F32)


def _dot_f32_exactrhs(a, b_bf16, passes=3):
    parts = _split3(a) if passes == 3 else _split2(a)
    out = _dot(parts[0], b_bf16)
    for p in parts[1:]:
        out = out + _dot(p, b_bf16)
    return out


def _dot_hi(a, b):
    ah, al = _split2(a)
    bh, bl = _split2(b)
    return _dot(ah, bh) + _dot(al, bh) + _dot(ah, bl)


def _silu(x):
    return x / (1.0 + jnp.exp(-x))


def _softplus(x):
    return jnp.maximum(x, 0.0) + jnp.log1p(jnp.exp(-jnp.abs(x)))


def _rms(x, g):
    ms = jnp.mean(x * x, axis=-1, keepdims=True)
    return (x * lax.rsqrt(ms + EPS)) * g


_MAIN_SPLITS = (SSM_WIDTH, CONV_DIM, ATT_WIDTH, ATT_WIDTH, ATT_WIDTH, ATT_WIDTH, X_WIDTH, X_WIDTH)


_W_ROW_CHUNK = 512


def _inproj_kernel(x_ref, nw_ref, wT_ref, z_ref, xbc_ref, q_ref, k_ref, v_ref, ga_ref, qx_ref, gx_ref, dtT_ref):
    xn = _rms(x_ref[...], nw_ref[...])
    xh, xl = _split2(xn)

    def proj(lo, width):
        wh, wl = _split2(wT_ref[lo:lo + width, :])
        return _dot_nt(xh, wh) + _dot_nt(xl, wh) + _dot_nt(xh, wl)

    outs = (z_ref, xbc_ref, None, q_ref, k_ref, v_ref, ga_ref, qx_ref, gx_ref)
    lo = 0
    for o_ref, width in zip(outs, _IN_SPLITS):
        if o_ref is None:
            wh, wl = _split2(wT_ref[lo:lo + width, :])
            dtT_ref[...] = _dot_nt(wh, xh) + _dot_nt(wh, xl) + _dot_nt(wl, xh)
        else:
            for c in range(0, width, _W_ROW_CHUNK):
                o_ref[:, c:c + _W_ROW_CHUNK] = proj(lo + c, _W_ROW_CHUNK)
        lo += width


_IN_SPLITS = (SSM_WIDTH, CONV_DIM, SSM_HEADS, ATT_WIDTH, ATT_WIDTH, ATT_WIDTH, ATT_WIDTH, X_WIDTH, X_WIDTH)


def _in_proj_sample(x, norm_w, w_in):
    m, d = x.shape
    wT = w_in.T
    full = lambda a: pl.BlockSpec(a.shape, lambda: (0,) * a.ndim)
    out_shapes = [jax.ShapeDtypeStruct((m, w), F32) for w in _MAIN_SPLITS]
    out_shapes += [jax.ShapeDtypeStruct((SSM_HEADS, m), F32)]
    nw = norm_w.reshape(1, d)
    return pl.pallas_call(
        _inproj_kernel,
        in_specs=[full(x), full(nw), full(wT)],
        out_specs=[pl.BlockSpec(s.shape, lambda: (0, 0)) for s in out_shapes],
        out_shape=out_shapes,
        compiler_params=pltpu.CompilerParams(vmem_limit_bytes=VMEM_LIMIT),
        name="in_proj_sample",
    )(x, nw, wT)


_PROMPT_NN = (SSM_WIDTH, CONV_DIM, X_WIDTH, X_WIDTH, ATT_WIDTH)


def _inproj_prompt_kernel(x_ref, nw_ref, wn_ref, wt_ref, z_ref, xbc_ref, gaT_ref, qx_ref, gx_ref, kb_ref,
                          km_ref, qT_ref, kT_ref, vT_ref, vTb_ref, dtT_ref):
    xh = _rms(x_ref[...], nw_ref[...]).astype(BF16)
    lo = 0
    for o_ref, width in zip((z_ref, xbc_ref, qx_ref, gx_ref), _PROMPT_NN[:-1]):
        val = _dot(xh, wn_ref[:, lo:lo + width])
        if o_ref is qx_ref:
            val = (val * (X_HEAD_DIM ** -0.5)).astype(BF16)
        o_ref[...] = val
        lo += width
    k = _dot(xh, wn_ref[:, lo:lo + ATT_WIDTH])
    kb_ref[0] = k.astype(BF16)
    km_ref[0] = jnp.mean(k, axis=0, keepdims=True)
    qT_ref[0] = _dot_nt(wt_ref[0:ATT_WIDTH, :], xh)
    kT_ref[...] = _dot_nt(wt_ref[ATT_WIDTH:2 * ATT_WIDTH, :], xh)
    vT = _dot_nt(wt_ref[2 * ATT_WIDTH:3 * ATT_WIDTH, :], xh)
    vT_ref[...] = vT
    vTb_ref[0] = vT.astype(BF16)
    gaT_ref[...] = _dot_nt(wt_ref[3 * ATT_WIDTH:4 * ATT_WIDTH, :], xh)
    dtT_ref[...] = _dot_nt(wt_ref[4 * ATT_WIDTH:, :], xh)


def _in_proj_prompt(x, norm_w, w_in):
    m, d = x.shape
    tm = MOBA_BLOCK
    nb = m // tm
    c_z, c_dt = 0, SSM_WIDTH + CONV_DIM
    c_q = c_dt + SSM_HEADS
    c_k, c_qx = c_q + ATT_WIDTH, c_q + 4 * ATT_WIDTH
    cols = lambda lo, w: w_in[:, lo:lo + w]
    wn = jnp.concatenate([cols(c_z, SSM_WIDTH + CONV_DIM), cols(c_qx, 2 * X_WIDTH),
                          cols(c_k, ATT_WIDTH)], axis=1).astype(BF16)
    wt = jnp.concatenate([cols(c_q, 4 * ATT_WIDTH), cols(c_dt, SSM_HEADS)], axis=1).T.astype(BF16)
    nw = norm_w.reshape(1, d)
    row = lambda w: pl.BlockSpec((tm, w), lambda i: (i, 0))
    col = lambda h: pl.BlockSpec((h, tm), lambda i: (0, i))
    full = lambda a: pl.BlockSpec(a.shape, lambda i: (0,) * a.ndim)
    tile3 = lambda a, b: pl.BlockSpec((1, a, b), lambda i: (i, 0, 0))
    out_specs = [row(SSM_WIDTH), row(CONV_DIM), col(ATT_WIDTH), row(X_WIDTH), row(X_WIDTH),
                 tile3(tm, ATT_WIDTH), tile3(1, ATT_WIDTH), tile3(ATT_WIDTH, tm),
                 col(ATT_WIDTH), col(ATT_WIDTH), tile3(ATT_WIDTH, tm), col(SSM_HEADS)]
    sds = jax.ShapeDtypeStruct
    out_shape = [sds((m, SSM_WIDTH), F32), sds((m, CONV_DIM), F32), sds((ATT_WIDTH, m), F32),
                 sds((m, X_WIDTH), BF16), sds((m, X_WIDTH), F32),
                 sds((nb, tm, ATT_WIDTH), BF16), sds((nb, 1, ATT_WIDTH), F32), sds((nb, ATT_WIDTH, tm), F32),
                 sds((ATT_WIDTH, m), F32), sds((ATT_WIDTH, m), F32), sds((nb, ATT_WIDTH, tm), BF16),
                 sds((SSM_HEADS, m), F32)]
    return pl.pallas_call(
        _inproj_prompt_kernel,
        grid=(nb,),
        in_specs=[row(d), full(nw), full(wn), full(wt)],
        out_specs=out_specs,
        out_shape=out_shape,
        compiler_params=_cparams(("parallel",)),
        name="in_proj_prompt",
    )(x, nw, wn, wt)


_SSD_CHUNKS = 2
def _ssd_kernel(xbc_ref, dtT_ref, z_ref, convw_ref, convb_ref, dtb_ref, alog_ref, dskip_ref, nw_ref,
                e_ref, y_ref, hT_ref, ext_sc, h_sc):
    c = pl.program_id(0)
    L = SSM_CHUNK

    @pl.when(c == 0)
    def _():
        ext_sc[0:SUBLANES, :] = jnp.zeros((SUBLANES, CONV_DIM), F32)
        h_sc[...] = jnp.zeros_like(h_sc)

    for sub in range(_SSD_CHUNKS):
        rows = slice(sub * L, (sub + 1) * L)
        x = xbc_ref[rows, :]
        ext_sc[SUBLANES:SUBLANES + L, :] = x
        w = convw_ref[...]
        acc = x * w[CONV_W - 1:CONV_W, :] + convb_ref[...]
        for k in range(1, CONV_W):
            acc = acc + ext_sc[SUBLANES - k:SUBLANES - k + L, :] * w[CONV_W - 1 - k:CONV_W - k, :]
        ext_sc[0:SUBLANES, :] = x[L - SUBLANES:L]
        u = _silu(acc)
        xs = u[:, :SSM_WIDTH]
        bm = u[:, SSM_WIDTH:SSM_WIDTH + SSM_GROUPS * SSM_STATE]
        cm = u[:, SSM_WIDTH + SSM_GROUPS * SSM_STATE:]

        dtT = _softplus(dtT_ref[:, rows] + dtb_ref[...])
        aT = dtT * (-jnp.exp(alog_ref[...]))
        r_i = lax.broadcasted_iota(jnp.int32, (L, L), 0)
        c_i = lax.broadcasted_iota(jnp.int32, (L, L), 1)
        upper = (r_i <= c_i).astype(BF16)
        acumT = _dot_f32_exactrhs(aT, upper)
        a_last = acumT[:, L - 1:L]
        to_endT = jnp.exp(a_last - acumT) * dtT
        eacT = jnp.exp(acumT)
        ac2T = acumT * LOG2E
        stack = jnp.concatenate(
            [ac2T, dtT, to_endT, eacT, jnp.zeros((L - 4 * SSM_HEADS, L), F32)], axis=0)
        st = stack.T
        e = e_ref[...]
        dt_e = _dot_f32_exactrhs(st[:, SSM_HEADS:2 * SSM_HEADS], e, 2)
        toend_e = _dot_f32_exactrhs(st[:, 2 * SSM_HEADS:3 * SSM_HEADS], e, 2)
        eac_e = _dot_f32_exactrhs(st[:, 3 * SSM_HEADS:4 * SSM_HEADS], e, 2)
        cd_e = eac_e[L - 1:L, :]

        xd = (xs * dt_e).astype(BF16)
        xw = (xs * toend_e).astype(BF16)
        causal = r_i >= c_i
        ys = []
        yoffs = []
        for g in range(SSM_GROUPS):
            bg = bm[:, g * SSM_STATE:(g + 1) * SSM_STATE].astype(BF16)
            cg = cm[:, g * SSM_STATE:(g + 1) * SSM_STATE].astype(BF16)
            cb = jnp.where(causal, _dot_nt(cg, bg), 0.0)
            for hh in range(SSM_HG):
                h = g * SSM_HG + hh
                seg = st[:, h:h + 1] - ac2T[h:h + 1, :]
                mix = (cb * jnp.exp2(jnp.minimum(seg, 0.0))).astype(BF16)
                ys.append(_dot(mix, xd[:, h * SSM_HEAD_DIM:(h + 1) * SSM_HEAD_DIM]))
            hprev = h_sc[g]
            yoffs.append(_dot(cg, hprev.astype(BF16)))
            s_new = _dot_tn(bg, xw[:, g * SSM_GW:(g + 1) * SSM_GW])
            h_sc[g] = hprev * cd_e[:, g * SSM_GW:(g + 1) * SSM_GW] + s_new

        y = jnp.concatenate(ys, axis=1) + jnp.concatenate(yoffs, axis=1) * eac_e + dskip_ref[...] * xs
        gt = y * _silu(z_ref[rows, :])
        outs = []
        for g in range(SSM_GROUPS):
            gg = gt[:, g * SSM_GW:(g + 1) * SSM_GW]
            outs.append(gg * lax.rsqrt(jnp.mean(gg * gg, axis=-1, keepdims=True) + EPS))
        y_ref[rows, :] = (jnp.concatenate(outs, axis=1) * nw_ref[...]).astype(y_ref.dtype)

    @pl.when(c == pl.num_programs(0) - 1)
    def _():
        hT_ref[...] = h_sc[...]


def _head_expand():
    return jnp.asarray(np.repeat(np.eye(SSM_HEADS, dtype=np.float32), SSM_HEAD_DIM, axis=1), BF16)


def _ssd_prompt(xbc, dtT, z, conv_w, conv_b, dt_bias, a_log, d_skip, ssm_norm_w):
    m = xbc.shape[0]
    L = SSM_CHUNK
    step = L * _SSD_CHUNKS
    row = lambda w: pl.BlockSpec((step, w), lambda i: (i, 0))
    full = lambda a: pl.BlockSpec(a.shape, lambda i: (0,) * a.ndim)
    convb = conv_b.reshape(1, CONV_DIM)
    dtb = dt_bias.reshape(SSM_HEADS, 1)
    alog = a_log.reshape(SSM_HEADS, 1)
    dskip = jnp.repeat(d_skip, SSM_HEAD_DIM).reshape(1, SSM_WIDTH)
    nw = ssm_norm_w.reshape(1, SSM_WIDTH)
    e = _head_expand()
    y, hT = pl.pallas_call(
        _ssd_kernel,
        grid=(m // step,),
        in_specs=[row(CONV_DIM), pl.BlockSpec((SSM_HEADS, step), lambda i: (0, i)), row(SSM_WIDTH),
                  full(conv_w), full(convb), full(dtb), full(alog), full(dskip), full(nw), full(e)],
        out_specs=[row(SSM_WIDTH), pl.BlockSpec((SSM_GROUPS, SSM_STATE, SSM_GW), lambda i: (0, 0, 0))],
        out_shape=[jax.ShapeDtypeStruct((m, SSM_WIDTH), BF16),
                   jax.ShapeDtypeStruct((SSM_GROUPS, SSM_STATE, SSM_GW), F32)],
        scratch_shapes=[pltpu.VMEM((SUBLANES + L, CONV_DIM), F32),
                        pltpu.VMEM((SSM_GROUPS, SSM_STATE, SSM_GW), F32)],
        compiler_params=_cparams(("arbitrary",)),
        name="ssd_prompt",
    )(xbc, dtT, z, conv_w, convb, dtb, alog, dskip, nw, e)
    state = hT.reshape(SSM_GROUPS, SSM_STATE, SSM_HG, SSM_HEAD_DIM).transpose(0, 2, 3, 1)
    return y, state.reshape(SSM_HEADS, SSM_HEAD_DIM, SSM_STATE)


def _bucket_np(dist):
    n = np.maximum(dist, 0)
    exact = RPE_BUCKETS // 2
    nf = np.maximum(n, 1).astype(np.float32)
    scaled = (np.log(nf / np.float32(exact)) / np.float32(math.log(RPE_MAX_DIST / exact))
              * np.float32(RPE_BUCKETS - exact))
    large = exact + scaled.astype(np.int32)
    return np.where(n < exact, n, np.minimum(large, RPE_BUCKETS - 1)).astype(np.int32)


def _bias_kernel(rb_ref, map_ref, out_ref, *, scale):
    h = pl.program_id(0)
    bmap = map_ref[...]
    acc = jnp.full(bmap.shape, -2.0 * BIG, F32)
    for b in range(RPE_BUCKETS):
        acc = jnp.where(bmap == b, rb_ref[b * ATT_HEADS + h] * scale, acc)
    out_ref[0] = acc


def _bias_table(rel_bias, bucket_map, scale=1.0):
    bmap = jnp.asarray(bucket_map, jnp.int32)
    nd = bmap.ndim
    return pl.pallas_call(
        functools.partial(_bias_kernel, scale=scale),
        grid=(ATT_HEADS,),
        in_specs=[pl.BlockSpec(memory_space=pltpu.SMEM),
                  pl.BlockSpec(bmap.shape, lambda h: (0,) * nd)],
        out_specs=pl.BlockSpec((1,) + bmap.shape, lambda h: (h,) + (0,) * nd),
        out_shape=jax.ShapeDtypeStruct((ATT_HEADS,) + bmap.shape, F32),
        compiler_params=_cparams(("parallel",)),
        name="rel_bias_table",
    )(rel_bias.reshape(-1), bmap)


def _topk_hits(s, blk, axis):
    picks = []
    for _ in range(MOBA_TOPK):
        mx = jnp.max(s, axis=axis, keepdims=True)
        idx = jnp.min(jnp.where(s == mx, blk, float(2 * LANES)), axis=axis, keepdims=True)
        picks.append((idx, jnp.where(mx > -jnp.inf, 1.0, 0.0)))
        s = jnp.where(blk == idx, -jnp.inf, s)
    return picks


_HPS = 8
_K_HEADS = 4
_SPLIT = 4
_ONES_ROWS = 16


def _moba_kernel(cfar_ref, qT_ref, kb_ref, vT_ref, km_ref, d0_ref, d1_ref, gaT_ref, o_ref,
                 pen_sc, acc_sc, m_sc):
    hp = pl.program_id(0)
    i = pl.program_id(1)
    W = MOBA_BLOCK
    D = ATT_HEAD_DIM
    nb = km_ref.shape[0]
    blk = lax.broadcasted_iota(jnp.int32, (nb, W), 0).astype(F32)
    i_f = i.astype(F32)
    near = jnp.maximum(i - 1, 0)
    q_augs, cfars = [], []
    acc_sc[...] = jnp.zeros_like(acc_sc)
    for hh in range(_HPS):
        rows = slice(hh * D, (hh + 1) * D)
        cfar2 = cfar_ref[hp * _HPS + hh] * LOG2E
        qT = qT_ref[0, rows, :]
        sT = _dot_hi(km_ref[:, rows], qT)
        sT = jnp.where(blk < i_f, sT, -jnp.inf)
        hit = jnp.zeros((nb, W), F32)
        for idx, valid in _topk_hits(sT, blk, 0):
            hit = jnp.maximum(hit, jnp.where(blk == idx, valid, 0.0))
        pen_sc[hh] = jnp.where(hit > 0.0, cfar2, -2.0 * BIG)

        qs = (qT * (D ** -0.5 * LOG2E)).astype(BF16)
        parts = [jnp.zeros((D, W), BF16)] * _K_HEADS
        parts[hh % _K_HEADS] = qs
        q_augs.append(jnp.concatenate(parts, axis=0))
        cfars.append(cfar2)

    m_sc[...] = jnp.full(m_sc.shape, -BIG, F32)
    ones = jnp.ones((_ONES_ROWS, W), BF16)

    def scores(hh, j, bias_ref):
        g = hh // _K_HEADS
        lanes = slice(g * _K_HEADS * D, (g + 1) * _K_HEADS * D)
        s = _dot(kb_ref[j, :, lanes], q_augs[hh])
        return s if bias_ref is None else s + bias_ref[hh]

    def absorb(chains, ss):
        ps = []
        for (hh, a, j, _, pen), s in zip(chains, ss):
            m = m_sc[hh, a, 0:1, :]
            m_new = jnp.maximum(m, jnp.max(s, axis=0, keepdims=True) + pen)
            m_sc[hh, a, 0:1, :] = m_new
            ps.append((jnp.exp2(m - m_new), jnp.exp2(s - (m_new - pen)).astype(BF16)))
        for (hh, a, j, _, _), (alpha, p) in zip(chains, ps):
            v1 = jnp.concatenate([vT_ref[j, hh * D:(hh + 1) * D, :], ones], axis=0)
            acc_sc[hh, a] = alpha * acc_sc[hh, a] + _dot(v1, p)

    def group(chains):
        absorb(chains, [scores(hh, j, bias_ref) for hh, _, j, bias_ref, _ in chains])

    zero_row = jnp.zeros((1, W), F32)
    group([(hh, 0, i, d0_ref, zero_row) for hh in range(_HPS)]
          + [(hh, 1, near, d1_ref, pen_sc[hh, pl.ds(near, 1), :] - cfars[hh]) for hh in range(_HPS)])

    def far_chains(first, count):
        return [(hh, a, first + a, None, pen_sc[hh, pl.ds(first + a, 1), :])
                for a in range(count) for hh in range(_HPS)]

    def far_group(first, count):
        group(far_chains(first, count))

    def far_body(trip, carry):
        far_group(_SPLIT * trip, _SPLIT)
        return carry

    lax.fori_loop(0, near // _SPLIT, far_body, 0)
    done = (near // _SPLIT) * _SPLIT
    count = _SPLIT // 2
    while count >= 1:
        take = (near & count) != 0

        @pl.when(take)
        def _(done=done, count=count):
            far_group(done, count)

        done = done + jnp.where(take, count, 0)
        count //= 2

    outs = []
    for hh in range(_HPS):
        m = m_sc[hh, 0, 0:1, :]
        for a in range(1, _SPLIT):
            m = jnp.maximum(m, m_sc[hh, a, 0:1, :])
        acc = jnp.zeros((D + _ONES_ROWS, W), F32)
        for a in range(_SPLIT):
            acc = acc + jnp.exp2(m_sc[hh, a, 0:1, :] - m) * acc_sc[hh, a]
        outs.append(acc[:D] / acc[D:D + 1])
    o_ref[...] = (jnp.concatenate(outs, axis=0) * _silu(gaT_ref[...])).astype(o_ref.dtype)


def _moba_prompt(qT3, kb3, vT3, kmean, d0T, d1T, gaT, rel_bias):
    nb = qT3.shape[0]
    W = MOBA_BLOCK
    HW = _HPS * ATT_HEAD_DIM
    assert nb - 1 >= MOBA_TOPK and nb <= LANES and ATT_HEADS % _HPS == 0 and HW % LANES == 0
    cfar = rel_bias[RPE_BUCKETS - 1]
    return pl.pallas_call(
        _moba_kernel,
        grid=(ATT_HEADS // _HPS, nb),
        in_specs=[pl.BlockSpec(memory_space=pltpu.SMEM),
                  pl.BlockSpec((1, HW, W), lambda hp, i: (i, hp, 0)),
                  pl.BlockSpec((nb, W, HW), lambda hp, i: (0, 0, hp), pipeline_mode=pl.Buffered(1)),
                  pl.BlockSpec((nb, HW, W), lambda hp, i: (0, hp, 0), pipeline_mode=pl.Buffered(1)),
                  pl.BlockSpec((nb, HW), lambda hp, i: (0, hp)),
                  pl.BlockSpec((_HPS, W, W), lambda hp, i: (hp, 0, 0)),
                  pl.BlockSpec((_HPS, W, W), lambda hp, i: (hp, 0, 0)),
                  pl.BlockSpec((HW, W), lambda hp, i: (hp, i))],
        out_specs=pl.BlockSpec((HW, W), lambda hp, i: (hp, i)),
        out_shape=jax.ShapeDtypeStruct((ATT_WIDTH, nb * W), BF16),
        scratch_shapes=[pltpu.VMEM((_HPS, nb, W), F32),
                        pltpu.VMEM((_HPS, _SPLIT, ATT_HEAD_DIM + _ONES_ROWS, W), F32),
                        pltpu.VMEM((_HPS, _SPLIT, SUBLANES, W), F32)],
        compiler_params=_cparams(("parallel", "arbitrary")),
        name="moba_prompt",
    )(cfar, qT3, kb3, vT3, kmean, d0T, d1T, gaT)


def _memkv_kernel(mem_ref, g_ref, w_ref, k_ref, v_ref):
    xn = _rms(mem_ref[...], g_ref[...]).astype(BF16)
    kv = _dot(xn, w_ref[...])
    k_ref[...] = kv[:, :X_WIDTH]
    v_ref[...] = kv[:, X_WIDTH:]


def _memory_kv(mem, g, w):
    nm, d = mem.shape
    full = lambda a: pl.BlockSpec(a.shape, lambda: (0,) * a.ndim)
    g2 = g.reshape(1, d)
    wb = w.astype(BF16)
    return pl.pallas_call(
        _memkv_kernel,
        in_specs=[full(mem), full(g2), full(wb)],
        out_specs=[pl.BlockSpec((nm, X_WIDTH), lambda: (0, 0))] * 2,
        out_shape=[jax.ShapeDtypeStruct((nm, X_WIDTH), F32)] * 2,
        compiler_params=pltpu.CompilerParams(vmem_limit_bytes=VMEM_LIMIT),
        name="memory_kv",
    )(mem, g2, wb)


def _xattn_kernel(q_ref, gx_ref, mk_ref, mv_ref, o_ref):
    q = q_ref[...]
    mk = mk_ref[...].astype(BF16)
    mv = mv_ref[...].astype(BF16)
    outs = []
    for h in range(X_HEADS):
        sl = slice(h * X_HEAD_DIM, (h + 1) * X_HEAD_DIM)
        s = _dot_nt(q[:, sl], mk[:, sl])
        m = jnp.max(s, axis=1, keepdims=True)
        p = jnp.exp(s - m)
        l = jnp.sum(p, axis=1, keepdims=True)
        outs.append(_dot(p.astype(BF16), mv[:, sl]) / l)
    o_ref[...] = (jnp.concatenate(outs, axis=1) * _silu(gx_ref[...])).astype(o_ref.dtype)


def _xattn_prompt(qx, gx, mk, mv, *, tq):
    m = qx.shape[0]
    row = pl.BlockSpec((tq, X_WIDTH), lambda i: (i, 0))
    full = pl.BlockSpec(mk.shape, lambda i: (0, 0))
    return pl.pallas_call(
        _xattn_kernel,
        grid=(m // tq,),
        in_specs=[row, row, full, full],
        out_specs=row,
        out_shape=jax.ShapeDtypeStruct((m, X_WIDTH), BF16),
        compiler_params=_cparams(("parallel",)),
        name="xattn_prompt",
    )(qx, gx, mk, mv)


def _xattn_sample_kernel(q_ref, gx_ref, mk_ref, mv_ref, o_ref):
    q = q_ref[0]
    outs = []
    for h in range(X_HEADS):
        sl = slice(h * X_HEAD_DIM, (h + 1) * X_HEAD_DIM)
        s = _dot_nt((q[:, sl] * (X_HEAD_DIM ** -0.5)).astype(BF16), mk_ref[0, :, h, :].astype(BF16))
        m = jnp.max(s, axis=1, keepdims=True)
        p = jnp.exp(s - m)
        l = jnp.sum(p, axis=1, keepdims=True)
        outs.append(_dot(p.astype(BF16), mv_ref[0, :, h, :].astype(BF16)) / l)
    o_ref[0] = jnp.concatenate(outs, axis=1) * _silu(gx_ref[0])


def _xattn_sample(qx, gx, mk, mv):
    b, t8, _ = qx.shape
    nm = mk.shape[1]
    qs = pl.BlockSpec((1, t8, X_WIDTH), lambda i: (i, 0, 0))
    ms = pl.BlockSpec((1, nm, X_HEADS, X_HEAD_DIM), lambda i: (i, 0, 0, 0))
    return pl.pallas_call(
        _xattn_sample_kernel,
        grid=(b,),
        in_specs=[qs, qs, ms, ms],
        out_specs=qs,
        out_shape=jax.ShapeDtypeStruct((b, t8, X_WIDTH), F32),
        compiler_params=_cparams(("parallel",)),
        name="xattn_sample",
    )(qx, gx, mk, mv)


def _outproj_tail(x_ref, ys_ref, att_proj, yx_ref, w_ref, fw_ref, o_ref):
    h = x_ref[...] + _dot(ys_ref[...].astype(BF16), w_ref[0:SSM_WIDTH, :])
    h = h + att_proj
    h = h + _dot(yx_ref[...].astype(BF16), w_ref[SSM_WIDTH + ATT_WIDTH:, :])
    o_ref[...] = _rms(h, fw_ref[...])


def _outproj_kernel(x_ref, ys_ref, ya_ref, ga_ref, yx_ref, w_ref, fw_ref, o_ref):
    att = (ya_ref[...] * _silu(ga_ref[...])).astype(BF16)
    att_proj = _dot(att, w_ref[SSM_WIDTH:SSM_WIDTH + ATT_WIDTH, :])
    _outproj_tail(x_ref, ys_ref, att_proj, yx_ref, w_ref, fw_ref, o_ref)


def _outproj_t_kernel(x_ref, ys_ref, yaT_ref, yx_ref, w_ref, fw_ref, o_ref):
    att_proj = _dot_tn(yaT_ref[...].astype(BF16), w_ref[SSM_WIDTH:SSM_WIDTH + ATT_WIDTH, :])
    _outproj_tail(x_ref, ys_ref, att_proj, yx_ref, w_ref, fw_ref, o_ref)


def _out_proj(x, y_ssd, y_att, ga, y_x, w_out, final_norm_w, *, tm):
    m, d = x.shape
    row = lambda w: pl.BlockSpec((tm, w), lambda i: (i, 0))
    full = lambda a: pl.BlockSpec(a.shape, lambda i: (0,) * a.ndim)
    wb = w_out.astype(BF16)
    fw = final_norm_w.reshape(1, d)
    if ga is None:
        body, att_args = _outproj_t_kernel, (y_att,)
        att_specs = [pl.BlockSpec((ATT_WIDTH, tm), lambda i: (0, i))]
    else:
        body, att_args = _outproj_kernel, (y_att, ga)
        att_specs = [row(ATT_WIDTH), row(ATT_WIDTH)]
    return pl.pallas_call(
        body,
        grid=(m // tm,),
        in_specs=[row(d), row(SSM_WIDTH)] + att_specs + [row(X_WIDTH), full(wb), full(fw)],
        out_specs=row(d),
        out_shape=jax.ShapeDtypeStruct((m, d), F32),
        compiler_params=_cparams(("parallel",)),
        name="out_proj",
    )(x, y_ssd, *att_args, y_x, wb, fw)


def _ssd_sample_kernel(x_ref, c_ref, dt_ref, z_ref, h0_ref, w_ref, b_ref, dtb_ref, alog_ref, dsk_ref,
                       nw_ref, sel_ref, y_ref, h_ref, y_sc, *, T):
    P = SSM_HEAD_DIM
    N = SSM_STATE
    cr = c_ref[0]
    xr = x_ref[0]
    ext = lambda i: cr[i:i + 1, :] if i < CONV_W - 1 else xr[i - (CONV_W - 1):i - (CONV_W - 2), :]
    w = w_ref[...]
    rows = []
    for t in range(T):
        u = b_ref[...]
        for j in range(CONV_W):
            u = u + w[j:j + 1, :] * ext(t + j)
        rows.append(u)
    u = _silu(jnp.concatenate(rows + [jnp.zeros((SUBLANES - T, CONV_DIM), F32)], axis=0))
    xs = u[:, :SSM_WIDTH]
    dt = _softplus(dt_ref[0] + dtb_ref[...])
    dec = jnp.exp(dt * (-jnp.exp(alog_ref[...])))
    xdt = xs[0:T] * dt
    sel = sel_ref[...]
    bc = [_dot_tn_exactrhs(u[:, SSM_WIDTH + k * N:SSM_WIDTH + (k + 1) * N], sel)
          for k in range(2 * SSM_GROUPS)]
    for pair in range(SSM_HEADS // 2):
        g = (2 * pair) // SSM_HG
        lanes = slice(pair * 2 * P, (pair + 1) * 2 * P)
        hT = jnp.concatenate([h0_ref[0, 2 * pair], h0_ref[0, 2 * pair + 1]], axis=0).T
        for t in range(T):
            tl = slice(t * N, (t + 1) * N)
            hT = hT * dec[t:t + 1, lanes] + bc[g][:, tl] * xdt[t:t + 1, lanes]
            y_sc[t:t + 1, lanes] = jnp.sum(hT * bc[SSM_GROUPS + g][:, tl], axis=0, keepdims=True)
        h2 = hT.T
        h_ref[0, 2 * pair] = h2[:P]
        h_ref[0, 2 * pair + 1] = h2[P:]
    y = y_sc[0:T, :] + dsk_ref[...] * xs[0:T]
    gt = y * _silu(z_ref[0])
    outs = []
    for g in range(SSM_GROUPS):
        gg = gt[:, g * SSM_GW:(g + 1) * SSM_GW]
        outs.append(gg * lax.rsqrt(jnp.mean(gg * gg, axis=-1, keepdims=True) + EPS))
    y_ref[0] = jnp.concatenate(outs, axis=1) * nw_ref[...]


def _dot_tn_exactrhs(a, b_bf16):
    out = None
    for part in _split3(a):
        term = _dot_tn(part, b_bf16)
        out = term if out is None else out + term
    return out


def _ssd_sample(xbc, state_conv, dt_raw, z, state_ssm, conv_w, conv_b, dt_bias, a_log, d_skip, ssm_norm_w):
    b, T, _ = xbc.shape
    per = lambda a: pl.BlockSpec((1,) + a.shape[1:], lambda i: (i,) + (0,) * (a.ndim - 1))
    full = lambda a: pl.BlockSpec(a.shape, lambda i: (0,) * a.ndim)
    per_head = lambda v: jnp.repeat(v, SSM_HEAD_DIM, axis=-1)
    dt_e = per_head(dt_raw)
    br = conv_b.reshape(1, CONV_DIM)
    dtb = per_head(dt_bias).reshape(1, SSM_WIDTH)
    alog = per_head(a_log).reshape(1, SSM_WIDTH)
    dsk = per_head(d_skip).reshape(1, SSM_WIDTH)
    nw = ssm_norm_w.reshape(1, SSM_WIDTH)
    sel = np.zeros((SUBLANES, T * SSM_STATE), np.float32)
    for t in range(T):
        sel[t, t * SSM_STATE:(t + 1) * SSM_STATE] = 1.0
    sel = jnp.asarray(sel, BF16)
    args = (xbc, state_conv, dt_e, z, state_ssm, conv_w, br, dtb, alog, dsk, nw, sel)
    in_specs = [per(a) for a in args[:5]] + [full(a) for a in args[5:]]
    return pl.pallas_call(
        functools.partial(_ssd_sample_kernel, T=T),
        grid=(b,),
        in_specs=in_specs,
        out_specs=[pl.BlockSpec((1, T, SSM_WIDTH), lambda i: (i, 0, 0)), per(state_ssm)],
        out_shape=[jax.ShapeDtypeStruct((b, T, SSM_WIDTH), F32),
                   jax.ShapeDtypeStruct(state_ssm.shape, F32)],
        scratch_shapes=[pltpu.VMEM((SUBLANES, SSM_WIDTH), F32)],
        compiler_params=_cparams(("parallel",)),
        name="ssd_sample",
    )(*args)


_KM_PAGES = 64


def _select_pages_kernel(pt_ref, q_ref, *refs):
    page_refs, o_ref, km_sc = refs[:-2], refs[-2], refs[-1]
    si = pl.program_id(1)
    nb = km_sc.shape[1]

    @pl.when(si == 0)
    def _():
        km_sc[...] = jnp.zeros_like(km_sc)

    lane = lax.broadcasted_iota(jnp.int32, (ATT_WIDTH, nb), 1)
    acc = km_sc[...]
    for b in range(_KM_PAGES // PAGES_PER_BLOCK):
        tok = page_refs[b * PAGES_PER_BLOCK][0].reshape(ATT_WIDTH, PAGE_SIZE)
        for pg in range(1, PAGES_PER_BLOCK):
            tok = tok + page_refs[b * PAGES_PER_BLOCK + pg][0].reshape(ATT_WIDTH, PAGE_SIZE)
        mean = jnp.sum(tok, axis=1, keepdims=True) * (1.0 / MOBA_BLOCK)
        acc = jnp.where(lane == si * (_KM_PAGES // PAGES_PER_BLOCK) + b, mean, acc)
    km_sc[...] = acc

    @pl.when(si == pl.num_programs(1) - 1)
    def _():
        q = q_ref[0]
        t8 = q.shape[0]
        blk = lax.broadcasted_iota(jnp.int32, (t8, nb), 1).astype(F32)
        pick_lane = lax.broadcasted_iota(jnp.int32, (t8, LANES), 1)
        for h in range(ATT_HEADS):
            rows = slice(h * ATT_HEAD_DIM, (h + 1) * ATT_HEAD_DIM)
            s = _dot_hi(q[:, rows], km_sc[rows, :])
            tile = jnp.zeros((t8, LANES), F32)
            for r, (idx, _) in enumerate(_topk_hits(s, blk, 1)):
                tile = jnp.where(pick_lane == r, jnp.minimum(idx, nb - 1.0), tile)
            o_ref[0, h] = tile.astype(jnp.int32)


def _select_sample(q8, pool_kt, page_table):
    b, n_pages = page_table.shape
    t8 = q8.shape[1]
    nb = n_pages // PAGES_PER_BLOCK
    assert nb >= MOBA_TOPK
    steps = n_pages // _KM_PAGES
    page_spec = lambda r: pl.BlockSpec(
        (1, ATT_HEADS, ATT_HEAD_DIM, PAGE_SIZE),
        lambda bi, si, pt: (pt[bi * n_pages + si * _KM_PAGES + r], 0, 0, 0))
    grid_spec = pltpu.PrefetchScalarGridSpec(
        num_scalar_prefetch=1,
        grid=(b, steps),
        in_specs=[pl.BlockSpec((1, t8, ATT_WIDTH), lambda bi, si, pt: (bi, 0, 0))]
        + [page_spec(r) for r in range(_KM_PAGES)],
        out_specs=pl.BlockSpec((1, ATT_HEADS, t8, LANES), lambda bi, si, pt: (bi, 0, 0, 0)),
        scratch_shapes=[pltpu.VMEM((ATT_WIDTH, nb), F32)],
    )
    return pl.pallas_call(
        _select_pages_kernel,
        grid_spec=grid_spec,
        out_shape=jax.ShapeDtypeStruct((b, ATT_HEADS, t8, LANES), jnp.int32),
        compiler_params=_cparams(("parallel", "arbitrary")),
        name="moba_select_pages",
    )(page_table.reshape(-1), q8, *([pool_kt] * _KM_PAGES))


_N_SLAB = MOBA_TOPK * PAGES_PER_BLOCK


def _attend_sample_kernel(sel_ref, pt_ref, rb_ref, q_ref, kn_ref, vn_ref, near_ref, pk_ref, pv_ref,
                          o_ref, kbuf, vbuf, sem, *, T, n_pages):
    bi = pl.program_id(0)
    t = pl.program_id(1)
    step = bi * T + t
    nsteps = pl.num_programs(0) * T
    nb = n_pages // PAGES_PER_BLOCK
    D = ATT_HEAD_DIM

    def copies(s, slot):
        b_s = s // T
        out = []
        for h in range(ATT_HEADS):
            for r in range(MOBA_TOPK):
                blk = sel_ref[(s * ATT_HEADS + h) * MOBA_TOPK + r]
                for pg in range(PAGES_PER_BLOCK):
                    phys = pt_ref[b_s * n_pages + blk * PAGES_PER_BLOCK + pg]
                    i = r * PAGES_PER_BLOCK + pg
                    out.append(pltpu.make_async_copy(pk_ref.at[phys, h], kbuf.at[slot, h, i], sem.at[0, slot]))
                    out.append(pltpu.make_async_copy(pv_ref.at[phys, h], vbuf.at[slot, h, i], sem.at[1, slot]))
        return out

    @pl.when(step == 0)
    def _():
        for cp in copies(step, 0):
            cp.start()

    @pl.when(step + 1 < nsteps)
    def _():
        for cp in copies(step + 1, (step + 1) % 2):
            cp.start()

    slot = step % 2
    pltpu.make_async_copy(kbuf.at[slot], kbuf.at[slot], sem.at[0, slot]).wait()
    pltpu.make_async_copy(vbuf.at[slot], vbuf.at[slot], sem.at[1, slot]).wait()

    first_row = jnp.where(lax.broadcasted_iota(jnp.int32, (SUBLANES, PAGE_SIZE), 0) == 0, 1.0, 0.0).astype(BF16)
    q8 = jnp.concatenate([q_ref[0], jnp.zeros((SUBLANES - 1, ATT_WIDTH), F32)], axis=0)
    qb = _dot_tn_exactrhs(q8, first_row)
    lane_t = lax.broadcasted_iota(jnp.int32, (1, T), 1)
    neg_tail = jnp.full((1, PAGE_SIZE - T), NEG, F32)
    neg_row = jnp.full((1, PAGE_SIZE), NEG, F32)

    tiles = []
    for h in range(ATT_HEADS):
        rows = slice(h * D, (h + 1) * D)
        cfar = rb_ref[(RPE_BUCKETS - 1) * ATT_HEADS + h]
        s_rows = []
        for r in range(MOBA_TOPK):
            blk = sel_ref[(step * ATT_HEADS + h) * MOBA_TOPK + r]
            for pg in range(PAGES_PER_BLOCK):
                s = jnp.sum(kbuf[slot, h, r * PAGES_PER_BLOCK + pg] * qb[rows], axis=0, keepdims=True)
                s_rows.append(s + jnp.where(blk == nb - 1, near_ref[h, 0, pg:pg + 1, :], cfar))
        s_own = jnp.sum(kn_ref[0, rows, :] * qb[rows, 0:T], axis=0, keepdims=True)
        b_own = jnp.zeros((1, T), F32)
        for tp in range(T):
            b_own = jnp.where(lane_t == tp, rb_ref[jnp.maximum(t - tp, 0) * ATT_HEADS + h], b_own)
        s_own = jnp.where(lane_t <= t, s_own + b_own, NEG)
        tiles.append(jnp.concatenate(s_rows + [jnp.concatenate([s_own, neg_tail], axis=1), neg_row], axis=0))

    ones8 = jnp.ones((SUBLANES, PAGE_SIZE), BF16)
    zero_tail = jnp.zeros((D, PAGE_SIZE - T), F32)
    outs = []
    for h, s in enumerate(tiles):
        rows = slice(h * D, (h + 1) * D)
        m = jnp.max(jnp.max(s, axis=1, keepdims=True), axis=0, keepdims=True)
        p = jnp.exp(s - m)
        l = jnp.sum(jnp.sum(p, axis=1, keepdims=True), axis=0, keepdims=True)
        acc = jnp.concatenate([vn_ref[0, rows, :] * p[_N_SLAB:_N_SLAB + 1, 0:T], zero_tail], axis=1)
        for n in range(_N_SLAB):
            acc = acc + vbuf[slot, h, n] * p[n:n + 1, :]
        o = None
        for part in _split3(acc):
            term = _dot_nt(ones8, part)
            o = term if o is None else o + term
        outs.append(o[0:1, :] / l)
    o_ref[0] = jnp.concatenate(outs, axis=1)


def _attend_sample(sel, page_table, rel_bias, q_col, k_newT, v_newT, near, pool_kt, pool_vt):
    b, _, T = k_newT.shape
    n_pages = page_table.shape[1]
    assert T <= RPE_BUCKETS // 2
    slab = (2, ATT_HEADS, _N_SLAB, ATT_HEAD_DIM, PAGE_SIZE)
    grid_spec = pltpu.PrefetchScalarGridSpec(
        num_scalar_prefetch=3,
        grid=(b, T),
        in_specs=[pl.BlockSpec((1, 1, ATT_WIDTH), lambda bi, t, *_: (bi * T + t, 0, 0)),
                  pl.BlockSpec((1, ATT_WIDTH, T), lambda bi, t, *_: (bi, 0, 0)),
                  pl.BlockSpec((1, ATT_WIDTH, T), lambda bi, t, *_: (bi, 0, 0)),
                  pl.BlockSpec((ATT_HEADS, 1, PAGES_PER_BLOCK, PAGE_SIZE), lambda bi, t, *_: (0, t, 0, 0)),
                  pl.BlockSpec(memory_space=pl.ANY),
                  pl.BlockSpec(memory_space=pl.ANY)],
        out_specs=pl.BlockSpec((1, 1, ATT_WIDTH), lambda bi, t, *_: (bi * T + t, 0, 0)),
        scratch_shapes=[pltpu.VMEM(slab, F32), pltpu.VMEM(slab, F32), pltpu.SemaphoreType.DMA((2, 2))],
    )
    return pl.pallas_call(
        functools.partial(_attend_sample_kernel, T=T, n_pages=n_pages),
        grid_spec=grid_spec,
        out_shape=jax.ShapeDtypeStruct((b * T, 1, ATT_WIDTH), F32),
        compiler_params=_cparams(("arbitrary", "arbitrary")),
        name="moba_attend_sample",
    )(sel, page_table.reshape(-1), rel_bias.reshape(-1), q_col, k_newT, v_newT, near, pool_kt, pool_vt)


_PROMPT_ROW_TILE = 1024


def _prompt_layer(x, mem, norm_w, w_in, conv_w, conv_b, dt_bias, a_log, d_skip, ssm_norm_w,
                  mem_norm_w, w_mem_kv, w_out, rel_bias, final_norm_w):
    s = x.shape[0]
    W = MOBA_BLOCK
    z, xbc, gaT, qx, gx, kb3, km3, qT3, kT, vT, vT3, dtT = _in_proj_prompt(x, norm_w, w_in)
    y_ssd, state = _ssd_prompt(xbc, dtT, z, conv_w, conv_b, dt_bias, a_log, d_skip, ssm_norm_w)
    ki = np.arange(W)[:, None]
    qi = np.arange(W)[None, :]
    own_map = np.where(ki <= qi, _bucket_np(qi - ki), -1)
    d01 = _bias_table(rel_bias, np.stack([own_map, _bucket_np(W + qi - ki)]), LOG2E)
    y_attT = _moba_prompt(qT3, kb3, vT3, km3.reshape(-1, ATT_WIDTH), d01[:, 0], d01[:, 1], gaT, rel_bias)
    mk, mv = _memory_kv(mem, mem_norm_w, w_mem_kv)
    y_x = _xattn_prompt(qx, gx, mk, mv, tq=_PROMPT_ROW_TILE)
    y = _out_proj(x, y_ssd, y_attT, None, y_x, w_out, final_norm_w, tm=_PROMPT_ROW_TILE)
    conv_state = xbc[s - (CONV_W - 1):]
    heads = lambda aT: aT.reshape(ATT_HEADS, ATT_HEAD_DIM, s).transpose(2, 0, 1)
    return y, heads(kT), heads(vT), mk, mv, state, conv_state


def _sample_layer(x, cache_k, cache_v, page_table, mem_k, mem_v, state_ssm, state_conv, norm_w, w_in,
                  conv_w, conv_b, dt_bias, a_log, d_skip, ssm_norm_w, w_out, rel_bias, final_norm_w):
    b, T, d = x.shape
    n_pages = page_table.shape[1]
    assert (n_pages * PAGE_SIZE) % MOBA_BLOCK == 0 and n_pages % _KM_PAGES == 0
    assert CONV_W - 1 <= T <= SUBLANES
    xf = x.reshape(b * T, d)
    z, xbc, q, k, v, ga, qx, gx, dtT = _in_proj_sample(xf, norm_w, w_in)
    per = lambda a: a.reshape(b, T, a.shape[-1])
    y_ssd, h_new = _ssd_sample(per(xbc), state_conv, per(dtT.T), per(z), state_ssm,
                               conv_w, conv_b, dt_bias, a_log, d_skip, ssm_norm_w)
    pad8 = lambda a: jnp.pad(per(a), ((0, 0), (0, SUBLANES - T), (0, 0)))
    pool_kt = cache_k.transpose(0, 2, 3, 1)
    pool_vt = cache_v.transpose(0, 2, 3, 1)
    sel = _select_sample(pad8(q), pool_kt, page_table)[:, :, :T, :MOBA_TOPK]
    sel = sel.transpose(0, 2, 1, 3).reshape(-1)
    kk = (np.arange(PAGES_PER_BLOCK)[:, None] * PAGE_SIZE + np.arange(PAGE_SIZE)[None, :])[None]
    tt = np.arange(T)[:, None, None]
    near = _bias_table(rel_bias, _bucket_np(MOBA_BLOCK + tt - kk))
    q_col = (q * (ATT_HEAD_DIM ** -0.5)).reshape(b * T, 1, ATT_WIDTH)
    y_att = _attend_sample(sel, page_table, rel_bias, q_col, per(k).transpose(0, 2, 1),
                           per(v).transpose(0, 2, 1), near, pool_kt, pool_vt)
    y_att = y_att.reshape(b * T, ATT_WIDTH)
    y_x = _xattn_sample(pad8(qx), pad8(gx), mem_k, mem_v)
    y_x = y_x[:, :T].reshape(b * T, X_WIDTH)
    y = _out_proj(xf, y_ssd.reshape(b * T, SSM_WIDTH), y_att, ga, y_x, w_out, final_norm_w, tm=b * T)
    conv_state = per(xbc)[:, T - (CONV_W - 1):]
    return y.reshape(b, T, d), per(k), per(v), h_new, conv_state


def kernel(x_prompt, x_sample, mem_prompt, cache_k, cache_v, page_table, cache_mem_k, cache_mem_v,
           state_ssm, state_conv, norm_w, w_in, conv_w, conv_b, dt_bias, a_log, d_skip, ssm_norm_w,
           mem_norm_w, w_mem_kv, w_out, rel_bias, final_norm_w):
    bp, s, d = x_prompt.shape
    bs, T, _ = x_sample.shape
    depth = w_in.shape[0]
    assert bp == 1 and depth == 1
    l = 0
    y_p, k_p, v_p, mk, mv, ssm_p, conv_p = _prompt_layer(
        x_prompt[0], mem_prompt[0], norm_w[l], w_in[l], conv_w[l], conv_b[l], dt_bias[l], a_log[l],
        d_skip[l], ssm_norm_w[l], mem_norm_w[l], w_mem_kv[l], w_out[l], rel_bias, final_norm_w)
    y_s, k_s, v_s, ssm_s, conv_s = _sample_layer(
        x_sample, cache_k[l], cache_v[l], page_table, cache_mem_k[l], cache_mem_v[l], state_ssm[l],
        state_conv[l], norm_w[l], w_in[l], conv_w[l], conv_b[l], dt_bias[l], a_log[l], d_skip[l],
        ssm_norm_w[l], w_out[l], rel_bias, final_norm_w)
    nm = mem_prompt.shape[1]
    return (y_p[None],
            y_s,
            k_p[None, None],
            v_p[None, None],
            mk.reshape(1, 1, nm, X_HEADS, X_HEAD_DIM),
            mv.reshape(1, 1, nm, X_HEADS, X_HEAD_DIM),
            ssm_p[None, None],
            conv_p[None, None],
            k_s.reshape(1, bs, T, ATT_HEADS, ATT_HEAD_DIM),
            v_s.reshape(1, bs, T, ATT_HEADS, ATT_HEAD_DIM),
            ssm_s[None],
            conv_s[None])
```

```python
import functools
import math

import numpy as np
import jax
import jax.numpy as jnp
from jax import lax
from jax.experimental import pallas as pl
from jax.experimental.pallas import tpu as pltpu

F32 = jnp.float32
BF16 = jnp.bfloat16
EPS = 1e-6

SSM_HEADS = 16
SSM_HEAD_DIM = 64
SSM_GROUPS = 2
SSM_HG = SSM_HEADS // SSM_GROUPS
SSM_STATE = 128
SSM_WIDTH = SSM_HEADS * SSM_HEAD_DIM
SSM_GW = SSM_WIDTH // SSM_GROUPS
SSM_CHUNK = 128
CONV_W = 4
CONV_DIM = SSM_WIDTH + 2 * SSM_GROUPS * SSM_STATE
ATT_HEADS = 8
ATT_HEAD_DIM = 64
ATT_WIDTH = ATT_HEADS * ATT_HEAD_DIM
MOBA_BLOCK = 256
MOBA_TOPK = 3
PAGE_SIZE = 128
PAGES_PER_BLOCK = MOBA_BLOCK // PAGE_SIZE
RPE_BUCKETS = 32
RPE_MAX_DIST = 128
X_HEADS = 4
X_HEAD_DIM = 128
X_WIDTH = X_HEADS * X_HEAD_DIM

LANES = 128
SUBLANES = 8
VMEM_LIMIT = 56 * 1024 * 1024
NEG = -1e30
BIG = 2.0 ** 100
LOG2E = math.log2(math.e)


def _cparams(sem, vmem=VMEM_LIMIT):
    return pltpu.CompilerParams(dimension_semantics=sem, vmem_limit_bytes=vmem)


def _split2(x):
    hi = x.astype(BF16)
    lo = (x - hi.astype(F32)).astype(BF16)
    return hi, lo


def _split3(x):
    hi = x.astype(BF16)
    r = x - hi.astype(F32)
    mid = r.astype(BF16)
    lo = (r - mid.astype(F32)).astype(BF16)
    return hi, mid, lo


def _dot(a, b):
    return jnp.dot(a, b, preferred_element_type=F32)


def _dot_nt(a, b):
    return lax.dot_general(a, b, (((1,), (1,)), ((), ())), preferred_element_type=F32)


def _dot_tn(a, b):
    return lax.dot_general(a, b, (((0,), (0,)), ((), ())), preferred_element_type=F32)


def _dot_f32_exactrhs(a, b_bf16, passes=3):
    parts = _split3(a) if passes == 3 else _split2(a)
    out = _dot(parts[0], b_bf16)
    for p in parts[1:]:
        out = out + _dot(p, b_bf16)
    return out


def _dot_hi(a, b):
    ah, al = _split2(a)
    bh, bl = _split2(b)
    return _dot(ah, bh) + _dot(al, bh) + _dot(ah, bl)


def _silu(x):
    return x / (1.0 + jnp.exp(-x))


def _softplus(x):
    return jnp.maximum(x, 0.0) + jnp.log1p(jnp.exp(-jnp.abs(x)))


def _rms(x, g):
    ms = jnp.mean(x * x, axis=-1, keepdims=True)
    return (x * lax.rsqrt(ms + EPS)) * g


_MAIN_SPLITS = (SSM_WIDTH, CONV_DIM, ATT_WIDTH, ATT_WIDTH, ATT_WIDTH, ATT_WIDTH, X_WIDTH, X_WIDTH)


_W_ROW_CHUNK = 512


def _inproj_kernel(x_ref, nw_ref, wT_ref, z_ref, xbc_ref, q_ref, k_ref, v_ref, ga_ref, qx_ref, gx_ref, dtT_ref):
    xn = _rms(x_ref[...], nw_ref[...])
    xh, xl = _split2(xn)

    def proj(lo, width):
        wh, wl = _split2(wT_ref[lo:lo + width, :])
        return _dot_nt(xh, wh) + _dot_nt(xl, wh) + _dot_nt(xh, wl)

    outs = (z_ref, xbc_ref, None, q_ref, k_ref, v_ref, ga_ref, qx_ref, gx_ref)
    lo = 0
    for o_ref, width in zip(outs, _IN_SPLITS):
        if o_ref is None:
            wh, wl = _split2(wT_ref[lo:lo + width, :])
            dtT_ref[...] = _dot_nt(wh, xh) + _dot_nt(wh, xl) + _dot_nt(wl, xh)
        else:
            for c in range(0, width, _W_ROW_CHUNK):
                o_ref[:, c:c + _W_ROW_CHUNK] = proj(lo + c, _W_ROW_CHUNK)
        lo += width


_IN_SPLITS = (SSM_WIDTH, CONV_DIM, SSM_HEADS, ATT_WIDTH, ATT_WIDTH, ATT_WIDTH, ATT_WIDTH, X_WIDTH, X_WIDTH)


def _in_proj_sample(x, norm_w, w_in):
    m, d = x.shape
    wT = w_in.T
    full = lambda a: pl.BlockSpec(a.shape, lambda: (0,) * a.ndim)
    out_shapes = [jax.ShapeDtypeStruct((m, w), F32) for w in _MAIN_SPLITS]
    out_shapes += [jax.ShapeDtypeStruct((SSM_HEADS, m), F32)]
    nw = norm_w.reshape(1, d)
    return pl.pallas_call(
        _inproj_kernel,
        in_specs=[full(x), full(nw), full(wT)],
        out_specs=[pl.BlockSpec(s.shape, lambda: (0, 0)) for s in out_shapes],
        out_shape=out_shapes,
        compiler_params=pltpu.CompilerParams(vmem_limit_bytes=VMEM_LIMIT),
        name="in_proj_sample",
    )(x, nw, wT)


_PROMPT_NN = (SSM_WIDTH, CONV_DIM, X_WIDTH, X_WIDTH, ATT_WIDTH)


def _inproj_prompt_kernel(x_ref, nw_ref, wn_ref, wt_ref, z_ref, xbc_ref, gaT_ref, qx_ref, gx_ref, kb_ref,
                          km_ref, qT_ref, kT_ref, vT_ref, vTb_ref, dtT_ref):
    xh = _rms(x_ref[...], nw_ref[...]).astype(BF16)
    lo = 0
    for o_ref, width in zip((z_ref, xbc_ref, qx_ref, gx_ref), _PROMPT_NN[:-1]):
        val = _dot(xh, wn_ref[:, lo:lo + width])
        if o_ref is qx_ref:
            val = (val * (X_HEAD_DIM ** -0.5)).astype(BF16)
        o_ref[...] = val
        lo += width
    k = _dot(xh, wn_ref[:, lo:lo + ATT_WIDTH])
    kb_ref[0] = k.astype(BF16)
    km_ref[0] = jnp.mean(k, axis=0, keepdims=True)
    qT_ref[0] = _dot_nt(wt_ref[0:ATT_WIDTH, :], xh)
    kT_ref[...] = k.T
    vT = _dot_nt(wt_ref[2 * ATT_WIDTH:3 * ATT_WIDTH, :], xh)
    vT_ref[...] = vT
    vTb_ref[0] = vT.astype(BF16)
    gaT_ref[...] = _dot_nt(wt_ref[3 * ATT_WIDTH:4 * ATT_WIDTH, :], xh)
    dtT_ref[...] = _dot_nt(wt_ref[4 * ATT_WIDTH:, :], xh)


def _in_proj_prompt(x, norm_w, w_in):
    m, d = x.shape
    tm = MOBA_BLOCK
    nb = m // tm
    c_z, c_dt = 0, SSM_WIDTH + CONV_DIM
    c_q = c_dt + SSM_HEADS
    c_k, c_qx = c_q + ATT_WIDTH, c_q + 4 * ATT_WIDTH
    cols = lambda lo, w: w_in[:, lo:lo + w]
    wn = jnp.concatenate([cols(c_z, SSM_WIDTH + CONV_DIM), cols(c_qx, 2 * X_WIDTH),
                          cols(c_k, ATT_WIDTH)], axis=1).astype(BF16)
    wt = jnp.concatenate([cols(c_q, 4 * ATT_WIDTH), cols(c_dt, SSM_HEADS)], axis=1).T.astype(BF16)
    nw = norm_w.reshape(1, d)
    row = lambda w: pl.BlockSpec((tm, w), lambda i: (i, 0))
    col = lambda h: pl.BlockSpec((h, tm), lambda i: (0, i))
    full = lambda a: pl.BlockSpec(a.shape, lambda i: (0,) * a.ndim)
    tile3 = lambda a, b: pl.BlockSpec((1, a, b), lambda i: (i, 0, 0))
    out_specs = [row(SSM_WIDTH), row(CONV_DIM), col(ATT_WIDTH), row(X_WIDTH), row(X_WIDTH),
                 tile3(tm, ATT_WIDTH), tile3(1, ATT_WIDTH), tile3(ATT_WIDTH, tm),
                 col(ATT_WIDTH), col(ATT_WIDTH), tile3(ATT_WIDTH, tm), col(SSM_HEADS)]
    sds = jax.ShapeDtypeStruct
    out_shape = [sds((m, SSM_WIDTH), F32), sds((m, CONV_DIM), F32), sds((ATT_WIDTH, m), F32),
                 sds((m, X_WIDTH), BF16), sds((m, X_WIDTH), F32),
                 sds((nb, tm, ATT_WIDTH), BF16), sds((nb, 1, ATT_WIDTH), F32), sds((nb, ATT_WIDTH, tm), F32),
                 sds((ATT_WIDTH, m), F32), sds((ATT_WIDTH, m), F32), sds((nb, ATT_WIDTH, tm), BF16),
                 sds((SSM_HEADS, m), F32)]
    return pl.pallas_call(
        _inproj_prompt_kernel,
        grid=(nb,),
        in_specs=[row(d), full(nw), full(wn), full(wt)],
        out_specs=out_specs,
        out_shape=out_shape,
        compiler_params=_cparams(("parallel",)),
        name="in_proj_prompt",
    )(x, nw, wn, wt)


_SSD_CHUNKS = 2
def _ssd_kernel(xbc_ref, dtT_ref, z_ref, convw_ref, convb_ref, dtb_ref, alog_ref, dskip_ref, nw_ref,
                e_ref, y_ref, hT_ref, ext_sc, h_sc):
    c = pl.program_id(0)
    L = SSM_CHUNK

    @pl.when(c == 0)
    def _():
        ext_sc[0:SUBLANES, :] = jnp.zeros((SUBLANES, CONV_DIM), F32)
        h_sc[...] = jnp.zeros_like(h_sc)

    for sub in range(_SSD_CHUNKS):
        rows = slice(sub * L, (sub + 1) * L)
        x = xbc_ref[rows, :]
        ext_sc[SUBLANES:SUBLANES + L, :] = x
        w = convw_ref[...]
        acc = x * w[CONV_W - 1:CONV_W, :] + convb_ref[...]
        for k in range(1, CONV_W):
            acc = acc + ext_sc[SUBLANES - k:SUBLANES - k + L, :] * w[CONV_W - 1 - k:CONV_W - k, :]
        ext_sc[0:SUBLANES, :] = x[L - SUBLANES:L]
        u = _silu(acc)
        xs = u[:, :SSM_WIDTH]
        bm = u[:, SSM_WIDTH:SSM_WIDTH + SSM_GROUPS * SSM_STATE]
        cm = u[:, SSM_WIDTH + SSM_GROUPS * SSM_STATE:]

        dtT = _softplus(dtT_ref[:, rows] + dtb_ref[...])
        aT = dtT * (-jnp.exp(alog_ref[...]))
        r_i = lax.broadcasted_iota(jnp.int32, (L, L), 0)
        c_i = lax.broadcasted_iota(jnp.int32, (L, L), 1)
        upper = (r_i <= c_i).astype(BF16)
        acumT = _dot_f32_exactrhs(aT, upper)
        a_last = acumT[:, L - 1:L]
        to_endT = jnp.exp(a_last - acumT) * dtT
        eacT = jnp.exp(acumT)
        ac2T = acumT * LOG2E
        stack = jnp.concatenate(
            [ac2T, dtT, to_endT, eacT, jnp.zeros((L - 4 * SSM_HEADS, L), F32)], axis=0)
        st = stack.T
        e = e_ref[...]
        dt_e = _dot_f32_exactrhs(st[:, SSM_HEADS:2 * SSM_HEADS], e, 2)
        toend_e = _dot_f32_exactrhs(st[:, 2 * SSM_HEADS:3 * SSM_HEADS], e, 2)
        eac_e = _dot_f32_exactrhs(st[:, 3 * SSM_HEADS:4 * SSM_HEADS], e, 2)
        cd_e = eac_e[L - 1:L, :]

        xd = (xs * dt_e).astype(BF16)
        xw = (xs * toend_e).astype(BF16)
        causal = r_i >= c_i
        ys = []
        yoffs = []
        for g in range(SSM_GROUPS):
            bg = bm[:, g * SSM_STATE:(g + 1) * SSM_STATE].astype(BF16)
            cg = cm[:, g * SSM_STATE:(g + 1) * SSM_STATE].astype(BF16)
            cb = jnp.where(causal, _dot_nt(cg, bg), 0.0)
            for hh in range(SSM_HG):
                h = g * SSM_HG + hh
                seg = st[:, h:h + 1] - ac2T[h:h + 1, :]
                mix = (cb * jnp.exp2(jnp.minimum(seg, 0.0))).astype(BF16)
                ys.append(_dot(mix, xd[:, h * SSM_HEAD_DIM:(h + 1) * SSM_HEAD_DIM]))
            hprev = h_sc[g]
            yoffs.append(_dot(cg, hprev.astype(BF16)))
            s_new = _dot_tn(bg, xw[:, g * SSM_GW:(g + 1) * SSM_GW])
            h_sc[g] = hprev * cd_e[:, g * SSM_GW:(g + 1) * SSM_GW] + s_new

        y = jnp.concatenate(ys, axis=1) + jnp.concatenate(yoffs, axis=1) * eac_e + dskip_ref[...] * xs
        gt = y * _silu(z_ref[rows, :])
        outs = []
        for g in range(SSM_GROUPS):
            gg = gt[:, g * SSM_GW:(g + 1) * SSM_GW]
            outs.append(gg * lax.rsqrt(jnp.mean(gg * gg, axis=-1, keepdims=True) + EPS))
        y_ref[rows, :] = (jnp.concatenate(outs, axis=1) * nw_ref[...]).astype(y_ref.dtype)

    @pl.when(c == pl.num_programs(0) - 1)
    def _():
        hT_ref[...] = h_sc[...]


def _head_expand():
    return jnp.asarray(np.repeat(np.eye(SSM_HEADS, dtype=np.float32), SSM_HEAD_DIM, axis=1), BF16)


def _ssd_prompt(xbc, dtT, z, conv_w, conv_b, dt_bias, a_log, d_skip, ssm_norm_w):
    m = xbc.shape[0]
    L = SSM_CHUNK
    step = L * _SSD_CHUNKS
    row = lambda w: pl.BlockSpec((step, w), lambda i: (i, 0))
    full = lambda a: pl.BlockSpec(a.shape, lambda i: (0,) * a.ndim)
    convb = conv_b.reshape(1, CONV_DIM)
    dtb = dt_bias.reshape(SSM_HEADS, 1)
    alog = a_log.reshape(SSM_HEADS, 1)
    dskip = jnp.repeat(d_skip, SSM_HEAD_DIM).reshape(1, SSM_WIDTH)
    nw = ssm_norm_w.reshape(1, SSM_WIDTH)
    e = _head_expand()
    y, hT = pl.pallas_call(
        _ssd_kernel,
        grid=(m // step,),
        in_specs=[row(CONV_DIM), pl.BlockSpec((SSM_HEADS, step), lambda i: (0, i)), row(SSM_WIDTH),
                  full(conv_w), full(convb), full(dtb), full(alog), full(dskip), full(nw), full(e)],
        out_specs=[row(SSM_WIDTH), pl.BlockSpec((SSM_GROUPS, SSM_STATE, SSM_GW), lambda i: (0, 0, 0))],
        out_shape=[jax.ShapeDtypeStruct((m, SSM_WIDTH), BF16),
                   jax.ShapeDtypeStruct((SSM_GROUPS, SSM_STATE, SSM_GW), F32)],
        scratch_shapes=[pltpu.VMEM((SUBLANES + L, CONV_DIM), F32),
                        pltpu.VMEM((SSM_GROUPS, SSM_STATE, SSM_GW), F32)],
        compiler_params=_cparams(("arbitrary",)),
        name="ssd_prompt",
    )(xbc, dtT, z, conv_w, convb, dtb, alog, dskip, nw, e)
    state = hT.reshape(SSM_GROUPS, SSM_STATE, SSM_HG, SSM_HEAD_DIM).transpose(0, 2, 3, 1)
    return y, state.reshape(SSM_HEADS, SSM_HEAD_DIM, SSM_STATE)


def _bucket_np(dist):
    n = np.maximum(dist, 0)
    exact = RPE_BUCKETS // 2
    nf = np.maximum(n, 1).astype(np.float32)
    scaled = (np.log(nf / np.float32(exact)) / np.float32(math.log(RPE_MAX_DIST / exact))
              * np.float32(RPE_BUCKETS - exact))
    large = exact + scaled.astype(np.int32)
    return np.where(n < exact, n, np.minimum(large, RPE_BUCKETS - 1)).astype(np.int32)


def _bias_kernel(rb_ref, map_ref, out_ref, *, scale):
    h = pl.program_id(0)
    bmap = map_ref[...]
    acc = jnp.full(bmap.shape, -2.0 * BIG, F32)
    for b in range(RPE_BUCKETS):
        acc = jnp.where(bmap == b, rb_ref[b * ATT_HEADS + h] * scale, acc)
    out_ref[0] = acc


def _bias_table(rel_bias, bucket_map, scale=1.0):
    bmap = jnp.asarray(bucket_map, jnp.int32)
    nd = bmap.ndim
    return pl.pallas_call(
        functools.partial(_bias_kernel, scale=scale),
        grid=(ATT_HEADS,),
        in_specs=[pl.BlockSpec(memory_space=pltpu.SMEM),
                  pl.BlockSpec(bmap.shape, lambda h: (0,) * nd)],
        out_specs=pl.BlockSpec((1,) + bmap.shape, lambda h: (h,) + (0,) * nd),
        out_shape=jax.ShapeDtypeStruct((ATT_HEADS,) + bmap.shape, F32),
        compiler_params=_cparams(("parallel",)),
        name="rel_bias_table",
    )(rel_bias.reshape(-1), bmap)


def _topk_hits(s, blk, axis):
    picks = []
    for _ in range(MOBA_TOPK):
        mx = jnp.max(s, axis=axis, keepdims=True)
        idx = jnp.min(jnp.where(s == mx, blk, float(2 * LANES)), axis=axis, keepdims=True)
        picks.append((idx, jnp.where(mx > -jnp.inf, 1.0, 0.0)))
        s = jnp.where(blk == idx, -jnp.inf, s)
    return picks


_HPS = 8
_K_HEADS = 4
_SPLIT = 4
_ONES_ROWS = 16


def _moba_kernel(cfar_ref, qT_ref, kb_ref, vT_ref, km_ref, d0_ref, d1_ref, gaT_ref, o_ref,
                 pen_sc, acc_sc, m_sc):
    hp = pl.program_id(0)
    i = pl.program_id(1)
    W = MOBA_BLOCK
    D = ATT_HEAD_DIM
    nb = km_ref.shape[0]
    blk = lax.broadcasted_iota(jnp.int32, (nb, W), 0).astype(F32)
    i_f = i.astype(F32)
    near = jnp.maximum(i - 1, 0)
    q_augs, cfars = [], []
    acc_sc[...] = jnp.zeros_like(acc_sc)
    for hh in range(_HPS):
        rows = slice(hh * D, (hh + 1) * D)
        cfar2 = cfar_ref[hp * _HPS + hh] * LOG2E
        qT = qT_ref[0, rows, :]
        sT = _dot_hi(km_ref[:, rows], qT)
        sT = jnp.where(blk < i_f, sT, -jnp.inf)
        hit = jnp.zeros((nb, W), F32)
        for idx, valid in _topk_hits(sT, blk, 0):
            hit = jnp.maximum(hit, jnp.where(blk == idx, valid, 0.0))
        pen_sc[hh] = jnp.where(hit > 0.0, cfar2, -2.0 * BIG)

        qs = (qT * (D ** -0.5 * LOG2E)).astype(BF16)
        parts = [jnp.zeros((D, W), BF16)] * _K_HEADS
        parts[hh % _K_HEADS] = qs
        q_augs.append(jnp.concatenate(parts, axis=0))
        cfars.append(cfar2)

    m_sc[...] = jnp.full(m_sc.shape, -BIG, F32)
    ones = jnp.ones((_ONES_ROWS, W), BF16)

    def scores(hh, j, bias_ref):
        g = hh // _K_HEADS
        lanes = slice(g * _K_HEADS * D, (g + 1) * _K_HEADS * D)
        s = _dot(kb_ref[j, :, lanes], q_augs[hh])
        return s if bias_ref is None else s + bias_ref[hh]

    def absorb(chains, ss):
        ps = []
        for (hh, a, j, _, pen), s in zip(chains, ss):
            m = m_sc[hh, a, 0:1, :]
            m_new = jnp.maximum(m, jnp.max(s, axis=0, keepdims=True) + pen)
            m_sc[hh, a, 0:1, :] = m_new
            ps.append((jnp.exp2(m - m_new), jnp.exp2(s - (m_new - pen)).astype(BF16)))
        for (hh, a, j, _, _), (alpha, p) in zip(chains, ps):
            v1 = jnp.concatenate([vT_ref[j, hh * D:(hh + 1) * D, :], ones], axis=0)
            acc_sc[hh, a] = alpha * acc_sc[hh, a] + _dot(v1, p)

    def group(chains):
        absorb(chains, [scores(hh, j, bias_ref) for hh, _, j, bias_ref, _ in chains])

    zero_row = jnp.zeros((1, W), F32)
    group([(hh, 0, i, d0_ref, zero_row) for hh in range(_HPS)]
          + [(hh, 1, near, d1_ref, pen_sc[hh, pl.ds(near, 1), :] - cfars[hh]) for hh in range(_HPS)])

    def far_chains(first, count):
        return [(hh, a, first + a, None, pen_sc[hh, pl.ds(first + a, 1), :])
                for a in range(count) for hh in range(_HPS)]

    def far_group(first, count):
        group(far_chains(first, count))

    def far_body(trip, carry):
        far_group(_SPLIT * trip, _SPLIT)
        return carry

    lax.fori_loop(0, near // _SPLIT, far_body, 0)
    done = (near // _SPLIT) * _SPLIT
    count = _SPLIT // 2
    while count >= 1:
        take = (near & count) != 0

        @pl.when(take)
        def _(done=done, count=count):
            far_group(done, count)

        done = done + jnp.where(take, count, 0)
        count //= 2

    outs = []
    for hh in range(_HPS):
        m = m_sc[hh, 0, 0:1, :]
        for a in range(1, _SPLIT):
            m = jnp.maximum(m, m_sc[hh, a, 0:1, :])
        acc = jnp.zeros((D + _ONES_ROWS, W), F32)
        for a in range(_SPLIT):
            acc = acc + jnp.exp2(m_sc[hh, a, 0:1, :] - m) * acc_sc[hh, a]
        outs.append(acc[:D] / acc[D:D + 1])
    o_ref[...] = (jnp.concatenate(outs, axis=0) * _silu(gaT_ref[...])).astype(o_ref.dtype)


def _moba_prompt(qT3, kb3, vT3, kmean, d0T, d1T, gaT, rel_bias):
    nb = qT3.shape[0]
    W = MOBA_BLOCK
    HW = _HPS * ATT_HEAD_DIM
    assert nb - 1 >= MOBA_TOPK and nb <= LANES and ATT_HEADS % _HPS == 0 and HW % LANES == 0
    cfar = rel_bias[RPE_BUCKETS - 1]
    return pl.pallas_call(
        _moba_kernel,
        grid=(ATT_HEADS // _HPS, nb),
        in_specs=[pl.BlockSpec(memory_space=pltpu.SMEM),
                  pl.BlockSpec((1, HW, W), lambda hp, i: (i, hp, 0)),
                  pl.BlockSpec((nb, W, HW), lambda hp, i: (0, 0, hp), pipeline_mode=pl.Buffered(1)),
                  pl.BlockSpec((nb, HW, W), lambda hp, i: (0, hp, 0), pipeline_mode=pl.Buffered(1)),
                  pl.BlockSpec((nb, HW), lambda hp, i: (0, hp)),
                  pl.BlockSpec((_HPS, W, W), lambda hp, i: (hp, 0, 0)),
                  pl.BlockSpec((_HPS, W, W), lambda hp, i: (hp, 0, 0)),
                  pl.BlockSpec((HW, W), lambda hp, i: (hp, i))],
        out_specs=pl.BlockSpec((HW, W), lambda hp, i: (hp, i)),
        out_shape=jax.ShapeDtypeStruct((ATT_WIDTH, nb * W), BF16),
        scratch_shapes=[pltpu.VMEM((_HPS, nb, W), F32),
                        pltpu.VMEM((_HPS, _SPLIT, ATT_HEAD_DIM + _ONES_ROWS, W), F32),
                        pltpu.VMEM((_HPS, _SPLIT, SUBLANES, W), F32)],
        compiler_params=_cparams(("parallel", "arbitrary")),
        name="moba_prompt",
    )(cfar, qT3, kb3, vT3, kmean, d0T, d1T, gaT)


def _memkv_kernel(mem_ref, g_ref, w_ref, k_ref, v_ref):
    xn = _rms(mem_ref[...], g_ref[...]).astype(BF16)
    kv = _dot(xn, w_ref[...])
    k_ref[...] = kv[:, :X_WIDTH]
    v_ref[...] = kv[:, X_WIDTH:]


def _memory_kv(mem, g, w):
    nm, d = mem.shape
    full = lambda a: pl.BlockSpec(a.shape, lambda: (0,) * a.ndim)
    g2 = g.reshape(1, d)
    wb = w.astype(BF16)
    return pl.pallas_call(
        _memkv_kernel,
        in_specs=[full(mem), full(g2), full(wb)],
        out_specs=[pl.BlockSpec((nm, X_WIDTH), lambda: (0, 0))] * 2,
        out_shape=[jax.ShapeDtypeStruct((nm, X_WIDTH), F32)] * 2,
        compiler_params=pltpu.CompilerParams(vmem_limit_bytes=VMEM_LIMIT),
        name="memory_kv",
    )(mem, g2, wb)


def _xattn_kernel(q_ref, gx_ref, mk_ref, mv_ref, o_ref):
    q = q_ref[...]
    mk = mk_ref[...].astype(BF16)
    mv = mv_ref[...].astype(BF16)
    outs = []
    for h in range(X_HEADS):
        sl = slice(h * X_HEAD_DIM, (h + 1) * X_HEAD_DIM)
        s = _dot_nt(q[:, sl], mk[:, sl])
        m = jnp.max(s, axis=1, keepdims=True)
        p = jnp.exp(s - m)
        l = jnp.sum(p, axis=1, keepdims=True)
        outs.append(_dot(p.astype(BF16), mv[:, sl]) / l)
    o_ref[...] = (jnp.concatenate(outs, axis=1) * _silu(gx_ref[...])).astype(o_ref.dtype)


def _xattn_prompt(qx, gx, mk, mv, *, tq):
    m = qx.shape[0]
    row = pl.BlockSpec((tq, X_WIDTH), lambda i: (i, 0))
    full = pl.BlockSpec(mk.shape, lambda i: (0, 0))
    return pl.pallas_call(
        _xattn_kernel,
        grid=(m // tq,),
        in_specs=[row, row, full, full],
        out_specs=row,
        out_shape=jax.ShapeDtypeStruct((m, X_WIDTH), BF16),
        compiler_params=_cparams(("parallel",)),
        name="xattn_prompt",
    )(qx, gx, mk, mv)


def _xattn_sample_kernel(q_ref, gx_ref, mk_ref, mv_ref, o_ref):
    q = q_ref[0]
    outs = []
    for h in range(X_HEADS):
        sl = slice(h * X_HEAD_DIM, (h + 1) * X_HEAD_DIM)
        s = _dot_nt((q[:, sl] * (X_HEAD_DIM ** -0.5)).astype(BF16), mk_ref[0, :, h, :].astype(BF16))
        m = jnp.max(s, axis=1, keepdims=True)
        p = jnp.exp(s - m)
        l = jnp.sum(p, axis=1, keepdims=True)
        outs.append(_dot(p.astype(BF16), mv_ref[0, :, h, :].astype(BF16)) / l)
    o_ref[0] = jnp.concatenate(outs, axis=1) * _silu(gx_ref[0])


def _xattn_sample(qx, gx, mk, mv):
    b, t8, _ = qx.shape
    nm = mk.shape[1]
    qs = pl.BlockSpec((1, t8, X_WIDTH), lambda i: (i, 0, 0))
    ms = pl.BlockSpec((1, nm, X_HEADS, X_HEAD_DIM), lambda i: (i, 0, 0, 0))
    return pl.pallas_call(
        _xattn_sample_kernel,
        grid=(b,),
        in_specs=[qs, qs, ms, ms],
        out_specs=qs,
        out_shape=jax.ShapeDtypeStruct((b, t8, X_WIDTH), F32),
        compiler_params=_cparams(("parallel",)),
        name="xattn_sample",
    )(qx, gx, mk, mv)


def _outproj_tail(x_ref, ys_ref, att_proj, yx_ref, w_ref, fw_ref, o_ref):
    h = x_ref[...] + _dot(ys_ref[...].astype(BF16), w_ref[0:SSM_WIDTH, :])
    h = h + att_proj
    h = h + _dot(yx_ref[...].astype(BF16), w_ref[SSM_WIDTH + ATT_WIDTH:, :])
    o_ref[...] = _rms(h, fw_ref[...])


def _outproj_kernel(x_ref, ys_ref, ya_ref, ga_ref, yx_ref, w_ref, fw_ref, o_ref):
    att = (ya_ref[...] * _silu(ga_ref[...])).astype(BF16)
    att_proj = _dot(att, w_ref[SSM_WIDTH:SSM_WIDTH + ATT_WIDTH, :])
    _outproj_tail(x_ref, ys_ref, att_proj, yx_ref, w_ref, fw_ref, o_ref)


def _outproj_t_kernel(x_ref, ys_ref, yaT_ref, yx_ref, w_ref, fw_ref, o_ref):
    att_proj = _dot_tn(yaT_ref[...].astype(BF16), w_ref[SSM_WIDTH:SSM_WIDTH + ATT_WIDTH, :])
    _outproj_tail(x_ref, ys_ref, att_proj, yx_ref, w_ref, fw_ref, o_ref)


def _out_proj(x, y_ssd, y_att, ga, y_x, w_out, final_norm_w, *, tm):
    m, d = x.shape
    row = lambda w: pl.BlockSpec((tm, w), lambda i: (i, 0))
    full = lambda a: pl.BlockSpec(a.shape, lambda i: (0,) * a.ndim)
    wb = w_out.astype(BF16)
    fw = final_norm_w.reshape(1, d)
    if ga is None:
        body, att_args = _outproj_t_kernel, (y_att,)
        att_specs = [pl.BlockSpec((ATT_WIDTH, tm), lambda i: (0, i))]
    else:
        body, att_args = _outproj_kernel, (y_att, ga)
        att_specs = [row(ATT_WIDTH), row(ATT_WIDTH)]
    return pl.pallas_call(
        body,
        grid=(m // tm,),
        in_specs=[row(d), row(SSM_WIDTH)] + att_specs + [row(X_WIDTH), full(wb), full(fw)],
        out_specs=row(d),
        out_shape=jax.ShapeDtypeStruct((m, d), F32),
        compiler_params=_cparams(("parallel",)),
        name="out_proj",
    )(x, y_ssd, *att_args, y_x, wb, fw)


def _ssd_sample_kernel(x_ref, c_ref, dt_ref, z_ref, h0_ref, w_ref, b_ref, dtb_ref, alog_ref, dsk_ref,
                       nw_ref, sel_ref, y_ref, h_ref, y_sc, *, T):
    P = SSM_HEAD_DIM
    N = SSM_STATE
    cr = c_ref[0]
    xr = x_ref[0]
    ext = lambda i: cr[i:i + 1, :] if i < CONV_W - 1 else xr[i - (CONV_W - 1):i - (CONV_W - 2), :]
    w = w_ref[...]
    rows = []
    for t in range(T):
        u = b_ref[...]
        for j in range(CONV_W):
            u = u + w[j:j + 1, :] * ext(t + j)
        rows.append(u)
    u = _silu(jnp.concatenate(rows + [jnp.zeros((SUBLANES - T, CONV_DIM), F32)], axis=0))
    xs = u[:, :SSM_WIDTH]
    dt = _softplus(dt_ref[0] + dtb_ref[...])
    dec = jnp.exp(dt * (-jnp.exp(alog_ref[...])))
    xdt = xs[0:T] * dt
    sel = sel_ref[...]
    bc = [_dot_tn_exactrhs(u[:, SSM_WIDTH + k * N:SSM_WIDTH + (k + 1) * N], sel)
          for k in range(2 * SSM_GROUPS)]
    for pair in range(SSM_HEADS // 2):
        g = (2 * pair) // SSM_HG
        lanes = slice(pair * 2 * P, (pair + 1) * 2 * P)
        hT = jnp.concatenate([h0_ref[0, 2 * pair], h0_ref[0, 2 * pair + 1]], axis=0).T
        for t in range(T):
            tl = slice(t * N, (t + 1) * N)
            hT = hT * dec[t:t + 1, lanes] + bc[g][:, tl] * xdt[t:t + 1, lanes]
            y_sc[t:t + 1, lanes] = jnp.sum(hT * bc[SSM_GROUPS + g][:, tl], axis=0, keepdims=True)
        h2 = hT.T
        h_ref[0, 2 * pair] = h2[:P]
        h_ref[0, 2 * pair + 1] = h2[P:]
    y = y_sc[0:T, :] + dsk_ref[...] * xs[0:T]
    gt = y * _silu(z_ref[0])
    outs = []
    for g in range(SSM_GROUPS):
        gg = gt[:, g * SSM_GW:(g + 1) * SSM_GW]
        outs.append(gg * lax.rsqrt(jnp.mean(gg * gg, axis=-1, keepdims=True) + EPS))
    y_ref[0] = jnp.concatenate(outs, axis=1) * nw_ref[...]


def _dot_tn_exactrhs(a, b_bf16):
    out = None
    for part in _split3(a):
        term = _dot_tn(part, b_bf16)
        out = term if out is None else out + term
    return out


def _ssd_sample(xbc, state_conv, dt_raw, z, state_ssm, conv_w, conv_b, dt_bias, a_log, d_skip, ssm_norm_w):
    b, T, _ = xbc.shape
    per = lambda a: pl.BlockSpec((1,) + a.shape[1:], lambda i: (i,) + (0,) * (a.ndim - 1))
    full = lambda a: pl.BlockSpec(a.shape, lambda i: (0,) * a.ndim)
    per_head = lambda v: jnp.repeat(v, SSM_HEAD_DIM, axis=-1)
    dt_e = per_head(dt_raw)
    br = conv_b.reshape(1, CONV_DIM)
    dtb = per_head(dt_bias).reshape(1, SSM_WIDTH)
    alog = per_head(a_log).reshape(1, SSM_WIDTH)
    dsk = per_head(d_skip).reshape(1, SSM_WIDTH)
    nw = ssm_norm_w.reshape(1, SSM_WIDTH)
    sel = np.zeros((SUBLANES, T * SSM_STATE), np.float32)
    for t in range(T):
        sel[t, t * SSM_STATE:(t + 1) * SSM_STATE] = 1.0
    sel = jnp.asarray(sel, BF16)
    args = (xbc, state_conv, dt_e, z, state_ssm, conv_w, br, dtb, alog, dsk, nw, sel)
    in_specs = [per(a) for a in args[:5]] + [full(a) for a in args[5:]]
    return pl.pallas_call(
        functools.partial(_ssd_sample_kernel, T=T),
        grid=(b,),
        in_specs=in_specs,
        out_specs=[pl.BlockSpec((1, T, SSM_WIDTH), lambda i: (i, 0, 0)), per(state_ssm)],
        out_shape=[jax.ShapeDtypeStruct((b, T, SSM_WIDTH), F32),
                   jax.ShapeDtypeStruct(state_ssm.shape, F32)],
        scratch_shapes=[pltpu.VMEM((SUBLANES, SSM_WIDTH), F32)],
        compiler_params=_cparams(("parallel",)),
        name="ssd_sample",
    )(*args)


_KM_PAGES = 64


def _select_pages_kernel(pt_ref, q_ref, *refs):
    page_refs, o_ref, km_sc = refs[:-2], refs[-2], refs[-1]
    si = pl.program_id(1)
    nb = km_sc.shape[1]

    @pl.when(si == 0)
    def _():
        km_sc[...] = jnp.zeros_like(km_sc)

    lane = lax.broadcasted_iota(jnp.int32, (ATT_WIDTH, nb), 1)
    acc = km_sc[...]
    for b in range(_KM_PAGES // PAGES_PER_BLOCK):
        tok = page_refs[b * PAGES_PER_BLOCK][0].reshape(ATT_WIDTH, PAGE_SIZE)
        for pg in range(1, PAGES_PER_BLOCK):
            tok = tok + page_refs[b * PAGES_PER_BLOCK + pg][0].reshape(ATT_WIDTH, PAGE_SIZE)
        mean = jnp.sum(tok, axis=1, keepdims=True) * (1.0 / MOBA_BLOCK)
        acc = jnp.where(lane == si * (_KM_PAGES // PAGES_PER_BLOCK) + b, mean, acc)
    km_sc[...] = acc

    @pl.when(si == pl.num_programs(1) - 1)
    def _():
        q = q_ref[0]
        t8 = q.shape[0]
        blk = lax.broadcasted_iota(jnp.int32, (t8, nb), 1).astype(F32)
        pick_lane = lax.broadcasted_iota(jnp.int32, (t8, LANES), 1)
        for h in range(ATT_HEADS):
            rows = slice(h * ATT_HEAD_DIM, (h + 1) * ATT_HEAD_DIM)
            s = _dot_hi(q[:, rows], km_sc[rows, :])
            tile = jnp.zeros((t8, LANES), F32)
            for r, (idx, _) in enumerate(_topk_hits(s, blk, 1)):
                tile = jnp.where(pick_lane == r, jnp.minimum(idx, nb - 1.0), tile)
            o_ref[0, h] = tile.astype(jnp.int32)


def _select_sample(q8, pool_kt, page_table):
    b, n_pages = page_table.shape
    t8 = q8.shape[1]
    nb = n_pages // PAGES_PER_BLOCK
    assert nb >= MOBA_TOPK
    steps = n_pages // _KM_PAGES
    page_spec = lambda r: pl.BlockSpec(
        (1, ATT_HEADS, ATT_HEAD_DIM, PAGE_SIZE),
        lambda bi, si, pt: (pt[bi * n_pages + si * _KM_PAGES + r], 0, 0, 0))
    grid_spec = pltpu.PrefetchScalarGridSpec(
        num_scalar_prefetch=1,
        grid=(b, steps),
        in_specs=[pl.BlockSpec((1, t8, ATT_WIDTH), lambda bi, si, pt: (bi, 0, 0))]
        + [page_spec(r) for r in range(_KM_PAGES)],
        out_specs=pl.BlockSpec((1, ATT_HEADS, t8, LANES), lambda bi, si, pt: (bi, 0, 0, 0)),
        scratch_shapes=[pltpu.VMEM((ATT_WIDTH, nb), F32)],
    )
    return pl.pallas_call(
        _select_pages_kernel,
        grid_spec=grid_spec,
        out_shape=jax.ShapeDtypeStruct((b, ATT_HEADS, t8, LANES), jnp.int32),
        compiler_params=_cparams(("parallel", "arbitrary")),
        name="moba_select_pages",
    )(page_table.reshape(-1), q8, *([pool_kt] * _KM_PAGES))


_N_SLAB = MOBA_TOPK * PAGES_PER_BLOCK


def _attend_sample_kernel(sel_ref, pt_ref, rb_ref, q_ref, kn_ref, vn_ref, near_ref, pk_ref, pv_ref,
                          o_ref, kbuf, vbuf, sem, *, T, n_pages):
    bi = pl.program_id(0)
    t = pl.program_id(1)
    step = bi * T + t
    nsteps = pl.num_programs(0) * T
    nb = n_pages // PAGES_PER_BLOCK
    D = ATT_HEAD_DIM

    def copies(s, slot):
        b_s = s // T
        out = []
        for h in range(ATT_HEADS):
            for r in range(MOBA_TOPK):
                blk = sel_ref[(s * ATT_HEADS + h) * MOBA_TOPK + r]
                for pg in range(PAGES_PER_BLOCK):
                    phys = pt_ref[b_s * n_pages + blk * PAGES_PER_BLOCK + pg]
                    i = r * PAGES_PER_BLOCK + pg
                    out.append(pltpu.make_async_copy(pk_ref.at[phys, h], kbuf.at[slot, h, i], sem.at[0, slot]))
                    out.append(pltpu.make_async_copy(pv_ref.at[phys, h], vbuf.at[slot, h, i], sem.at[1, slot]))
        return out

    @pl.when(step == 0)
    def _():
        for cp in copies(step, 0):
            cp.start()

    @pl.when(step + 1 < nsteps)
    def _():
        for cp in copies(step + 1, (step + 1) % 2):
            cp.start()

    slot = step % 2
    pltpu.make_async_copy(kbuf.at[slot], kbuf.at[slot], sem.at[0, slot]).wait()
    pltpu.make_async_copy(vbuf.at[slot], vbuf.at[slot], sem.at[1, slot]).wait()

    first_row = jnp.where(lax.broadcasted_iota(jnp.int32, (SUBLANES, PAGE_SIZE), 0) == 0, 1.0, 0.0).astype(BF16)
    q8 = jnp.concatenate([q_ref[0], jnp.zeros((SUBLANES - 1, ATT_WIDTH), F32)], axis=0)
    qb = _dot_tn_exactrhs(q8, first_row)
    lane_t = lax.broadcasted_iota(jnp.int32, (1, T), 1)
    neg_tail = jnp.full((1, PAGE_SIZE - T), NEG, F32)
    neg_row = jnp.full((1, PAGE_SIZE), NEG, F32)

    tiles = []
    for h in range(ATT_HEADS):
        rows = slice(h * D, (h + 1) * D)
        cfar = rb_ref[(RPE_BUCKETS - 1) * ATT_HEADS + h]
        s_rows = []
        for r in range(MOBA_TOPK):
            blk = sel_ref[(step * ATT_HEADS + h) * MOBA_TOPK + r]
            for pg in range(PAGES_PER_BLOCK):
                s = jnp.sum(kbuf[slot, h, r * PAGES_PER_BLOCK + pg] * qb[rows], axis=0, keepdims=True)
                s_rows.append(s + jnp.where(blk == nb - 1, near_ref[h, 0, pg:pg + 1, :], cfar))
        s_own = jnp.sum(kn_ref[0, rows, :] * qb[rows, 0:T], axis=0, keepdims=True)
        b_own = jnp.zeros((1, T), F32)
        for tp in range(T):
            b_own = jnp.where(lane_t == tp, rb_ref[jnp.maximum(t - tp, 0) * ATT_HEADS + h], b_own)
        s_own = jnp.where(lane_t <= t, s_own + b_own, NEG)
        tiles.append(jnp.concatenate(s_rows + [jnp.concatenate([s_own, neg_tail], axis=1), neg_row], axis=0))

    ones8 = jnp.ones((SUBLANES, PAGE_SIZE), BF16)
    zero_tail = jnp.zeros((D, PAGE_SIZE - T), F32)
    outs = []
    for h, s in enumerate(tiles):
        rows = slice(h * D, (h + 1) * D)
        m = jnp.max(jnp.max(s, axis=1, keepdims=True), axis=0, keepdims=True)
        p = jnp.exp(s - m)
        l = jnp.sum(jnp.sum(p, axis=1, keepdims=True), axis=0, keepdims=True)
        acc = jnp.concatenate([vn_ref[0, rows, :] * p[_N_SLAB:_N_SLAB + 1, 0:T], zero_tail], axis=1)
        for n in range(_N_SLAB):
            acc = acc + vbuf[slot, h, n] * p[n:n + 1, :]
        o = None
        for part in _split3(acc):
            term = _dot_nt(ones8, part)
            o = term if o is None else o + term
        outs.append(o[0:1, :] / l)
    o_ref[0] = jnp.concatenate(outs, axis=1)


def _attend_sample(sel, page_table, rel_bias, q_col, k_newT, v_newT, near, pool_kt, pool_vt):
    b, _, T = k_newT.shape
    n_pages = page_table.shape[1]
    assert T <= RPE_BUCKETS // 2
    slab = (2, ATT_HEADS, _N_SLAB, ATT_HEAD_DIM, PAGE_SIZE)
    grid_spec = pltpu.PrefetchScalarGridSpec(
        num_scalar_prefetch=3,
        grid=(b, T),
        in_specs=[pl.BlockSpec((1, 1, ATT_WIDTH), lambda bi, t, *_: (bi * T + t, 0, 0)),
                  pl.BlockSpec((1, ATT_WIDTH, T), lambda bi, t, *_: (bi, 0, 0)),
                  pl.BlockSpec((1, ATT_WIDTH, T), lambda bi, t, *_: (bi, 0, 0)),
                  pl.BlockSpec((ATT_HEADS, 1, PAGES_PER_BLOCK, PAGE_SIZE), lambda bi, t, *_: (0, t, 0, 0)),
                  pl.BlockSpec(memory_space=pl.ANY),
                  pl.BlockSpec(memory_space=pl.ANY)],
        out_specs=pl.BlockSpec((1, 1, ATT_WIDTH), lambda bi, t, *_: (bi * T + t, 0, 0)),
        scratch_shapes=[pltpu.VMEM(slab, F32), pltpu.VMEM(slab, F32), pltpu.SemaphoreType.DMA((2, 2))],
    )
    return pl.pallas_call(
        functools.partial(_attend_sample_kernel, T=T, n_pages=n_pages),
        grid_spec=grid_spec,
        out_shape=jax.ShapeDtypeStruct((b * T, 1, ATT_WIDTH), F32),
        compiler_params=_cparams(("arbitrary", "arbitrary")),
        name="moba_attend_sample",
    )(sel, page_table.reshape(-1), rel_bias.reshape(-1), q_col, k_newT, v_newT, near, pool_kt, pool_vt)


_PROMPT_ROW_TILE = 1024


def _prompt_layer(x, mem, norm_w, w_in, conv_w, conv_b, dt_bias, a_log, d_skip, ssm_norm_w,
                  mem_norm_w, w_mem_kv, w_out, rel_bias, final_norm_w):
    s = x.shape[0]
    W = MOBA_BLOCK
    z, xbc, gaT, qx, gx, kb3, km3, qT3, kT, vT, vT3, dtT = _in_proj_prompt(x, norm_w, w_in)
    y_ssd, state = _ssd_prompt(xbc, dtT, z, conv_w, conv_b, dt_bias, a_log, d_skip, ssm_norm_w)
    ki = np.arange(W)[:, None]
    qi = np.arange(W)[None, :]
    own_map = np.where(ki <= qi, _bucket_np(qi - ki), -1)
    d01 = _bias_table(rel_bias, np.stack([own_map, _bucket_np(W + qi - ki)]), LOG2E)
    y_attT = _moba_prompt(qT3, kb3, vT3, km3.reshape(-1, ATT_WIDTH), d01[:, 0], d01[:, 1], gaT, rel_bias)
    mk, mv = _memory_kv(mem, mem_norm_w, w_mem_kv)
    y_x = _xattn_prompt(qx, gx, mk, mv, tq=_PROMPT_ROW_TILE)
    y = _out_proj(x, y_ssd, y_attT, None, y_x, w_out, final_norm_w, tm=_PROMPT_ROW_TILE)
    conv_state = xbc[s - (CONV_W - 1):]
    heads = lambda aT: aT.reshape(ATT_HEADS, ATT_HEAD_DIM, s).transpose(2, 0, 1)
    return y, heads(kT), heads(vT), mk, mv, state, conv_state


def _sample_layer(x, cache_k, cache_v, page_table, mem_k, mem_v, state_ssm, state_conv, norm_w, w_in,
                  conv_w, conv_b, dt_bias, a_log, d_skip, ssm_norm_w, w_out, rel_bias, final_norm_w):
    b, T, d = x.shape
    n_pages = page_table.shape[1]
    assert (n_pages * PAGE_SIZE) % MOBA_BLOCK == 0 and n_pages % _KM_PAGES == 0
    assert CONV_W - 1 <= T <= SUBLANES
    xf = x.reshape(b * T, d)
    z, xbc, q, k, v, ga, qx, gx, dtT = _in_proj_sample(xf, norm_w, w_in)
    per = lambda a: a.reshape(b, T, a.shape[-1])
    y_ssd, h_new = _ssd_sample(per(xbc), state_conv, per(dtT.T), per(z), state_ssm,
                               conv_w, conv_b, dt_bias, a_log, d_skip, ssm_norm_w)
    pad8 = lambda a: jnp.pad(per(a), ((0, 0), (0, SUBLANES - T), (0, 0)))
    pool_kt = cache_k.transpose(0, 2, 3, 1)
    pool_vt = cache_v.transpose(0, 2, 3, 1)
    sel = _select_sample(pad8(q), pool_kt, page_table)[:, :, :T, :MOBA_TOPK]
    sel = sel.transpose(0, 2, 1, 3).reshape(-1)
    kk = (np.arange(PAGES_PER_BLOCK)[:, None] * PAGE_SIZE + np.arange(PAGE_SIZE)[None, :])[None]
    tt = np.arange(T)[:, None, None]
    near = _bias_table(rel_bias, _bucket_np(MOBA_BLOCK + tt - kk))
    q_col = (q * (ATT_HEAD_DIM ** -0.5)).reshape(b * T, 1, ATT_WIDTH)
    y_att = _attend_sample(sel, page_table, rel_bias, q_col, per(k).transpose(0, 2, 1),
                           per(v).transpose(0, 2, 1), near, pool_kt, pool_vt)
    y_att = y_att.reshape(b * T, ATT_WIDTH)
    y_x = _xattn_sample(pad8(qx), pad8(gx), mem_k, mem_v)
    y_x = y_x[:, :T].reshape(b * T, X_WIDTH)
    y = _out_proj(xf, y_ssd.reshape(b * T, SSM_WIDTH), y_att, ga, y_x, w_out, final_norm_w, tm=b * T)
    conv_state = per(xbc)[:, T - (CONV_W - 1):]
    return y.reshape(b, T, d), per(k), per(v), h_new, conv_state


def kernel(x_prompt, x_sample, mem_prompt, cache_k, cache_v, page_table, cache_mem_k, cache_mem_v,
           state_ssm, state_conv, norm_w, w_in, conv_w, conv_b, dt_bias, a_log, d_skip, ssm_norm_w,
           mem_norm_w, w_mem_kv, w_out, rel_bias, final_norm_w):
    bp, s, d = x_prompt.shape
    bs, T, _ = x_sample.shape
    depth = w_in.shape[0]
    assert bp == 1 and depth == 1
    l = 0
    y_p, k_p, v_p, mk, mv, ssm_p, conv_p = _prompt_layer(
        x_prompt[0], mem_prompt[0], norm_w[l], w_in[l], conv_w[l], conv_b[l], dt_bias[l], a_log[l],
        d_skip[l], ssm_norm_w[l], mem_norm_w[l], w_mem_kv[l], w_out[l], rel_bias, final_norm_w)
    y_s, k_s, v_s, ssm_s, conv_s = _sample_layer(
        x_sample, cache_k[l], cache_v[l], page_table, cache_mem_k[l], cache_mem_v[l], state_ssm[l],
        state_conv[l], norm_w[l], w_in[l], conv_w[l], conv_b[l], dt_bias[l], a_log[l], d_skip[l],
        ssm_norm_w[l], w_out[l], rel_bias, final_norm_w)
    nm = mem_prompt.shape[1]
    return (y_p[None],
            y_s,
            k_p[None, None],
            v_p[None, None],
            mk.reshape(1, 1, nm, X_HEADS, X_HEAD_DIM),
            mv.reshape(1, 1, nm, X_HEADS, X_HEAD_DIM),
            ssm_p[None, None],
            conv_p[None, None],
            k_s.reshape(1, bs, T, ATT_HEADS, ATT_HEAD_DIM),
            v_s.reshape(1, bs, T, ATT_HEADS, ATT_HEAD_DIM),
            ssm_s[None],
            conv_s[None])
```
